```python
import jax
import jax.numpy as jnp
from jax import lax
import numpy as np

D_MODEL = 4096
BATCH = 1
SEQ = 8192
DEPTH = 2

RWKV_WIDTH = D_MODEL // 2
RWKV_HEAD = 64
RWKV_HEADS = RWKV_WIDTH // RWKV_HEAD
W_LORA = 96
A_LORA = 96
V_LORA = 64
G_LORA = 256
HGRN_WIDTH = D_MODEL - RWKV_WIDTH
HGRN_EXPAND = 128
HGRN_HEADS = HGRN_WIDTH // HGRN_EXPAND
HGRN_VDIM = HGRN_WIDTH // HGRN_HEADS
HGRN_CHUNK = 64
MIN_FORGET = 1e-30
N_EXPERTS = 64
TOP_K = 8
N_GROUPS = 8
TOPK_GROUPS = 4
EXPERT_FF = 320
SHARED_FF = 320
ROUTE_SCALE = 2.5
MASKED_SCORE = -1e4
NORM_EPS = 1e-6
GN_EPS = 64e-5

RWKV_SPLITS = (RWKV_WIDTH, 2 * RWKV_WIDTH, 3 * RWKV_WIDTH, 3 * RWKV_WIDTH + W_LORA, 3 * RWKV_WIDTH + W_LORA + A_LORA)
RWKV_COLS = 3 * RWKV_WIDTH + W_LORA + A_LORA + G_LORA
IN_COLS = RWKV_COLS + 4 * HGRN_WIDTH

kernel_name = "hybrid_rwkv7_hgrn2_moe_adaln"


def rms_norm(x, gain):
    xf = x.astype(jnp.float32)
    y = xf * lax.rsqrt(jnp.mean(xf * xf, axis=-1, keepdims=True) + NORM_EPS)
    return (y * gain.astype(jnp.float32)).astype(x.dtype)


def token_shift(p, mu):
    prev = jnp.pad(p, ((0, 0), (1, 0), (0, 0)))[:, :-1]
    return p + (prev - p) * mu


def split_heads(t, n):
    return t.reshape(t.shape[:-1] + (t.shape[-1] // n, n))


def rwkv7_time_mix(p, v_first, mu, w0, w_up, a0, a_up, g_up, k_k, k_a, r_k, gn_w, gn_b, vres):
    B, T, _ = p.shape
    p = token_shift(p.astype(jnp.float32), mu.astype(jnp.float32))
    r, k, v, w_lo, a_lo, g_lo = jnp.split(p, RWKV_SPLITS, axis=-1)
    w_log = -jax.nn.softplus(-(w0 + jnp.tanh(w_lo) @ w_up)) - 0.5
    decay = jnp.exp(-jnp.exp(w_log))
    a = jax.nn.sigmoid(a0 + a_lo @ a_up)
    g = jax.nn.sigmoid(g_lo) @ g_up
    v_own = v
    if vres is not None:
        v0, v_down, v_up = vres
        v = v + (v_first - v) * jax.nn.sigmoid(v0 + (v @ v_down) @ v_up)
    kk = split_heads(k * k_k, RWKV_HEAD)
    kk = kk / jnp.maximum(jnp.linalg.norm(kk, axis=-1, keepdims=True), 1e-12)
    k = k * (1.0 + (a - 1.0) * k_a)
    r_h, w_h, k_h, v_h, a_h = (split_heads(t, RWKV_HEAD) for t in (r, decay, k, v, a))

    def step(S, inp):
        r_t, w_t, k_t, v_t, kk_t, kka_t = inp
        S = (S * w_t[..., None, :]
             - jnp.einsum('bhvk,bhk->bhv', S, kk_t)[..., None] * kka_t[..., None, :]
             + v_t[..., None] * k_t[..., None, :])
        return S, jnp.einsum('bhvk,bhk->bhv', S, r_t)

    S0 = jnp.zeros((B, RWKV_HEADS, RWKV_HEAD, RWKV_HEAD), jnp.float32)
    xs = tuple(jnp.moveaxis(t, 1, 0) for t in (r_h, w_h, k_h, v_h, kk, kk * a_h))
    _, y = lax.scan(step, S0, xs)
    y = jnp.moveaxis(y, 0, 1)
    mean = jnp.mean(y, axis=-1, keepdims=True)
    var = jnp.mean(jnp.square(y - mean), axis=-1, keepdims=True)
    y = ((y - mean) * lax.rsqrt(var + GN_EPS)).reshape(B, T, RWKV_WIDTH) * gn_w + gn_b
    bonus = jnp.sum(r_h * k_h * r_k, axis=-1, keepdims=True) * v_h
    out = (y + bonus.reshape(B, T, RWKV_WIDTH)) * g
    return out, v_own


def hgrn2_mix(p, lb, norm_gain):
    B, T, _ = p.shape
    p = p.astype(jnp.float32)
    q, f, i, g = jnp.split(p, 4, axis=-1)
    q = jax.nn.silu(q)
    lb = lb.astype(jnp.float32)
    forget = lb + (1.0 - lb) * jax.nn.sigmoid(f)
    log_f = jnp.log(jnp.maximum(forget, MIN_FORGET))
    k = (1.0 - lb) * jax.nn.sigmoid(-f)
    n_chunks = T // HGRN_CHUNK

    def to_chunks(t):
        return t.reshape(B, n_chunks, HGRN_CHUNK, HGRN_HEADS, -1).transpose(1, 0, 3, 2, 4)

    causal = jnp.tril(jnp.ones((HGRN_CHUNK, HGRN_CHUNK), bool))[:, :, None]

    def chunk_step(S, inp):
        q_c, k_c, i_c, lf_c = inp
        b = jnp.cumsum(lf_c, axis=2)
        o_inter = jnp.einsum('bhtk,bhkv->bhtv', q_c * jnp.exp(b), S)
        diff = b[:, :, :, None, :] - b[:, :, None, :, :]
        dec = jnp.where(causal, jnp.exp(jnp.minimum(diff, 0.0)), 0.0)
        scores = jnp.einsum('bhtk,bhsk,bhtsk->bhts', q_c, k_c, dec)
        o = o_inter + jnp.einsum('bhts,bhsv->bhtv', scores, i_c)
        b_last = b[:, :, -1:, :]
        S = (S * jnp.exp(b_last[:, :, 0, :])[..., None]
             + jnp.einsum('bhsk,bhsv->bhkv', k_c * jnp.exp(b_last - b), i_c))
        return S, o

    S0 = jnp.zeros((B, HGRN_HEADS, HGRN_EXPAND, HGRN_VDIM), jnp.float32)
    _, o = lax.scan(chunk_step, S0, tuple(to_chunks(t) for t in (q, k, i, log_f)))
    o = o.transpose(1, 0, 3, 2, 4).reshape(B, T, HGRN_HEADS, HGRN_VDIM)
    o = o * lax.rsqrt(jnp.mean(o * o, axis=-1, keepdims=True) + NORM_EPS)
    return o.reshape(B, T, HGRN_WIDTH) * norm_gain * jax.nn.sigmoid(g)


def moe_ffn(h, router_w, router_bias, w_gate, w_up, w_down, ws_gate, ws_up, ws_down):
    B, T, D = h.shape
    xt = h.reshape(B * T, D)
    scores = jax.nn.sigmoid((xt @ router_w).astype(jnp.float32))
    biased = scores + router_bias.astype(jnp.float32)
    per_group = N_EXPERTS // N_GROUPS
    group_score = jnp.sum(lax.top_k(biased.reshape(-1, N_GROUPS, per_group), 2)[0], axis=-1)
    _, group_idx = lax.top_k(group_score, TOPK_GROUPS)
    group_ok = jnp.sum(jax.nn.one_hot(group_idx, N_GROUPS, dtype=jnp.float32), axis=1) > 0
    cand = jnp.where(jnp.repeat(group_ok, per_group, axis=1), biased, MASKED_SCORE)
    _, idx = lax.top_k(cand, TOP_K)
    wts = jnp.take_along_axis(scores, idx, axis=1)
    wts = wts / jnp.sum(wts, axis=-1, keepdims=True) * ROUTE_SCALE
    gates = jnp.einsum('nk,nke->en', wts, jax.nn.one_hot(idx, N_EXPERTS, dtype=jnp.float32)).astype(h.dtype)

    def expert_step(acc, inp):
        wg, wu, wd, gate_e = inp
        y = (jax.nn.silu(xt @ wg) * (xt @ wu)) @ wd
        return acc + gate_e[:, None] * y, None

    routed, _ = lax.scan(expert_step, jnp.zeros_like(xt), (w_gate, w_up, w_down, gates))
    shared = (jax.nn.silu(xt @ ws_gate) * (xt @ ws_up)) @ ws_down
    return (routed + shared).reshape(B, T, D)


def setup_inputs(seed: int = 0) -> dict:
    key = jax.random.key(seed)
    ks = iter(jax.random.split(key, 40))
    L, D, W, H = DEPTH, D_MODEL, RWKV_WIDTH, HGRN_WIDTH

    def nrm(shape, scale):
        return jax.random.normal(next(ks), shape, jnp.float32) * scale

    def unif(shape, lo, hi):
        return jax.random.uniform(next(ks), shape, jnp.float32, lo, hi)

    return {
        'x': nrm((BATCH, SEQ, D), 1.0),
        'c': nrm((BATCH, D), 1.0),
        'w_mod': nrm((L, D, 6 * D), 0.5 * D ** -0.5),
        'b_mod': nrm((L, 6 * D), 0.02),
        'norm_mix': 1.0 + nrm((L, D), 0.02),
        'norm_ffn': 1.0 + nrm((L, D), 0.02),
        'w_in': nrm((L, D, IN_COLS), D ** -0.5),
        'rwkv_mu': unif((L, RWKV_COLS), 0.0, 1.0),
        'rwkv_w0': unif((L, W), -6.5, -1.5),
        'rwkv_w_up': nrm((L, W_LORA, W), 0.5 * W_LORA ** -0.5),
        'rwkv_a0': nrm((L, W), 0.1),
        'rwkv_a_up': nrm((L, A_LORA, W), A_LORA ** -0.5),
        'rwkv_g_up': nrm((L, G_LORA, W), G_LORA ** -0.5),
        'rwkv_k_k': 0.85 + nrm((L, W), 0.02),
        'rwkv_k_a': 1.0 + nrm((L, W), 0.02),
        'rwkv_r_k': nrm((L, RWKV_HEADS, RWKV_HEAD), 0.1),
        'rwkv_gn_w': 1.0 + nrm((L, W), 0.02),
        'rwkv_gn_b': nrm((L, W), 0.02),
        'rwkv_v0': nrm((L - 1, W), 0.1),
        'rwkv_v_down': nrm((L - 1, W, V_LORA), W ** -0.5),
        'rwkv_v_up': nrm((L - 1, V_LORA, W), V_LORA ** -0.5),
        'hgrn_lower_bounds': nrm((L, H), 0.1),
        'hgrn_norm': 1.0 + nrm((L, H), 0.02),
        'w_out': nrm((L, D, D), D ** -0.5),
        'router_w': nrm((L, D, N_EXPERTS), D ** -0.5),
        'router_bias': nrm((L, N_EXPERTS), 0.01),
        'expert_w_gate': nrm((L, N_EXPERTS, D, EXPERT_FF), D ** -0.5),
        'expert_w_up': nrm((L, N_EXPERTS, D, EXPERT_FF), D ** -0.5),
        'expert_w_down': nrm((L, N_EXPERTS, EXPERT_FF, D), EXPERT_FF ** -0.5),
        'shared_w_gate': nrm((L, D, SHARED_FF), D ** -0.5),
        'shared_w_up': nrm((L, D, SHARED_FF), D ** -0.5),
        'shared_w_down': nrm((L, SHARED_FF, D), SHARED_FF ** -0.5),
        'final_norm': 1.0 + nrm((D,), 0.02),
    }


def reference(x, c, w_mod, b_mod, norm_mix, norm_ffn, w_in, rwkv_mu, rwkv_w0, rwkv_w_up, rwkv_a0,
              rwkv_a_up, rwkv_g_up, rwkv_k_k, rwkv_k_a, rwkv_r_k, rwkv_gn_w, rwkv_gn_b, rwkv_v0,
              rwkv_v_down, rwkv_v_up, hgrn_lower_bounds, hgrn_norm, w_out, router_w, router_bias,
              expert_w_gate, expert_w_up, expert_w_down, shared_w_gate, shared_w_up, shared_w_down,
              final_norm):
    lb_soft = jax.nn.softmax(hgrn_lower_bounds.astype(jnp.float32), axis=0)
    lbs = jnp.cumsum(lb_soft, axis=0) - lb_soft[0]
    cond = jax.nn.silu(c)
    v_first = None
    for l in range(DEPTH):
        mod = cond @ w_mod[l] + b_mod[l]
        sh1, sc1, g1, sh2, sc2, g2 = jnp.split(mod[:, None, :], 6, axis=-1)
        h = rms_norm(x, norm_mix[l]) * (1.0 + sc1) + sh1
        p = h @ w_in[l]
        vres = None if l == 0 else (rwkv_v0[l - 1], rwkv_v_down[l - 1], rwkv_v_up[l - 1])
        y_r, v_l = rwkv7_time_mix(p[..., :RWKV_COLS], v_first, rwkv_mu[l], rwkv_w0[l], rwkv_w_up[l],
                                  rwkv_a0[l], rwkv_a_up[l], rwkv_g_up[l], rwkv_k_k[l], rwkv_k_a[l],
                                  rwkv_r_k[l], rwkv_gn_w[l], rwkv_gn_b[l], vres)
        if l == 0:
            v_first = v_l
        y_h = hgrn2_mix(p[..., RWKV_COLS:], lbs[l], hgrn_norm[l])
        y = jnp.concatenate([y_r, y_h], axis=-1).astype(x.dtype) @ w_out[l]
        x = x + g1 * y
        h = rms_norm(x, norm_ffn[l]) * (1.0 + sc2) + sh2
        x = x + g2 * moe_ffn(h, router_w[l], router_bias[l], expert_w_gate[l], expert_w_up[l],
                             expert_w_down[l], shared_w_gate[l], shared_w_up[l], shared_w_down[l])
    return rms_norm(x, final_norm)
```

```python
import functools

import jax
import jax.numpy as jnp
from jax import lax
from jax.experimental import pallas as pl
from jax.experimental.pallas import tpu as pltpu

F32 = jnp.float32
BF16 = jnp.bfloat16

LANES = 128
VMEM_LIMIT = 56 * 1024 * 1024

RWKV_HEAD = 64
HGRN_HEAD = 128
CHUNK = 64
SUB = 16
LORA_PAD = 128
G_LORA = 256
V_LORA_PAD = 128
N_EXPERTS = 64
N_GROUPS = 8
TOPK_GROUPS = 4
TOP_K = 8
ROUTE_SCALE = 2.5
MASKED_SCORE = -1e4
MIN_FORGET = 1e-30
NORM_EPS = 1e-6
GN_EPS = 64e-5
DECAY_SCALE = 0.6065306597126334

NN = (((1,), (0,)), ((), ()))
NT = (((1,), (1,)), ((), ()))
TN = (((0,), (0,)), ((), ()))


def _mm(a, b, dims=NN):
    return lax.dot_general(a.astype(BF16), b.astype(BF16), dims, preferred_element_type=F32)


def _mmx(a, b, dims=NN):
    return lax.dot_general(a.astype(F32), b.astype(F32), dims, preferred_element_type=F32,
                           precision=lax.Precision.HIGHEST)


def _sigmoid(x):
    return 1.0 / (1.0 + jnp.exp(-x))


def _silu(x):
    return x * _sigmoid(x)


def _cparams(sem):
    return pltpu.CompilerParams(dimension_semantics=sem, vmem_limit_bytes=VMEM_LIMIT)


def _modulation(c, w_mod, b_mod):
    L, D, N = w_mod.shape
    tn = 512
    cond8 = jnp.broadcast_to(c, (8, D))
    out = pl.pallas_call(
        _mod_silu_kernel,
        grid=(L, N // tn),
        in_specs=[pl.BlockSpec((8, D), lambda l, j: (0, 0)),
                  pl.BlockSpec((1, D, tn), lambda l, j: (l, 0, j)),
                  pl.BlockSpec((1, 1, tn), lambda l, j: (l, 0, j))],
        out_specs=pl.BlockSpec((1, 8, tn), lambda l, j: (l, 0, j)),
        out_shape=jax.ShapeDtypeStruct((L, 8, N), F32),
        compiler_params=_cparams(("arbitrary", "arbitrary")),
        name="modulation",
    )(cond8, w_mod, b_mod.reshape(L, 1, N))
    return out[:, 0, :]


def _mod_silu_kernel(c_ref, w_ref, b_ref, o_ref):
    cond = _silu(c_ref[...])
    o_ref[0] = _mm(cond, w_ref[0]) + b_ref[0]


def _normmod(x, gain, shift):
    y = x * lax.rsqrt(jnp.mean(x * x, axis=-1, keepdims=True) + NORM_EPS)
    return y * gain + shift


def _norm_proj_kernel(x_ref, gain_ref, shift_ref, w_ref, o_ref, h_ref):
    @pl.when(pl.program_id(1) == 0)
    def _():
        h_ref[...] = _normmod(x_ref[...], gain_ref[...], shift_ref[...]).astype(BF16)

    o_ref[...] = jnp.dot(h_ref[...], w_ref[...], preferred_element_type=F32)


def _norm_proj(x, gain, shift, w, tm=512, tn=512):
    T, D = x.shape
    N = w.shape[1]
    return pl.pallas_call(
        _norm_proj_kernel,
        grid=(T // tm, N // tn),
        in_specs=[pl.BlockSpec((tm, D), lambda i, j: (i, 0)),
                  pl.BlockSpec((1, D), lambda i, j: (0, 0)),
                  pl.BlockSpec((1, D), lambda i, j: (0, 0)),
                  pl.BlockSpec((D, tn), lambda i, j: (0, j))],
        out_specs=pl.BlockSpec((tm, tn), lambda i, j: (i, j)),
        out_shape=jax.ShapeDtypeStruct((T, N), F32),
        scratch_shapes=[pltpu.VMEM((tm, D), BF16)],
        compiler_params=_cparams(("arbitrary", "arbitrary")),
        name="norm_proj",
    )(x, gain, shift, w)


def _pair_ones():
    r = lax.broadcasted_iota(jnp.int32, (LANES, LANES), 0) // RWKV_HEAD
    c = lax.broadcasted_iota(jnp.int32, (LANES, LANES), 1) // RWKV_HEAD
    return jnp.where(r == c, 1.0, 0.0).astype(F32)


def _rwkv_prep_kernel(has_vres, *refs):
    n_in = 20 if has_vres else 16
    (r_ref, k_ref, v_ref, lo_ref, rp_ref, kp_ref, vp_ref, lop_ref, mu_ref, w0_ref, wup_ref,
     a0_ref, aup_ref, gup_ref, kk_ref, ka_ref) = refs[:16]
    if has_vres:
        v0_ref, vdn_ref, vup_ref, vf_ref = refs[16:20]
    ro_ref, lwo_ref, ko_ref, vo_ref, kko_ref, kkao_ref, go_ref = refs[n_in:]
    first = pl.program_id(0) == 0
    W = r_ref.shape[1]

    def shift(cur_ref, prev_ref, mu):
        cur = cur_ref[...]
        prev_last = jnp.where(first, 0.0, prev_ref[7:8, :])
        rolled = pltpu.roll(cur, 1, axis=0)
        row = lax.broadcasted_iota(jnp.int32, cur.shape, 0)
        prev = jnp.where(row == 0, prev_last, rolled)
        return cur + (prev - cur) * mu

    r = shift(r_ref, rp_ref, mu_ref[:, 0:W])
    k = shift(k_ref, kp_ref, mu_ref[:, W:2 * W])
    v = shift(v_ref, vp_ref, mu_ref[:, 2 * W:3 * W])
    lo = shift(lo_ref, lop_ref, mu_ref[:, 3 * W:])
    w_lo = lo[:, 0:LORA_PAD]
    a_lo = lo[:, LORA_PAD:2 * LORA_PAD]
    g_lo = lo[:, 2 * LORA_PAD:]

    z = w0_ref[...] + _mm(jnp.tanh(w_lo), wup_ref[...])
    lw = -DECAY_SCALE * _sigmoid(z)
    a = _sigmoid(a0_ref[...] + _mm(a_lo, aup_ref[...]))
    g = _mm(_sigmoid(g_lo), gup_ref[...])
    if has_vres:
        mix = _sigmoid(v0_ref[...] + _mm(_mm(v, vdn_ref[...]), vup_ref[...]))
        v = v + (vf_ref[...] - v) * mix

    kkr = k * kk_ref[...]
    sq = kkr * kkr
    ones = _pair_ones()
    ss = jnp.concatenate(
        [_mmx(sq[:, j * LANES:(j + 1) * LANES], ones) for j in range(W // LANES)], axis=1)
    kk = kkr / jnp.maximum(jnp.sqrt(ss), 1e-12)
    k2 = k * (1.0 + (a - 1.0) * ka_ref[...])

    ro_ref[...] = r
    lwo_ref[...] = lw
    ko_ref[...] = k2
    vo_ref[...] = v
    kko_ref[...] = kk
    kkao_ref[...] = kk * a
    go_ref[...] = g


def _rwkv_prep(p, mu_p, w0, wup, a0, aup, gup, k_k, k_a, vres, tp=128):
    T = p.shape[0]
    W = w0.shape[1]
    lo_w = 2 * LORA_PAD + G_LORA
    nb = tp // 8
    lo_blk = 3 * W // lo_w

    def cur(cb, width):
        return pl.BlockSpec((tp, width), lambda i, cb=cb: (i, cb))

    def prev(cb, width):
        return pl.BlockSpec((8, width), lambda i, cb=cb: (jnp.maximum(i * nb - 1, 0), cb))

    def vec(n):
        return pl.BlockSpec((1, n), lambda i: (0, 0))

    def mat(a, b):
        return pl.BlockSpec((a, b), lambda i: (0, 0))

    in_specs = [cur(0, W), cur(1, W), cur(2, W), cur(lo_blk, lo_w),
                prev(0, W), prev(1, W), prev(2, W), prev(lo_blk, lo_w),
                vec(3 * W + lo_w), vec(W), mat(LORA_PAD, W), vec(W), mat(LORA_PAD, W),
                mat(G_LORA, W), vec(W), vec(W)]
    args = [p, p, p, p, p, p, p, p, mu_p, w0, wup, a0, aup, gup, k_k, k_a]
    if vres is not None:
        v0, vdn, vup, v_first = vres
        in_specs += [vec(W), mat(W, V_LORA_PAD), mat(V_LORA_PAD, W),
                     pl.BlockSpec((tp, W), lambda i: (i, 0))]
        args += [v0, vdn, vup, v_first]
    out_spec = pl.BlockSpec((tp, W), lambda i: (i, 0))
    sds = jax.ShapeDtypeStruct((T, W), F32)
    return pl.pallas_call(
        functools.partial(_rwkv_prep_kernel, vres is not None),
        grid=(T // tp,),
        in_specs=in_specs,
        out_specs=[out_spec] * 7,
        out_shape=[sds] * 7,
        compiler_params=_cparams(("arbitrary",)),
        name="rwkv_prep",
    )(*args)


def _rwkv_pair(r, lw, k, v, kk, kka, g, rk, gnw, gnb, st_ref):
    C = r.shape[0]
    C2 = 2 * C
    rowc = lax.broadcasted_iota(jnp.int32, (C, C), 0)
    colc = lax.broadcasted_iota(jnp.int32, (C, C), 1)
    tri = jnp.where(colc <= rowc, 1.0, 0.0).astype(F32)
    c = _mmx(tri, lw)
    cp = c - lw
    c_last = c[C - 1:C, :]
    e_c, e_cp, e_nc, e_dl = jnp.exp(c), jnp.exp(cp), jnp.exp(-c), jnp.exp(c_last - c)

    lane = lax.broadcasted_iota(jnp.int32, (C, LANES), 1)
    m0 = lane < RWKV_HEAD

    def stack(x):
        return jnp.concatenate([jnp.where(m0, x, 0.0), jnp.where(m0, 0.0, x)], axis=0)

    rt, aq, kt, bt = stack(r * e_c), stack(kk * e_cp), stack(k * e_nc), stack(kka * e_nc)
    kw, bw, vs = stack(k * e_dl), stack(kka * e_dl), stack(v)

    row = lax.broadcasted_iota(jnp.int32, (C2, C2), 0)
    col = lax.broadcasted_iota(jnp.int32, (C2, C2), 1)
    t_i, s_j = row % C, col % C
    strict, incl, eye = s_j < t_i, s_j <= t_i, row == col

    l_ab = jnp.where(strict, _mmx(aq, bt, NT), 0.0)
    l_ak = jnp.where(strict, _mmx(aq, kt, NT), 0.0)
    l_rb = jnp.where(incl, _mmx(rt, bt, NT), 0.0)
    l_rk = jnp.where(incl, _mmx(rt, kt, NT), 0.0)

    pw = -l_ab
    inv = jnp.where(eye, 1.0, 0.0) + pw
    n = 2
    while n < C:
        pw = _mmx(pw, pw)
        inv = inv + _mmx(inv, pw)
        n *= 2

    st = st_ref[...]
    u = _mmx(inv, _mmx(aq, st) + _mmx(l_ak, vs))
    y = _mmx(rt, st) - _mmx(l_rb, u) + _mmx(l_rk, vs)
    dmat = jnp.where(eye, jnp.broadcast_to(jnp.exp(c_last), (C2, C2)), 0.0)
    st_ref[...] = _mmx(dmat, st) + _mmx(kw, vs, TN) - _mmx(bw, u, TN)

    mh = (row // C) == (col // RWKV_HEAD)
    inv_n = 1.0 / RWKV_HEAD
    mean = jnp.sum(y, axis=1, keepdims=True) * inv_n
    yc = jnp.where(mh, y - mean, 0.0)
    var = jnp.sum(yc * yc, axis=1, keepdims=True) * inv_n
    yn = yc * lax.rsqrt(var + GN_EPS)
    bonus = jnp.sum(stack(r * k * rk), axis=1, keepdims=True) * vs
    yn_t = yn[:C] + yn[C:]
    bonus_t = bonus[:C] + bonus[C:]
    return (yn_t * gnw + gnb + bonus_t) * g


def _rwkv_rec_kernel(r_ref, lw_ref, k_ref, v_ref, kk_ref, kka_ref, g_ref, rk_ref, gnw_ref, gnb_ref,
                     o_ref, st_ref):
    @pl.when(pl.program_id(1) == 0)
    def _():
        st_ref[...] = jnp.zeros_like(st_ref)

    n_pairs = r_ref.shape[1] // LANES
    outs = []
    for j in range(n_pairs):
        sl = slice(j * LANES, (j + 1) * LANES)
        outs.append(_rwkv_pair(r_ref[:, sl], lw_ref[:, sl], k_ref[:, sl], v_ref[:, sl], kk_ref[:, sl],
                               kka_ref[:, sl], g_ref[:, sl], rk_ref[:, sl], gnw_ref[:, sl],
                               gnb_ref[:, sl], st_ref.at[j]))
    o_ref[...] = jnp.concatenate(outs, axis=1).astype(o_ref.dtype)


def _rwkv_recurrence(r, lw, k, v, kk, kka, g, rk, gnw, gnb, hb_lanes=256):
    T, W = r.shape
    n_pairs = hb_lanes // LANES
    seq = pl.BlockSpec((CHUNK, hb_lanes), lambda h, t: (t, h))
    vec = pl.BlockSpec((1, hb_lanes), lambda h, t: (0, h))
    return pl.pallas_call(
        _rwkv_rec_kernel,
        grid=(W // hb_lanes, T // CHUNK),
        in_specs=[seq] * 7 + [vec] * 3,
        out_specs=seq,
        out_shape=jax.ShapeDtypeStruct((T, W), BF16),
        scratch_shapes=[pltpu.VMEM((n_pairs, LANES, LANES), F32)],
        compiler_params=_cparams(("arbitrary", "arbitrary")),
        name="rwkv_recurrence",
    )(r, lw, k, v, kk, kka, g, rk, gnw, gnb)


def _hgrn_head(q_raw, f_raw, i_c, g_raw, lb, gain, s_ref):
    C = q_raw.shape[0]
    q = _silu(q_raw)
    forget = lb + (1.0 - lb) * _sigmoid(f_raw)
    lf = jnp.log(jnp.maximum(forget, MIN_FORGET))
    kin = (1.0 - lb) * _sigmoid(-f_raw)
    rowc = lax.broadcasted_iota(jnp.int32, (C, C), 0)
    colc = lax.broadcasted_iota(jnp.int32, (C, C), 1)
    tri = jnp.where(colc <= rowc, 1.0, 0.0).astype(F32)
    b = _mmx(tri, lf)

    s = s_ref[...]
    o = _mmx(q * jnp.exp(b), s)

    lane_s = lax.broadcasted_iota(jnp.int32, (SUB, C), 1)
    row_s = lax.broadcasted_iota(jnp.int32, (SUB, C), 0)
    score_rows = []
    for i in range(C // SUB):
        lo, hi = i * SUB, (i + 1) * SUB
        b_i, q_i = b[lo:hi], q[lo:hi]
        blk = jnp.zeros((SUB, C), F32)
        for sidx in range(SUB):
            srow = lo + sidx
            d = jnp.exp(jnp.minimum(b_i - b[srow:srow + 1], 0.0)) * q_i * kin[srow:srow + 1]
            blk = jnp.where(lane_s == srow, jnp.sum(d, axis=1, keepdims=True), blk)
        blk = jnp.where(lane_s <= row_s + lo, blk, 0.0)
        if i > 0:
            b_st = b[lo - 1:lo]
            qs = q_i * jnp.exp(b_i - b_st)
            ks = kin * jnp.exp(jnp.minimum(b_st - b, 0.0))
            blk = blk + jnp.where(lane_s < lo, _mmx(qs, ks, NT), 0.0)
        score_rows.append(blk)
    scores = jnp.concatenate(score_rows, axis=0)
    o = o + _mmx(scores, i_c)

    b_last = b[C - 1:C]
    row = lax.broadcasted_iota(jnp.int32, (LANES, LANES), 0)
    col = lax.broadcasted_iota(jnp.int32, (LANES, LANES), 1)
    dmat = jnp.where(row == col, jnp.broadcast_to(jnp.exp(b_last), (LANES, LANES)), 0.0)
    s_ref[...] = _mmx(dmat, s) + _mmx(kin * jnp.exp(b_last - b), i_c, TN)

    o = o * lax.rsqrt(jnp.mean(o * o, axis=1, keepdims=True) + NORM_EPS)
    return o * gain * _sigmoid(g_raw)


def _hgrn_kernel(q_ref, f_ref, i_ref, g_ref, lb_ref, gain_ref, o_ref, s_ref):
    @pl.when(pl.program_id(1) == 0)
    def _():
        s_ref[...] = jnp.zeros_like(s_ref)

    n_heads = q_ref.shape[1] // LANES
    outs = []
    for j in range(n_heads):
        sl = slice(j * LANES, (j + 1) * LANES)
        outs.append(_hgrn_head(q_ref[:, sl], f_ref[:, sl], i_ref[:, sl], g_ref[:, sl], lb_ref[:, sl],
                               gain_ref[:, sl], s_ref.at[j]))
    o_ref[...] = jnp.concatenate(outs, axis=1).astype(o_ref.dtype)


def _hgrn(p, col0, lb, gain, hb_lanes=256):
    T = p.shape[0]
    W = lb.shape[1]
    nb = W // hb_lanes
    b0 = col0 // hb_lanes

    def sec(n):
        return pl.BlockSpec((CHUNK, hb_lanes), lambda h, t, n=n: (t, b0 + n * nb + h))

    vec = pl.BlockSpec((1, hb_lanes), lambda h, t: (0, h))
    return pl.pallas_call(
        _hgrn_kernel,
        grid=(nb, T // CHUNK),
        in_specs=[sec(0), sec(1), sec(2), sec(3), vec, vec],
        out_specs=pl.BlockSpec((CHUNK, hb_lanes), lambda h, t: (t, h)),
        out_shape=jax.ShapeDtypeStruct((T, W), BF16),
        scratch_shapes=[pltpu.VMEM((hb_lanes // LANES, LANES, LANES), F32)],
        compiler_params=_cparams(("arbitrary", "arbitrary")),
        name="hgrn2",
    )(p, p, p, p, lb, gain)


def _out_proj_kernel(ya_ref, yb_ref, wa_ref, wb_ref, x_ref, gate_ref, o_ref):
    acc = jnp.dot(ya_ref[...], wa_ref[...], preferred_element_type=F32)
    acc = acc + jnp.dot(yb_ref[...], wb_ref[...], preferred_element_type=F32)
    o_ref[...] = x_ref[...] + gate_ref[...] * acc


def _out_proj(ya, yb, w, x, gate, tm=512, tn=1024):
    T, Ka = ya.shape
    Kb = yb.shape[1]
    D = w.shape[1]
    return pl.pallas_call(
        _out_proj_kernel,
        grid=(T // tm, D // tn),
        in_specs=[pl.BlockSpec((tm, Ka), lambda i, j: (i, 0)),
                  pl.BlockSpec((tm, Kb), lambda i, j: (i, 0)),
                  pl.BlockSpec((Ka, tn), lambda i, j: (0, j)),
                  pl.BlockSpec((Kb, tn), lambda i, j: (Ka // Kb, j)),
                  pl.BlockSpec((tm, tn), lambda i, j: (i, j)),
                  pl.BlockSpec((1, tn), lambda i, j: (0, j))],
        out_specs=pl.BlockSpec((tm, tn), lambda i, j: (i, j)),
        out_shape=jax.ShapeDtypeStruct((T, D), F32),
        compiler_params=_cparams(("arbitrary", "arbitrary")),
        name="out_proj",
    )(ya, yb, w, w, x, gate)


def _first_max_onehot(work, axis, size):
    m = jnp.max(work, axis=axis, keepdims=True)
    idx = lax.broadcasted_iota(jnp.int32, work.shape, axis)
    first = jnp.min(jnp.where(work == m, idx, size), axis=axis, keepdims=True)
    return idx == first, m


def _router_kernel(x_ref, gain_ref, shift_ref, rwt_ref, bias_ref, h_ref, gates_ref):
    h = _normmod(x_ref[...], gain_ref[...], shift_ref[...])
    h_ref[...] = h.astype(BF16)
    tm = h.shape[0]
    per_group = N_EXPERTS // N_GROUPS
    logits = _mmx(rwt_ref[...], h, NT)
    scores = _sigmoid(logits)
    biased = scores + bias_ref[...]

    b3 = biased.reshape(N_GROUPS, per_group, tm)
    pick1, m1 = _first_max_onehot(b3, 1, per_group)
    m2 = jnp.max(jnp.where(pick1, -jnp.inf, b3), axis=1, keepdims=True)
    gscore = (m1 + m2).reshape(N_GROUPS, tm)

    work = gscore
    gsel = jnp.zeros_like(gscore)
    for _ in range(TOPK_GROUPS):
        pick, _m = _first_max_onehot(work, 0, N_GROUPS)
        gsel = jnp.where(pick, 1.0, gsel)
        work = jnp.where(pick, -jnp.inf, work)
    ok = jnp.broadcast_to(gsel.reshape(N_GROUPS, 1, tm), (N_GROUPS, per_group, tm)).reshape(N_EXPERTS, tm)
    work = jnp.where(ok > 0.5, biased, MASKED_SCORE)
    picked = jnp.zeros_like(work)
    for _ in range(TOP_K):
        pick, _m = _first_max_onehot(work, 0, N_EXPERTS)
        picked = jnp.where(pick, 1.0, picked)
        work = jnp.where(pick, -jnp.inf, work)
    sel = picked * scores
    gates_t = sel / jnp.sum(sel, axis=0, keepdims=True) * ROUTE_SCALE
    row = lax.broadcasted_iota(jnp.int32, (LANES - N_EXPERTS, tm), 0)
    shared_rows = jnp.where(row == 0, 1.0, 0.0).astype(F32)
    gates_ref[...] = jnp.concatenate([gates_t, shared_rows], axis=0).T


def _router(x, gain, shift, rw_t, bias, tm=256):
    T, D = x.shape
    return pl.pallas_call(
        _router_kernel,
        grid=(T // tm,),
        in_specs=[pl.BlockSpec((tm, D), lambda i: (i, 0)),
                  pl.BlockSpec((1, D), lambda i: (0, 0)),
                  pl.BlockSpec((1, D), lambda i: (0, 0)),
                  pl.BlockSpec((N_EXPERTS, D), lambda i: (0, 0)),
                  pl.BlockSpec((N_EXPERTS, 1), lambda i: (0, 0))],
        out_specs=[pl.BlockSpec((tm, D), lambda i: (i, 0)),
                   pl.BlockSpec((tm, LANES), lambda i: (i, 0))],
        out_shape=[jax.ShapeDtypeStruct((T, D), BF16), jax.ShapeDtypeStruct((T, LANES), F32)],
        compiler_params=_cparams(("arbitrary",)),
        name="router",
    )(x, gain, shift, rw_t, bias)


def _experts_kernel(h_ref, gates_ref, wgu_ref, wd_ref, gate2_ref, o_ref):
    e = pl.program_id(1)
    ff = wd_ref.shape[1]

    @pl.when(e == 0)
    def _():
        o_ref[...] = jnp.zeros_like(o_ref)

    gu = jnp.dot(h_ref[...], wgu_ref[0], preferred_element_type=F32)
    act = _silu(gu[:, :ff]) * gu[:, ff:]
    lane = lax.broadcasted_iota(jnp.int32, gates_ref.shape, 1)
    gate = jnp.sum(jnp.where(lane == e, gates_ref[...], 0.0), axis=1, keepdims=True)
    act = (act * gate).astype(BF16)
    o_ref[...] += jnp.dot(act, wd_ref[0], preferred_element_type=F32)

    @pl.when(e == pl.num_programs(1) - 1)
    def _():
        o_ref[...] = o_ref[...] * gate2_ref[...]


def _experts(h, gates, wgu, wd, gate2, tm=512):
    T, D = h.shape
    E, _, ff2 = wgu.shape
    return pl.pallas_call(
        _experts_kernel,
        grid=(T // tm, E),
        in_specs=[pl.BlockSpec((tm, D), lambda i, e: (i, 0)),
                  pl.BlockSpec((tm, LANES), lambda i, e: (i, 0)),
                  pl.BlockSpec((1, D, ff2), lambda i, e: (e, 0, 0)),
                  pl.BlockSpec((1, ff2 // 2, D), lambda i, e: (e, 0, 0)),
                  pl.BlockSpec((1, D), lambda i, e: (0, 0))],
        out_specs=pl.BlockSpec((tm, D), lambda i, e: (i, 0)),
        out_shape=jax.ShapeDtypeStruct((T, D), F32),
        compiler_params=_cparams(("arbitrary", "arbitrary")),
        name="experts",
    )(h, gates, wgu, wd, gate2)


def _final_norm_kernel(x_ref, g_ref, o_ref):
    x = x_ref[...]
    o_ref[...] = x * lax.rsqrt(jnp.mean(x * x, axis=-1, keepdims=True) + NORM_EPS) * g_ref[...]


def _final_norm(x, gain, tm=512):
    T, D = x.shape
    return pl.pallas_call(
        _final_norm_kernel,
        grid=(T // tm,),
        in_specs=[pl.BlockSpec((tm, D), lambda i: (i, 0)), pl.BlockSpec((1, D), lambda i: (0, 0))],
        out_specs=pl.BlockSpec((tm, D), lambda i: (i, 0)),
        out_shape=jax.ShapeDtypeStruct((T, D), F32),
        compiler_params=_cparams(("arbitrary",)),
        name="final_norm",
    )(x, gain)


def _pad_cols(a, n):
    return jnp.pad(a, ((0, 0), (0, n - a.shape[1])))


def _pad_rows(a, n):
    return jnp.pad(a, ((0, n - a.shape[0]), (0, 0)))


def _pad_rwkv_cols(a, W, w_lora, a_lora):
    c0 = 3 * W
    return jnp.concatenate([a[:, :c0], _pad_cols(a[:, c0:c0 + w_lora], LORA_PAD),
                            _pad_cols(a[:, c0 + w_lora:c0 + w_lora + a_lora], LORA_PAD),
                            a[:, c0 + w_lora + a_lora:]], axis=1)


def kernel(x, c, w_mod, b_mod, norm_mix, norm_ffn, w_in, rwkv_mu, rwkv_w0, rwkv_w_up, rwkv_a0, rwkv_a_up, rwkv_g_up, rwkv_k_k, rwkv_k_a, rwkv_r_k, rwkv_gn_w, rwkv_gn_b, rwkv_v0, rwkv_v_down, rwkv_v_up, hgrn_lower_bounds, hgrn_norm, w_out, router_w, router_bias, expert_w_gate, expert_w_up, expert_w_down, shared_w_gate, shared_w_up, shared_w_down, final_norm):
    B, T, D = x.shape
    L = w_mod.shape[0]
    W = rwkv_w0.shape[1]
    w_lora, a_lora = rwkv_w_up.shape[1], rwkv_a_up.shape[1]
    rwkv_cols = rwkv_mu.shape[1]
    rwkv_cols_p = 3 * W + 2 * LORA_PAD + G_LORA

    lb_soft = jax.nn.softmax(hgrn_lower_bounds.astype(F32), axis=0)
    lbs = jnp.cumsum(lb_soft, axis=0) - lb_soft[0]
    mod = _modulation(c, w_mod, b_mod)

    xs = x.reshape(B * T, D)
    v_first = None
    for l in range(L):
        sh1, sc1, g1, sh2, sc2, g2 = [mod[l, n * D:(n + 1) * D].reshape(1, D) for n in range(6)]
        w_in_p = jnp.concatenate([_pad_rwkv_cols(w_in[l][:, :rwkv_cols], W, w_lora, a_lora),
                                  w_in[l][:, rwkv_cols:]], axis=1).astype(BF16)
        p = _norm_proj(xs, norm_mix[l].reshape(1, D) * (1.0 + sc1), sh1, w_in_p)
        mu_p = _pad_rwkv_cols(rwkv_mu[l].reshape(1, -1), W, w_lora, a_lora)
        vres = None
        if l > 0:
            vres = (rwkv_v0[l - 1].reshape(1, W), _pad_cols(rwkv_v_down[l - 1], V_LORA_PAD).astype(BF16),
                    _pad_rows(rwkv_v_up[l - 1], V_LORA_PAD).astype(BF16), v_first)
        r, lw, k, v, kk, kka, g = _rwkv_prep(
            p, mu_p, rwkv_w0[l].reshape(1, W), _pad_rows(rwkv_w_up[l], LORA_PAD).astype(BF16),
            rwkv_a0[l].reshape(1, W), _pad_rows(rwkv_a_up[l], LORA_PAD).astype(BF16),
            rwkv_g_up[l].astype(BF16), rwkv_k_k[l].reshape(1, W), rwkv_k_a[l].reshape(1, W), vres)
        if l == 0:
            v_first = v
        y_r = _rwkv_recurrence(r, lw, k, v, kk, kka, g, rwkv_r_k[l].reshape(1, W),
                               rwkv_gn_w[l].reshape(1, W), rwkv_gn_b[l].reshape(1, W))
        y_h = _hgrn(p, rwkv_cols_p, lbs[l].reshape(1, -1), hgrn_norm[l].reshape(1, -1))
        xs = _out_proj(y_r, y_h, w_out[l].astype(BF16), xs, g1)
        h2, gates = _router(xs, norm_ffn[l].reshape(1, D) * (1.0 + sc2), sh2,
                            router_w[l].T, router_bias[l].reshape(-1, 1))
        wgu = jnp.concatenate([
            jnp.concatenate([expert_w_gate[l], expert_w_up[l]], axis=2),
            jnp.concatenate([shared_w_gate[l], shared_w_up[l]], axis=1)[None]], axis=0).astype(BF16)
        wd = jnp.concatenate([expert_w_down[l], shared_w_down[l][None]], axis=0).astype(BF16)
        xs = xs + _experts(h2, gates, wgu, wd, g2)
    return _final_norm(xs, final_norm.reshape(1, D)).reshape(B, T, D)
```

```python
import functools

import jax
import jax.numpy as jnp
from jax import lax
from jax.experimental import pallas as pl
from jax.experimental.pallas import tpu as pltpu

F32 = jnp.float32
BF16 = jnp.bfloat16

LANES = 128
VMEM_LIMIT = 56 * 1024 * 1024

RWKV_HEAD = 64
HGRN_HEAD = 128
CHUNK = 64
SUB = 16
LORA_PAD = 128
G_LORA = 256
V_LORA_PAD = 128
N_EXPERTS = 64
N_GROUPS = 8
TOPK_GROUPS = 4
TOP_K = 8
ROUTE_SCALE = 2.5
MASKED_SCORE = -1e4
MIN_FORGET = 1e-30
NORM_EPS = 1e-6
GN_EPS = 64e-5
DECAY_SCALE = 0.6065306597126334

NN = (((1,), (0,)), ((), ()))
NT = (((1,), (1,)), ((), ()))
TN = (((0,), (0,)), ((), ()))


def _mm(a, b, dims=NN):
    return lax.dot_general(a.astype(BF16), b.astype(BF16), dims, preferred_element_type=F32)


def _mmx(a, b, dims=NN):
    return lax.dot_general(a.astype(F32), b.astype(F32), dims, preferred_element_type=F32,
                           precision=lax.Precision.HIGHEST)


def _split3(x):
    hi = x.astype(BF16)
    r1 = x - hi.astype(F32)
    mid = r1.astype(BF16)
    lo = (r1 - mid.astype(F32)).astype(BF16)
    return hi, mid, lo


def _mm_left01(a01, b, dims=NN):
    a = a01.astype(BF16)
    d = functools.partial(lax.dot_general, dimension_numbers=dims, preferred_element_type=F32)
    hi, mid, lo = _split3(b)
    return d(a, hi) + d(a, mid) + d(a, lo)


def _mm_right01(a, b01, dims=NN):
    b = b01.astype(BF16)
    d = functools.partial(lax.dot_general, dimension_numbers=dims, preferred_element_type=F32)
    hi, mid, lo = _split3(a)
    return d(hi, b) + d(mid, b) + d(lo, b)


def _sigmoid(x):
    return 1.0 / (1.0 + jnp.exp(-x))


def _silu(x):
    return x * _sigmoid(x)


def _cparams(sem):
    return pltpu.CompilerParams(dimension_semantics=sem, vmem_limit_bytes=VMEM_LIMIT)


def _modulation(c, w_mod, b_mod):
    L, D, N = w_mod.shape
    tn = 512
    cond8 = jnp.broadcast_to(c, (8, D))
    out = pl.pallas_call(
        _mod_silu_kernel,
        grid=(L, N // tn),
        in_specs=[pl.BlockSpec((8, D), lambda l, j: (0, 0)),
                  pl.BlockSpec((1, D, tn), lambda l, j: (l, 0, j)),
                  pl.BlockSpec((1, 1, tn), lambda l, j: (l, 0, j))],
        out_specs=pl.BlockSpec((1, 8, tn), lambda l, j: (l, 0, j)),
        out_shape=jax.ShapeDtypeStruct((L, 8, N), F32),
        compiler_params=_cparams(("arbitrary", "arbitrary")),
        name="modulation",
    )(cond8, w_mod, b_mod.reshape(L, 1, N))
    return out[:, 0, :]


def _mod_silu_kernel(c_ref, w_ref, b_ref, o_ref):
    cond = _silu(c_ref[...])
    o_ref[0] = _mm(cond, w_ref[0]) + b_ref[0]


def _normmod(x, gain, shift):
    y = x * lax.rsqrt(jnp.mean(x * x, axis=-1, keepdims=True) + NORM_EPS)
    return y * gain + shift


def _norm_proj_kernel(x_ref, gain_ref, shift_ref, w_ref, o_ref, h_ref):
    @pl.when(pl.program_id(1) == 0)
    def _():
        h_ref[...] = _normmod(x_ref[...], gain_ref[...], shift_ref[...]).astype(BF16)

    o_ref[...] = jnp.dot(h_ref[...], w_ref[...], preferred_element_type=F32)


def _norm_proj(x, gain, shift, w, tm=512, tn=512):
    T, D = x.shape
    N = w.shape[1]
    return pl.pallas_call(
        _norm_proj_kernel,
        grid=(T // tm, N // tn),
        in_specs=[pl.BlockSpec((tm, D), lambda i, j: (i, 0)),
                  pl.BlockSpec((1, D), lambda i, j: (0, 0)),
                  pl.BlockSpec((1, D), lambda i, j: (0, 0)),
                  pl.BlockSpec((D, tn), lambda i, j: (0, j))],
        out_specs=pl.BlockSpec((tm, tn), lambda i, j: (i, j)),
        out_shape=jax.ShapeDtypeStruct((T, N), F32),
        scratch_shapes=[pltpu.VMEM((tm, D), BF16)],
        compiler_params=_cparams(("arbitrary", "arbitrary")),
        name="norm_proj",
    )(x, gain, shift, w)


def _pair_ones():
    r = lax.broadcasted_iota(jnp.int32, (LANES, LANES), 0) // RWKV_HEAD
    c = lax.broadcasted_iota(jnp.int32, (LANES, LANES), 1) // RWKV_HEAD
    return jnp.where(r == c, 1.0, 0.0).astype(F32)


def _rwkv_prep_kernel(has_vres, *refs):
    n_in = 20 if has_vres else 16
    (r_ref, k_ref, v_ref, lo_ref, rp_ref, kp_ref, vp_ref, lop_ref, mu_ref, w0_ref, wup_ref,
     a0_ref, aup_ref, gup_ref, kk_ref, ka_ref) = refs[:16]
    if has_vres:
        v0_ref, vdn_ref, vup_ref, vf_ref = refs[16:20]
    ro_ref, lwo_ref, ko_ref, vo_ref, kko_ref, kkao_ref, go_ref = refs[n_in:]
    first = pl.program_id(0) == 0
    W = r_ref.shape[1]

    def shift(cur_ref, prev_ref, mu):
        cur = cur_ref[...]
        prev_last = jnp.where(first, 0.0, prev_ref[7:8, :])
        rolled = pltpu.roll(cur, 1, axis=0)
        row = lax.broadcasted_iota(jnp.int32, cur.shape, 0)
        prev = jnp.where(row == 0, prev_last, rolled)
        return cur + (prev - cur) * mu

    r = shift(r_ref, rp_ref, mu_ref[:, 0:W])
    k = shift(k_ref, kp_ref, mu_ref[:, W:2 * W])
    v = shift(v_ref, vp_ref, mu_ref[:, 2 * W:3 * W])
    lo = shift(lo_ref, lop_ref, mu_ref[:, 3 * W:])
    w_lo = lo[:, 0:LORA_PAD]
    a_lo = lo[:, LORA_PAD:2 * LORA_PAD]
    g_lo = lo[:, 2 * LORA_PAD:]

    z = w0_ref[...] + _mm(jnp.tanh(w_lo), wup_ref[...])
    lw = -DECAY_SCALE * _sigmoid(z)
    a = _sigmoid(a0_ref[...] + _mm(a_lo, aup_ref[...]))
    g = _mm(_sigmoid(g_lo), gup_ref[...])
    if has_vres:
        mix = _sigmoid(v0_ref[...] + _mm(_mm(v, vdn_ref[...]), vup_ref[...]))
        v = v + (vf_ref[...] - v) * mix

    kkr = k * kk_ref[...]
    sq = kkr * kkr
    ones = _pair_ones()
    ss = jnp.concatenate(
        [_mm_right01(sq[:, j * LANES:(j + 1) * LANES], ones) for j in range(W // LANES)], axis=1)
    kk = kkr / jnp.maximum(jnp.sqrt(ss), 1e-12)
    k2 = k * (1.0 + (a - 1.0) * ka_ref[...])

    ro_ref[...] = r
    lwo_ref[...] = lw
    ko_ref[...] = k2
    vo_ref[...] = v
    kko_ref[...] = kk
    kkao_ref[...] = kk * a
    go_ref[...] = g


def _rwkv_prep(p, mu_p, w0, wup, a0, aup, gup, k_k, k_a, vres, tp=128):
    T = p.shape[0]
    W = w0.shape[1]
    lo_w = 2 * LORA_PAD + G_LORA
    nb = tp // 8
    lo_blk = 3 * W // lo_w

    def cur(cb, width):
        return pl.BlockSpec((tp, width), lambda i, cb=cb: (i, cb))

    def prev(cb, width):
        return pl.BlockSpec((8, width), lambda i, cb=cb: (jnp.maximum(i * nb - 1, 0), cb))

    def vec(n):
        return pl.BlockSpec((1, n), lambda i: (0, 0))

    def mat(a, b):
        return pl.BlockSpec((a, b), lambda i: (0, 0))

    in_specs = [cur(0, W), cur(1, W), cur(2, W), cur(lo_blk, lo_w),
                prev(0, W), prev(1, W), prev(2, W), prev(lo_blk, lo_w),
                vec(3 * W + lo_w), vec(W), mat(LORA_PAD, W), vec(W), mat(LORA_PAD, W),
                mat(G_LORA, W), vec(W), vec(W)]
    args = [p, p, p, p, p, p, p, p, mu_p, w0, wup, a0, aup, gup, k_k, k_a]
    if vres is not None:
        v0, vdn, vup, v_first = vres
        in_specs += [vec(W), mat(W, V_LORA_PAD), mat(V_LORA_PAD, W),
                     pl.BlockSpec((tp, W), lambda i: (i, 0))]
        args += [v0, vdn, vup, v_first]
    out_spec = pl.BlockSpec((tp, W), lambda i: (i, 0))
    sds = jax.ShapeDtypeStruct((T, W), F32)
    return pl.pallas_call(
        functools.partial(_rwkv_prep_kernel, vres is not None),
        grid=(T // tp,),
        in_specs=in_specs,
        out_specs=[out_spec] * 7,
        out_shape=[sds] * 7,
        compiler_params=_cparams(("arbitrary",)),
        name="rwkv_prep",
    )(*args)


def _tri_incl(n):
    row = lax.broadcasted_iota(jnp.int32, (n, n), 0)
    col = lax.broadcasted_iota(jnp.int32, (n, n), 1)
    return jnp.where(col <= row, 1.0, 0.0).astype(BF16)


def _rwkv_pair(r, c, lw, k, v, kk, kka, g, rk, gnw, gnb, st):
    C = r.shape[0]
    C2 = 2 * C
    cp = c - lw
    c_last = c[C - 1:C, :]
    e_c, e_cp, e_nc, e_dl = jnp.exp(c), jnp.exp(cp), jnp.exp(-c), jnp.exp(c_last - c)

    lane = lax.broadcasted_iota(jnp.int32, (C, LANES), 1)
    m0 = lane < RWKV_HEAD

    def stack(x):
        return jnp.concatenate([jnp.where(m0, x, 0.0), jnp.where(m0, 0.0, x)], axis=0)

    aq_rt = jnp.concatenate([stack(kk * e_cp), stack(r * e_c)], axis=0).astype(BF16)
    bt_kt = jnp.concatenate([stack(kka * e_nc), stack(k * e_nc)], axis=0).astype(BF16)
    kw_bw = jnp.concatenate([stack(k * e_dl), stack(kka * e_dl)], axis=0).astype(BF16)
    vs = stack(v)
    vs_b = vs.astype(BF16)

    row = lax.broadcasted_iota(jnp.int32, (C2, C2), 0)
    col = lax.broadcasted_iota(jnp.int32, (C2, C2), 1)
    t_i, s_j = row % C, col % C
    strict, incl, eye = s_j < t_i, s_j <= t_i, row == col

    yield
    p = _mm(aq_rt, bt_kt, NT)
    yield
    l_ab = jnp.where(strict, p[:C2, :C2], 0.0)
    l_ak = jnp.where(strict, p[:C2, C2:], 0.0)
    l_rb = jnp.where(incl, p[C2:, :C2], 0.0)
    l_rk = jnp.where(incl, p[C2:, C2:], 0.0)

    pw = -l_ab
    inv = jnp.where(eye, 1.0, 0.0) + pw
    n = 2
    while n < C:
        pw = _mm(pw, pw)
        yield
        inv = inv + _mm(inv, pw)
        n *= 2

    xy = _mm(aq_rt, st) + _mm(jnp.concatenate([l_ak, l_rk], axis=0), vs_b)
    yield
    u = _mm(inv, xy[:C2])
    yield
    y = xy[C2:] - _mm(l_rb, u)
    w_col = jnp.broadcast_to(jnp.exp(c_last), (C2, C2)).T
    st_new = st * w_col + _mm(kw_bw, jnp.concatenate([vs, -u], axis=0), TN)
    yield

    mh = (row // C) == (col // RWKV_HEAD)
    inv_n = 1.0 / RWKV_HEAD
    mean = jnp.sum(y, axis=1, keepdims=True) * inv_n
    yc = jnp.where(mh, y - mean, 0.0)
    var = jnp.sum(yc * yc, axis=1, keepdims=True) * inv_n
    yn = yc * lax.rsqrt(var + GN_EPS)
    bonus = jnp.sum(stack(r * k * rk), axis=1, keepdims=True) * vs
    yn_t = yn[:C] + yn[C:]
    bonus_t = bonus[:C] + bonus[C:]
    return (yn_t * gnw + gnb + bonus_t) * g, st_new


def _interleave(gens):
    results = [None] * len(gens)
    live = list(enumerate(gens))
    while live:
        still = []
        for j, gen in live:
            try:
                next(gen)
                still.append((j, gen))
            except StopIteration as stop:
                results[j] = stop.value
        live = still
    return results


def _rwkv_rec_kernel(r_ref, lw_ref, k_ref, v_ref, kk_ref, kka_ref, g_ref, rk_ref, gnw_ref, gnb_ref,
                     o_ref, st_ref):
    @pl.when(pl.program_id(1) == 0)
    def _():
        st_ref[...] = jnp.zeros_like(st_ref)

    rows, lanes = r_ref.shape
    tri = _tri_incl(CHUNK)

    def chunk(ci, carry):
        rs = pl.ds(pl.multiple_of(ci * CHUNK, CHUNK), CHUNK)
        lw_all = lw_ref[rs, :]
        c_all = _mm_left01(tri, lw_all)
        r, k, v, kk, kka, g = (ref[rs, :] for ref in (r_ref, k_ref, v_ref, kk_ref, kka_ref, g_ref))
        rk, gnw, gnb = rk_ref[...], gnw_ref[...], gnb_ref[...]
        gens = []
        for j in range(lanes // LANES):
            sl = slice(j * LANES, (j + 1) * LANES)
            gens.append(_rwkv_pair(r[:, sl], c_all[:, sl], lw_all[:, sl], k[:, sl], v[:, sl], kk[:, sl],
                                   kka[:, sl], g[:, sl], rk[:, sl], gnw[:, sl], gnb[:, sl], st_ref[j]))
        results = _interleave(gens)
        for j, (_, st_new) in enumerate(results):
            st_ref[j] = st_new
        o_ref[rs, :] = jnp.concatenate([out for out, _ in results], axis=1).astype(o_ref.dtype)
        return carry

    lax.fori_loop(0, rows // CHUNK, chunk, 0)


def _rwkv_recurrence(r, lw, k, v, kk, kka, g, rk, gnw, gnb, hb_lanes=512, rows=512):
    T, W = r.shape
    rows = min(rows, T)
    seq = pl.BlockSpec((rows, hb_lanes), lambda h, t: (t, h))
    vec = pl.BlockSpec((1, hb_lanes), lambda h, t: (0, h))
    return pl.pallas_call(
        _rwkv_rec_kernel,
        grid=(W // hb_lanes, T // rows),
        in_specs=[seq] * 7 + [vec] * 3,
        out_specs=seq,
        out_shape=jax.ShapeDtypeStruct((T, W), BF16),
        scratch_shapes=[pltpu.VMEM((hb_lanes // LANES, LANES, LANES), F32)],
        compiler_params=_cparams(("arbitrary", "arbitrary")),
        name="rwkv_recurrence",
    )(r, lw, k, v, kk, kka, g, rk, gnw, gnb)


def _hgrn_head(q_raw, f_raw, i_c, g_raw, lb, gain, tri, s):
    C = q_raw.shape[0]
    q = _silu(q_raw)
    forget = lb + (1.0 - lb) * _sigmoid(f_raw)
    lf = jnp.log(jnp.maximum(forget, MIN_FORGET))
    kin = (1.0 - lb) * _sigmoid(-f_raw)
    b = _mm_left01(tri, lf)

    o = _mm(q * jnp.exp(b), s)

    lane_s = lax.broadcasted_iota(jnp.int32, (SUB, C), 1)
    row_s = lax.broadcasted_iota(jnp.int32, (SUB, C), 0)
    score_rows = []
    for i in range(C // SUB):
        lo, hi = i * SUB, (i + 1) * SUB
        b_i, q_i = b[lo:hi], q[lo:hi]
        blk = jnp.zeros((SUB, C), F32)
        for sidx in range(SUB):
            srow = lo + sidx
            d = jnp.exp(jnp.minimum(b_i - b[srow:srow + 1], 0.0)) * q_i * kin[srow:srow + 1]
            blk = jnp.where(lane_s == srow, jnp.sum(d, axis=1, keepdims=True), blk)
        blk = jnp.where(lane_s <= row_s + lo, blk, 0.0)
        if i > 0:
            b_st = b[lo - 1:lo]
            qs = q_i * jnp.exp(b_i - b_st)
            ks = kin * jnp.exp(jnp.minimum(b_st - b, 0.0))
            blk = blk + jnp.where(lane_s < lo, _mm(qs, ks, NT), 0.0)
        score_rows.append(blk)
    scores = jnp.concatenate(score_rows, axis=0)
    o = o + _mm(scores, i_c)

    b_last = b[C - 1:C]
    f_col = jnp.broadcast_to(jnp.exp(b_last), (LANES, LANES)).T
    s_new = s * f_col + _mm(kin * jnp.exp(b_last - b), i_c, TN)

    o = o * lax.rsqrt(jnp.mean(o * o, axis=1, keepdims=True) + NORM_EPS)
    return o * gain * _sigmoid(g_raw), s_new


def _hgrn_kernel(q_ref, f_ref, i_ref, g_ref, lb_ref, gain_ref, o_ref, s_ref):
    @pl.when(pl.program_id(1) == 0)
    def _():
        s_ref[...] = jnp.zeros_like(s_ref)

    rows, lanes = q_ref.shape
    tri = _tri_incl(CHUNK)

    def chunk(ci, carry):
        rs = pl.ds(pl.multiple_of(ci * CHUNK, CHUNK), CHUNK)
        q, f, i_c, g = (ref[rs, :] for ref in (q_ref, f_ref, i_ref, g_ref))
        lb, gain = lb_ref[...], gain_ref[...]
        results = []
        for j in range(lanes // LANES):
            sl = slice(j * LANES, (j + 1) * LANES)
            results.append(_hgrn_head(q[:, sl], f[:, sl], i_c[:, sl], g[:, sl], lb[:, sl], gain[:, sl],
                                      tri, s_ref[j]))
        for j, (_, s_new) in enumerate(results):
            s_ref[j] = s_new
        o_ref[rs, :] = jnp.concatenate([out for out, _ in results], axis=1).astype(o_ref.dtype)
        return carry

    lax.fori_loop(0, rows // CHUNK, chunk, 0)


def _hgrn(p, col0, lb, gain, hb_lanes=512, rows=512):
    T = p.shape[0]
    W = lb.shape[1]
    rows = min(rows, T)
    nb = W // hb_lanes
    b0 = col0 // hb_lanes

    def sec(n):
        return pl.BlockSpec((rows, hb_lanes), lambda h, t, n=n: (t, b0 + n * nb + h))

    vec = pl.BlockSpec((1, hb_lanes), lambda h, t: (0, h))
    return pl.pallas_call(
        _hgrn_kernel,
        grid=(nb, T // rows),
        in_specs=[sec(0), sec(1), sec(2), sec(3), vec, vec],
        out_specs=pl.BlockSpec((rows, hb_lanes), lambda h, t: (t, h)),
        out_shape=jax.ShapeDtypeStruct((T, W), BF16),
        scratch_shapes=[pltpu.VMEM((hb_lanes // LANES, LANES, LANES), F32)],
        compiler_params=_cparams(("arbitrary", "arbitrary")),
        name="hgrn2",
    )(p, p, p, p, lb, gain)


def _out_proj_kernel(ya_ref, yb_ref, wa_ref, wb_ref, x_ref, gate_ref, o_ref):
    acc = jnp.dot(ya_ref[...], wa_ref[...], preferred_element_type=F32)
    acc = acc + jnp.dot(yb_ref[...], wb_ref[...], preferred_element_type=F32)
    o_ref[...] = x_ref[...] + gate_ref[...] * acc


def _out_proj(ya, yb, w, x, gate, tm=512, tn=1024):
    T, Ka = ya.shape
    Kb = yb.shape[1]
    D = w.shape[1]
    return pl.pallas_call(
        _out_proj_kernel,
        grid=(T // tm, D // tn),
        in_specs=[pl.BlockSpec((tm, Ka), lambda i, j: (i, 0)),
                  pl.BlockSpec((tm, Kb), lambda i, j: (i, 0)),
                  pl.BlockSpec((Ka, tn), lambda i, j: (0, j)),
                  pl.BlockSpec((Kb, tn), lambda i, j: (Ka // Kb, j)),
                  pl.BlockSpec((tm, tn), lambda i, j: (i, j)),
                  pl.BlockSpec((1, tn), lambda i, j: (0, j))],
        out_specs=pl.BlockSpec((tm, tn), lambda i, j: (i, j)),
        out_shape=jax.ShapeDtypeStruct((T, D), F32),
        compiler_params=_cparams(("arbitrary", "arbitrary")),
        name="out_proj",
    )(ya, yb, w, w, x, gate)


def _first_max_onehot(work, axis, size):
    m = jnp.max(work, axis=axis, keepdims=True)
    idx = lax.broadcasted_iota(jnp.int32, work.shape, axis)
    first = jnp.min(jnp.where(work == m, idx, size), axis=axis, keepdims=True)
    return idx == first, m


def _router_kernel(x_ref, gain_ref, shift_ref, rwt_ref, bias_ref, h_ref, gates_ref):
    h = _normmod(x_ref[...], gain_ref[...], shift_ref[...])
    h_ref[...] = h.astype(BF16)
    tm = h.shape[0]
    per_group = N_EXPERTS // N_GROUPS
    logits = _mmx(rwt_ref[...], h, NT)
    scores = _sigmoid(logits)
    biased = scores + bias_ref[...]

    b3 = biased.reshape(N_GROUPS, per_group, tm)
    pick1, m1 = _first_max_onehot(b3, 1, per_group)
    m2 = jnp.max(jnp.where(pick1, -jnp.inf, b3), axis=1, keepdims=True)
    gscore = (m1 + m2).reshape(N_GROUPS, tm)

    work = gscore
    gsel = jnp.zeros_like(gscore)
    for _ in range(TOPK_GROUPS):
        pick, _m = _first_max_onehot(work, 0, N_GROUPS)
        gsel = jnp.where(pick, 1.0, gsel)
        work = jnp.where(pick, -jnp.inf, work)
    ok = jnp.broadcast_to(gsel.reshape(N_GROUPS, 1, tm), (N_GROUPS, per_group, tm)).reshape(N_EXPERTS, tm)
    work = jnp.where(ok > 0.5, biased, MASKED_SCORE)
    picked = jnp.zeros_like(work)
    for _ in range(TOP_K):
        pick, _m = _first_max_onehot(work, 0, N_EXPERTS)
        picked = jnp.where(pick, 1.0, picked)
        work = jnp.where(pick, -jnp.inf, work)
    sel = picked * scores
    gates_t = sel / jnp.sum(sel, axis=0, keepdims=True) * ROUTE_SCALE
    row = lax.broadcasted_iota(jnp.int32, (LANES - N_EXPERTS, tm), 0)
    shared_rows = jnp.where(row == 0, 1.0, 0.0).astype(F32)
    gates_ref[...] = jnp.concatenate([gates_t, shared_rows], axis=0).T


def _router(x, gain, shift, rw_t, bias, tm=256):
    T, D = x.shape
    return pl.pallas_call(
        _router_kernel,
        grid=(T // tm,),
        in_specs=[pl.BlockSpec((tm, D), lambda i: (i, 0)),
                  pl.BlockSpec((1, D), lambda i: (0, 0)),
                  pl.BlockSpec((1, D), lambda i: (0, 0)),
                  pl.BlockSpec((N_EXPERTS, D), lambda i: (0, 0)),
                  pl.BlockSpec((N_EXPERTS, 1), lambda i: (0, 0))],
        out_specs=[pl.BlockSpec((tm, D), lambda i: (i, 0)),
                   pl.BlockSpec((tm, LANES), lambda i: (i, 0))],
        out_shape=[jax.ShapeDtypeStruct((T, D), BF16), jax.ShapeDtypeStruct((T, LANES), F32)],
        compiler_params=_cparams(("arbitrary",)),
        name="router",
    )(x, gain, shift, rw_t, bias)


def _experts_kernel(h_ref, gates_ref, wgu_ref, wd_ref, gate2_ref, o_ref):
    e = pl.program_id(1)
    ff = wd_ref.shape[1]

    @pl.when(e == 0)
    def _():
        o_ref[...] = jnp.zeros_like(o_ref)

    gu = jnp.dot(h_ref[...], wgu_ref[0], preferred_element_type=F32)
    act = _silu(gu[:, :ff]) * gu[:, ff:]
    lane = lax.broadcasted_iota(jnp.int32, gates_ref.shape, 1)
    gate = jnp.sum(jnp.where(lane == e, gates_ref[...], 0.0), axis=1, keepdims=True)
    act = (act * gate).astype(BF16)
    o_ref[...] += jnp.dot(act, wd_ref[0], preferred_element_type=F32)

    @pl.when(e == pl.num_programs(1) - 1)
    def _():
        o_ref[...] = o_ref[...] * gate2_ref[...]


def _experts(h, gates, wgu, wd, gate2, tm=512):
    T, D = h.shape
    E, _, ff2 = wgu.shape
    return pl.pallas_call(
        _experts_kernel,
        grid=(T // tm, E),
        in_specs=[pl.BlockSpec((tm, D), lambda i, e: (i, 0)),
                  pl.BlockSpec((tm, LANES), lambda i, e: (i, 0)),
                  pl.BlockSpec((1, D, ff2), lambda i, e: (e, 0, 0)),
                  pl.BlockSpec((1, ff2 // 2, D), lambda i, e: (e, 0, 0)),
                  pl.BlockSpec((1, D), lambda i, e: (0, 0))],
        out_specs=pl.BlockSpec((tm, D), lambda i, e: (i, 0)),
        out_shape=jax.ShapeDtypeStruct((T, D), F32),
        compiler_params=_cparams(("arbitrary", "arbitrary")),
        name="experts",
    )(h, gates, wgu, wd, gate2)


def _final_norm_kernel(x_ref, g_ref, o_ref):
    x = x_ref[...]
    o_ref[...] = x * lax.rsqrt(jnp.mean(x * x, axis=-1, keepdims=True) + NORM_EPS) * g_ref[...]


def _final_norm(x, gain, tm=512):
    T, D = x.shape
    return pl.pallas_call(
        _final_norm_kernel,
        grid=(T // tm,),
        in_specs=[pl.BlockSpec((tm, D), lambda i: (i, 0)), pl.BlockSpec((1, D), lambda i: (0, 0))],
        out_specs=pl.BlockSpec((tm, D), lambda i: (i, 0)),
        out_shape=jax.ShapeDtypeStruct((T, D), F32),
        compiler_params=_cparams(("arbitrary",)),
        name="final_norm",
    )(x, gain)


def _pad_cols(a, n):
    return jnp.pad(a, ((0, 0), (0, n - a.shape[1])))


def _pad_rows(a, n):
    return jnp.pad(a, ((0, n - a.shape[0]), (0, 0)))


def _pad_rwkv_cols(a, W, w_lora, a_lora):
    c0 = 3 * W
    return jnp.concatenate([a[:, :c0], _pad_cols(a[:, c0:c0 + w_lora], LORA_PAD),
                            _pad_cols(a[:, c0 + w_lora:c0 + w_lora + a_lora], LORA_PAD),
                            a[:, c0 + w_lora + a_lora:]], axis=1)


def kernel(x, c, w_mod, b_mod, norm_mix, norm_ffn, w_in, rwkv_mu, rwkv_w0, rwkv_w_up, rwkv_a0, rwkv_a_up, rwkv_g_up, rwkv_k_k, rwkv_k_a, rwkv_r_k, rwkv_gn_w, rwkv_gn_b, rwkv_v0, rwkv_v_down, rwkv_v_up, hgrn_lower_bounds, hgrn_norm, w_out, router_w, router_bias, expert_w_gate, expert_w_up, expert_w_down, shared_w_gate, shared_w_up, shared_w_down, final_norm):
    B, T, D = x.shape
    L = w_mod.shape[0]
    W = rwkv_w0.shape[1]
    w_lora, a_lora = rwkv_w_up.shape[1], rwkv_a_up.shape[1]
    rwkv_cols = rwkv_mu.shape[1]
    rwkv_cols_p = 3 * W + 2 * LORA_PAD + G_LORA

    lb_soft = jax.nn.softmax(hgrn_lower_bounds.astype(F32), axis=0)
    lbs = jnp.cumsum(lb_soft, axis=0) - lb_soft[0]
    mod = _modulation(c, w_mod, b_mod)

    xs = x.reshape(B * T, D)
    v_first = None
    for l in range(L):
        sh1, sc1, g1, sh2, sc2, g2 = [mod[l, n * D:(n + 1) * D].reshape(1, D) for n in range(6)]
        w_in_p = jnp.concatenate([_pad_rwkv_cols(w_in[l][:, :rwkv_cols], W, w_lora, a_lora),
                                  w_in[l][:, rwkv_cols:]], axis=1).astype(BF16)
        p = _norm_proj(xs, norm_mix[l].reshape(1, D) * (1.0 + sc1), sh1, w_in_p)
        mu_p = _pad_rwkv_cols(rwkv_mu[l].reshape(1, -1), W, w_lora, a_lora)
        vres = None
        if l > 0:
            vres = (rwkv_v0[l - 1].reshape(1, W), _pad_cols(rwkv_v_down[l - 1], V_LORA_PAD).astype(BF16),
                    _pad_rows(rwkv_v_up[l - 1], V_LORA_PAD).astype(BF16), v_first)
        r, lw, k, v, kk, kka, g = _rwkv_prep(
            p, mu_p, rwkv_w0[l].reshape(1, W), _pad_rows(rwkv_w_up[l], LORA_PAD).astype(BF16),
            rwkv_a0[l].reshape(1, W), _pad_rows(rwkv_a_up[l], LORA_PAD).astype(BF16),
            rwkv_g_up[l].astype(BF16), rwkv_k_k[l].reshape(1, W), rwkv_k_a[l].reshape(1, W), vres)
        if l == 0:
            v_first = v
        y_r = _rwkv_recurrence(r, lw, k, v, kk, kka, g, rwkv_r_k[l].reshape(1, W),
                               rwkv_gn_w[l].reshape(1, W), rwkv_gn_b[l].reshape(1, W))
        y_h = _hgrn(p, rwkv_cols_p, lbs[l].reshape(1, -1), hgrn_norm[l].reshape(1, -1))
        xs = _out_proj(y_r, y_h, w_out[l].astype(BF16), xs, g1)
        h2, gates = _router(xs, norm_ffn[l].reshape(1, D) * (1.0 + sc2), sh2,
                            router_w[l].T, router_bias[l].reshape(-1, 1))
        wgu = jnp.concatenate([
            jnp.concatenate([expert_w_gate[l], expert_w_up[l]], axis=2),
            jnp.concatenate([shared_w_gate[l], shared_w_up[l]], axis=1)[None]], axis=0).astype(BF16)
        wd = jnp.concatenate([expert_w_down[l], shared_w_down[l][None]], axis=0).astype(BF16)
        xs = xs + _experts(h2, gates, wgu, wd, g2)
    return _final_norm(xs, final_norm.reshape(1, D)).reshape(B, T, D)
```

```python
import functools
import math

import jax
import jax.numpy as jnp
from jax import lax
from jax.experimental import pallas as pl
from jax.experimental.pallas import tpu as pltpu

F32 = jnp.float32
BF16 = jnp.bfloat16

LANES = 128
VMEM_LIMIT = 56 * 1024 * 1024

RWKV_HEAD = 64
HGRN_HEAD = 128
CHUNK = 64
SUB = 16
LORA_PAD = 128
G_LORA = 256
V_LORA_PAD = 128
N_EXPERTS = 64
N_GROUPS = 8
TOPK_GROUPS = 4
TOP_K = 8
ROUTE_SCALE = 2.5
MASKED_SCORE = -1e4
MIN_FORGET = 1e-30
NORM_EPS = 1e-6
GN_EPS = 64e-5
DECAY_SCALE = 0.6065306597126334

NN = (((1,), (0,)), ((), ()))
NT = (((1,), (1,)), ((), ()))
TN = (((0,), (0,)), ((), ()))


def _mm(a, b, dims=NN):
    return lax.dot_general(a.astype(BF16), b.astype(BF16), dims, preferred_element_type=F32)


def _mmx(a, b, dims=NN):
    return lax.dot_general(a.astype(F32), b.astype(F32), dims, preferred_element_type=F32,
                           precision=lax.Precision.HIGHEST)


def _split3(x):
    hi = x.astype(BF16)
    r1 = x - hi.astype(F32)
    mid = r1.astype(BF16)
    lo = (r1 - mid.astype(F32)).astype(BF16)
    return hi, mid, lo


def _mm_left01(a01, b, dims=NN):
    a = a01.astype(BF16)
    d = functools.partial(lax.dot_general, dimension_numbers=dims, preferred_element_type=F32)
    hi, mid, lo = _split3(b)
    return d(a, hi) + d(a, mid) + d(a, lo)


def _mm_right01(a, b01, dims=NN):
    b = b01.astype(BF16)
    d = functools.partial(lax.dot_general, dimension_numbers=dims, preferred_element_type=F32)
    hi, mid, lo = _split3(a)
    return d(hi, b) + d(mid, b) + d(lo, b)


def _sigmoid(x):
    return 1.0 / (1.0 + jnp.exp(-x))


def _silu(x):
    return x * _sigmoid(x)


def _cparams(sem):
    return pltpu.CompilerParams(dimension_semantics=sem, vmem_limit_bytes=VMEM_LIMIT)


def _modulation(c, w_mod, b_mod):
    L, D, N = w_mod.shape
    tn = 512
    cond8 = jnp.broadcast_to(c, (8, D))
    out = pl.pallas_call(
        _mod_silu_kernel,
        grid=(L, N // tn),
        in_specs=[pl.BlockSpec((8, D), lambda l, j: (0, 0)),
                  pl.BlockSpec((1, D, tn), lambda l, j: (l, 0, j)),
                  pl.BlockSpec((1, 1, tn), lambda l, j: (l, 0, j))],
        out_specs=pl.BlockSpec((1, 8, tn), lambda l, j: (l, 0, j)),
        out_shape=jax.ShapeDtypeStruct((L, 8, N), F32),
        compiler_params=_cparams(("arbitrary", "arbitrary")),
        name="modulation",
    )(cond8, w_mod, b_mod.reshape(L, 1, N))
    return out[:, 0, :]


def _mod_silu_kernel(c_ref, w_ref, b_ref, o_ref):
    cond = _silu(c_ref[...])
    o_ref[0] = _mm(cond, w_ref[0]) + b_ref[0]


def _normmod(x, gain, shift):
    y = x * lax.rsqrt(jnp.mean(x * x, axis=-1, keepdims=True) + NORM_EPS)
    return y * gain + shift


def _norm_proj_kernel(x_ref, gain_ref, shift_ref, w_ref, o_ref, h_ref):
    @pl.when(pl.program_id(1) == 0)
    def _():
        h_ref[...] = _normmod(x_ref[...], gain_ref[...], shift_ref[...]).astype(BF16)

    o_ref[...] = jnp.dot(h_ref[...], w_ref[...], preferred_element_type=F32)


def _norm_proj(x, gain, shift, w, tm=512, tn=512):
    T, D = x.shape
    N = w.shape[1]
    return pl.pallas_call(
        _norm_proj_kernel,
        grid=(T // tm, N // tn),
        in_specs=[pl.BlockSpec((tm, D), lambda i, j: (i, 0)),
                  pl.BlockSpec((1, D), lambda i, j: (0, 0)),
                  pl.BlockSpec((1, D), lambda i, j: (0, 0)),
                  pl.BlockSpec((D, tn), lambda i, j: (0, j))],
        out_specs=pl.BlockSpec((tm, tn), lambda i, j: (i, j)),
        out_shape=jax.ShapeDtypeStruct((T, N), F32),
        scratch_shapes=[pltpu.VMEM((tm, D), BF16)],
        compiler_params=_cparams(("arbitrary", "arbitrary")),
        name="norm_proj",
    )(x, gain, shift, w)


def _pair_ones():
    r = lax.broadcasted_iota(jnp.int32, (LANES, LANES), 0) // RWKV_HEAD
    c = lax.broadcasted_iota(jnp.int32, (LANES, LANES), 1) // RWKV_HEAD
    return jnp.where(r == c, 1.0, 0.0).astype(F32)


def _rwkv_prep_kernel(has_vres, *refs):
    n_in = 20 if has_vres else 16
    (r_ref, k_ref, v_ref, lo_ref, rp_ref, kp_ref, vp_ref, lop_ref, mu_ref, w0_ref, wup_ref,
     a0_ref, aup_ref, gup_ref, kk_ref, ka_ref) = refs[:16]
    if has_vres:
        v0_ref, vdn_ref, vup_ref, vf_ref = refs[16:20]
    ro_ref, lwo_ref, ko_ref, vo_ref, kko_ref, kkao_ref, go_ref = refs[n_in:]
    first = pl.program_id(0) == 0
    W = r_ref.shape[1]

    def shift(cur_ref, prev_ref, mu):
        cur = cur_ref[...]
        prev_last = jnp.where(first, 0.0, prev_ref[7:8, :])
        rolled = pltpu.roll(cur, 1, axis=0)
        row = lax.broadcasted_iota(jnp.int32, cur.shape, 0)
        prev = jnp.where(row == 0, prev_last, rolled)
        return cur + (prev - cur) * mu

    r = shift(r_ref, rp_ref, mu_ref[:, 0:W])
    k = shift(k_ref, kp_ref, mu_ref[:, W:2 * W])
    v = shift(v_ref, vp_ref, mu_ref[:, 2 * W:3 * W])
    lo = shift(lo_ref, lop_ref, mu_ref[:, 3 * W:])
    w_lo = lo[:, 0:LORA_PAD]
    a_lo = lo[:, LORA_PAD:2 * LORA_PAD]
    g_lo = lo[:, 2 * LORA_PAD:]

    z = w0_ref[...] + _mm(jnp.tanh(w_lo), wup_ref[...])
    lw = -DECAY_SCALE * _sigmoid(z)
    a = _sigmoid(a0_ref[...] + _mm(a_lo, aup_ref[...]))
    g = _mm(_sigmoid(g_lo), gup_ref[...])
    if has_vres:
        mix = _sigmoid(v0_ref[...] + _mm(_mm(v, vdn_ref[...]), vup_ref[...]))
        v = v + (vf_ref[...] - v) * mix

    kkr = k * kk_ref[...]
    sq = kkr * kkr
    ones = _pair_ones()
    ss = jnp.concatenate(
        [_mm_right01(sq[:, j * LANES:(j + 1) * LANES], ones) for j in range(W // LANES)], axis=1)
    kk = kkr / jnp.maximum(jnp.sqrt(ss), 1e-12)
    k2 = k * (1.0 + (a - 1.0) * ka_ref[...])

    ro_ref[...] = r
    lwo_ref[...] = lw
    ko_ref[...] = k2
    vo_ref[...] = v
    kko_ref[...] = kk
    kkao_ref[...] = kk * a
    go_ref[...] = g


def _rwkv_prep(p, mu_p, w0, wup, a0, aup, gup, k_k, k_a, vres, tp=128):
    T = p.shape[0]
    W = w0.shape[1]
    lo_w = 2 * LORA_PAD + G_LORA
    nb = tp // 8
    lo_blk = 3 * W // lo_w

    def cur(cb, width):
        return pl.BlockSpec((tp, width), lambda i, cb=cb: (i, cb))

    def prev(cb, width):
        return pl.BlockSpec((8, width), lambda i, cb=cb: (jnp.maximum(i * nb - 1, 0), cb))

    def vec(n):
        return pl.BlockSpec((1, n), lambda i: (0, 0))

    def mat(a, b):
        return pl.BlockSpec((a, b), lambda i: (0, 0))

    in_specs = [cur(0, W), cur(1, W), cur(2, W), cur(lo_blk, lo_w),
                prev(0, W), prev(1, W), prev(2, W), prev(lo_blk, lo_w),
                vec(3 * W + lo_w), vec(W), mat(LORA_PAD, W), vec(W), mat(LORA_PAD, W),
                mat(G_LORA, W), vec(W), vec(W)]
    args = [p, p, p, p, p, p, p, p, mu_p, w0, wup, a0, aup, gup, k_k, k_a]
    if vres is not None:
        v0, vdn, vup, v_first = vres
        in_specs += [vec(W), mat(W, V_LORA_PAD), mat(V_LORA_PAD, W),
                     pl.BlockSpec((tp, W), lambda i: (i, 0))]
        args += [v0, vdn, vup, v_first]
    out_spec = pl.BlockSpec((tp, W), lambda i: (i, 0))
    sds = jax.ShapeDtypeStruct((T, W), F32)
    return pl.pallas_call(
        functools.partial(_rwkv_prep_kernel, vres is not None),
        grid=(T // tp,),
        in_specs=in_specs,
        out_specs=[out_spec] * 7,
        out_shape=[sds] * 7,
        compiler_params=_cparams(("arbitrary",)),
        name="rwkv_prep",
    )(*args)


def _tri_incl(n):
    row = lax.broadcasted_iota(jnp.int32, (n, n), 0)
    col = lax.broadcasted_iota(jnp.int32, (n, n), 1)
    return jnp.where(col <= row, 1.0, 0.0).astype(BF16)


def _rwkv_pair(r, c, lw, k, v, kk, kka, g, rk, gnw, gnb, st):
    C = r.shape[0]
    C2 = 2 * C
    cp = c - lw
    c_last = c[C - 1:C, :]
    e_c, e_cp, e_nc, e_dl = jnp.exp(c), jnp.exp(cp), jnp.exp(-c), jnp.exp(c_last - c)

    lane = lax.broadcasted_iota(jnp.int32, (C, LANES), 1)
    m0 = lane < RWKV_HEAD

    def stack(x):
        return jnp.concatenate([jnp.where(m0, x, 0.0), jnp.where(m0, 0.0, x)], axis=0)

    aq_rt = jnp.concatenate([stack(kk * e_cp), stack(r * e_c)], axis=0).astype(BF16)
    bt_kt = jnp.concatenate([stack(kka * e_nc), stack(k * e_nc)], axis=0).astype(BF16)
    kw_bw = jnp.concatenate([stack(k * e_dl), stack(kka * e_dl)], axis=0).astype(BF16)
    vs = stack(v)
    vs_b = vs.astype(BF16)

    row = lax.broadcasted_iota(jnp.int32, (C2, C2), 0)
    col = lax.broadcasted_iota(jnp.int32, (C2, C2), 1)
    t_i, s_j = row % C, col % C
    strict, incl, eye = s_j < t_i, s_j <= t_i, row == col

    yield
    p = _mm(aq_rt, bt_kt, NT)
    yield
    l_ab = jnp.where(strict, p[:C2, :C2], 0.0)
    l_ak = jnp.where(strict, p[:C2, C2:], 0.0)
    l_rb = jnp.where(incl, p[C2:, :C2], 0.0)
    l_rk = jnp.where(incl, p[C2:, C2:], 0.0)

    pw = -l_ab
    inv = jnp.where(eye, 1.0, 0.0) + pw
    n = 2
    while n < C:
        pw = _mm(pw, pw)
        yield
        inv = inv + _mm(inv, pw)
        n *= 2

    xy = _mm(aq_rt, st) + _mm(jnp.concatenate([l_ak, l_rk], axis=0), vs_b)
    yield
    u = _mm(inv, xy[:C2])
    yield
    y = xy[C2:] - _mm(l_rb, u)
    w_col = jnp.broadcast_to(jnp.exp(c_last), (C2, C2)).T
    st_new = st * w_col + _mm(kw_bw, jnp.concatenate([vs, -u], axis=0), TN)
    yield

    mh = (row // C) == (col // RWKV_HEAD)
    inv_n = 1.0 / RWKV_HEAD
    mean = jnp.sum(y, axis=1, keepdims=True) * inv_n
    yc = jnp.where(mh, y - mean, 0.0)
    var = jnp.sum(yc * yc, axis=1, keepdims=True) * inv_n
    yn = yc * lax.rsqrt(var + GN_EPS)
    bonus = jnp.sum(stack(r * k * rk), axis=1, keepdims=True) * vs
    yn_t = yn[:C] + yn[C:]
    bonus_t = bonus[:C] + bonus[C:]
    return (yn_t * gnw + gnb + bonus_t) * g, st_new


def _interleave(gens):
    results = [None] * len(gens)
    live = list(enumerate(gens))
    while live:
        still = []
        for j, gen in live:
            try:
                next(gen)
                still.append((j, gen))
            except StopIteration as stop:
                results[j] = stop.value
        live = still
    return results


def _rwkv_rec_kernel(r_ref, lw_ref, k_ref, v_ref, kk_ref, kka_ref, g_ref, rk_ref, gnw_ref, gnb_ref,
                     o_ref, st_ref):
    @pl.when(pl.program_id(1) == 0)
    def _():
        st_ref[...] = jnp.zeros_like(st_ref)

    rows, lanes = r_ref.shape
    tri = _tri_incl(CHUNK)

    def chunk(ci, carry):
        rs = pl.ds(pl.multiple_of(ci * CHUNK, CHUNK), CHUNK)
        lw_all = lw_ref[rs, :]
        c_all = _mm_left01(tri, lw_all)
        r, k, v, kk, kka, g = (ref[rs, :] for ref in (r_ref, k_ref, v_ref, kk_ref, kka_ref, g_ref))
        rk, gnw, gnb = rk_ref[...], gnw_ref[...], gnb_ref[...]
        gens = []
        for j in range(lanes // LANES):
            sl = slice(j * LANES, (j + 1) * LANES)
            gens.append(_rwkv_pair(r[:, sl], c_all[:, sl], lw_all[:, sl], k[:, sl], v[:, sl], kk[:, sl],
                                   kka[:, sl], g[:, sl], rk[:, sl], gnw[:, sl], gnb[:, sl], st_ref[j]))
        results = _interleave(gens)
        for j, (_, st_new) in enumerate(results):
            st_ref[j] = st_new
        o_ref[rs, :] = jnp.concatenate([out for out, _ in results], axis=1).astype(o_ref.dtype)
        return carry

    lax.fori_loop(0, rows // CHUNK, chunk, 0)


def _rwkv_recurrence(r, lw, k, v, kk, kka, g, rk, gnw, gnb, hb_lanes=512, rows=512):
    T, W = r.shape
    rows = min(rows, T)
    seq = pl.BlockSpec((rows, hb_lanes), lambda h, t: (t, h))
    vec = pl.BlockSpec((1, hb_lanes), lambda h, t: (0, h))
    return pl.pallas_call(
        _rwkv_rec_kernel,
        grid=(W // hb_lanes, T // rows),
        in_specs=[seq] * 7 + [vec] * 3,
        out_specs=seq,
        out_shape=jax.ShapeDtypeStruct((T, W), BF16),
        scratch_shapes=[pltpu.VMEM((hb_lanes // LANES, LANES, LANES), F32)],
        compiler_params=_cparams(("arbitrary", "arbitrary")),
        name="rwkv_recurrence",
    )(r, lw, k, v, kk, kka, g, rk, gnw, gnb)


def _hgrn_head(q_raw, f_raw, i_c, g_raw, lb, gain, tri, s):
    C = q_raw.shape[0]
    q = _silu(q_raw)
    forget = lb + (1.0 - lb) * _sigmoid(f_raw)
    lf = jnp.log(jnp.maximum(forget, MIN_FORGET))
    kin = (1.0 - lb) * _sigmoid(-f_raw)
    b = _mm_left01(tri, lf)

    o = _mm(q * jnp.exp(b), s)

    lane_s = lax.broadcasted_iota(jnp.int32, (SUB, C), 1)
    row_s = lax.broadcasted_iota(jnp.int32, (SUB, C), 0)
    score_rows = []
    for i in range(C // SUB):
        lo, hi = i * SUB, (i + 1) * SUB
        b_i, q_i = b[lo:hi], q[lo:hi]
        blk = jnp.zeros((SUB, C), F32)
        for sidx in range(SUB):
            srow = lo + sidx
            d = jnp.exp(jnp.minimum(b_i - b[srow:srow + 1], 0.0)) * q_i * kin[srow:srow + 1]
            blk = jnp.where(lane_s == srow, jnp.sum(d, axis=1, keepdims=True), blk)
        blk = jnp.where(lane_s <= row_s + lo, blk, 0.0)
        if i > 0:
            b_st = b[lo - 1:lo]
            qs = q_i * jnp.exp(b_i - b_st)
            ks = kin * jnp.exp(jnp.minimum(b_st - b, 0.0))
            blk = blk + jnp.where(lane_s < lo, _mm(qs, ks, NT), 0.0)
        score_rows.append(blk)
    scores = jnp.concatenate(score_rows, axis=0)
    o = o + _mm(scores, i_c)

    b_last = b[C - 1:C]
    f_col = jnp.broadcast_to(jnp.exp(b_last), (LANES, LANES)).T
    s_new = s * f_col + _mm(kin * jnp.exp(b_last - b), i_c, TN)

    o = o * lax.rsqrt(jnp.mean(o * o, axis=1, keepdims=True) + NORM_EPS)
    return o * gain * _sigmoid(g_raw), s_new


def _hgrn_kernel(q_ref, f_ref, i_ref, g_ref, lb_ref, gain_ref, o_ref, s_ref):
    @pl.when(pl.program_id(1) == 0)
    def _():
        s_ref[...] = jnp.zeros_like(s_ref)

    rows, lanes = q_ref.shape
    tri = _tri_incl(CHUNK)

    def chunk(ci, carry):
        rs = pl.ds(pl.multiple_of(ci * CHUNK, CHUNK), CHUNK)
        q, f, i_c, g = (ref[rs, :] for ref in (q_ref, f_ref, i_ref, g_ref))
        lb, gain = lb_ref[...], gain_ref[...]
        results = []
        for j in range(lanes // LANES):
            sl = slice(j * LANES, (j + 1) * LANES)
            results.append(_hgrn_head(q[:, sl], f[:, sl], i_c[:, sl], g[:, sl], lb[:, sl], gain[:, sl],
                                      tri, s_ref[j]))
        for j, (_, s_new) in enumerate(results):
            s_ref[j] = s_new
        o_ref[rs, :] = jnp.concatenate([out for out, _ in results], axis=1).astype(o_ref.dtype)
        return carry

    lax.fori_loop(0, rows // CHUNK, chunk, 0)


def _hgrn(p, col0, lb, gain, hb_lanes=512, rows=512):
    T = p.shape[0]
    W = lb.shape[1]
    rows = min(rows, T)
    nb = W // hb_lanes
    b0 = col0 // hb_lanes

    def sec(n):
        return pl.BlockSpec((rows, hb_lanes), lambda h, t, n=n: (t, b0 + n * nb + h))

    vec = pl.BlockSpec((1, hb_lanes), lambda h, t: (0, h))
    return pl.pallas_call(
        _hgrn_kernel,
        grid=(nb, T // rows),
        in_specs=[sec(0), sec(1), sec(2), sec(3), vec, vec],
        out_specs=pl.BlockSpec((rows, hb_lanes), lambda h, t: (t, h)),
        out_shape=jax.ShapeDtypeStruct((T, W), BF16),
        scratch_shapes=[pltpu.VMEM((hb_lanes // LANES, LANES, LANES), F32)],
        compiler_params=_cparams(("arbitrary", "arbitrary")),
        name="hgrn2",
    )(p, p, p, p, lb, gain)


def _out_proj_kernel(ya_ref, yb_ref, wa_ref, wb_ref, x_ref, gate_ref, o_ref):
    acc = jnp.dot(ya_ref[...], wa_ref[...], preferred_element_type=F32)
    acc = acc + jnp.dot(yb_ref[...], wb_ref[...], preferred_element_type=F32)
    o_ref[...] = x_ref[...] + gate_ref[...] * acc


def _out_proj(ya, yb, w, x, gate, tm=512, tn=1024):
    T, Ka = ya.shape
    Kb = yb.shape[1]
    D = w.shape[1]
    return pl.pallas_call(
        _out_proj_kernel,
        grid=(T // tm, D // tn),
        in_specs=[pl.BlockSpec((tm, Ka), lambda i, j: (i, 0)),
                  pl.BlockSpec((tm, Kb), lambda i, j: (i, 0)),
                  pl.BlockSpec((Ka, tn), lambda i, j: (0, j)),
                  pl.BlockSpec((Kb, tn), lambda i, j: (Ka // Kb, j)),
                  pl.BlockSpec((tm, tn), lambda i, j: (i, j)),
                  pl.BlockSpec((1, tn), lambda i, j: (0, j))],
        out_specs=pl.BlockSpec((tm, tn), lambda i, j: (i, j)),
        out_shape=jax.ShapeDtypeStruct((T, D), F32),
        compiler_params=_cparams(("arbitrary", "arbitrary")),
        name="out_proj",
    )(ya, yb, w, w, x, gate)


def _first_max_onehot(work, axis, size):
    m = jnp.max(work, axis=axis, keepdims=True)
    idx = lax.broadcasted_iota(jnp.int32, work.shape, axis)
    first = jnp.min(jnp.where(work == m, idx, size), axis=axis, keepdims=True)
    return idx == first, m


def _pitch(s_per):
    return s_per + 8 if s_per % 16 == 0 else s_per


def _store_token_major(ref, val):
    n, d = val.shape
    pitch = ref.shape[0] // n
    for s in range(d // LANES):
        ref[pl.ds(s, n, stride=pitch), :] = val[:, s * LANES:(s + 1) * LANES]
    for s in range(d // LANES, pitch):
        ref[pl.ds(s, n, stride=pitch), :] = jnp.zeros((n, LANES), F32)


def _router_kernel(x_ref, gain_ref, shift_ref, rwt_ref, bias_ref,
                   h_ref, wts_ref, idx_ref, rank_ref, cnt_ref, carry_ref):
    @pl.when(pl.program_id(0) == 0)
    def _():
        carry_ref[...] = jnp.zeros_like(carry_ref)

    h = _normmod(x_ref[...], gain_ref[...], shift_ref[...])
    _store_token_major(h_ref, h)
    tm = h.shape[0]
    per_group = N_EXPERTS // N_GROUPS
    logits = _mmx(rwt_ref[...], h, NT)
    scores = _sigmoid(logits)
    biased = scores + bias_ref[...]

    b3 = biased.reshape(N_GROUPS, per_group, tm)
    pick1, m1 = _first_max_onehot(b3, 1, per_group)
    m2 = jnp.max(jnp.where(pick1, -jnp.inf, b3), axis=1, keepdims=True)
    gscore = (m1 + m2).reshape(N_GROUPS, tm)

    work = gscore
    gsel = jnp.zeros_like(gscore)
    for _ in range(TOPK_GROUPS):
        pick, _m = _first_max_onehot(work, 0, N_GROUPS)
        gsel = jnp.where(pick, 1.0, gsel)
        work = jnp.where(pick, -jnp.inf, work)
    ok = jnp.broadcast_to(gsel.reshape(N_GROUPS, 1, tm), (N_GROUPS, per_group, tm)).reshape(N_EXPERTS, tm)
    work = jnp.where(ok > 0.5, biased, MASKED_SCORE)
    picks = []
    for _ in range(TOP_K):
        pick, _m = _first_max_onehot(work, 0, N_EXPERTS)
        picks.append(pick)
        work = jnp.where(pick, -jnp.inf, work)
    picked = functools.reduce(lambda a, b: a + b, [jnp.where(p, 1.0, 0.0) for p in picks])
    sel = picked * scores
    gates_t = sel / jnp.sum(sel, axis=0, keepdims=True) * ROUTE_SCALE

    ra = lax.broadcasted_iota(jnp.int32, (tm, tm), 0)
    rb = lax.broadcasted_iota(jnp.int32, (tm, tm), 1)
    before = jnp.where(ra < rb, 1.0, 0.0).astype(BF16)
    carry = carry_ref[:, 0:1]
    rank_full = _mm(picked, before) + carry
    e_iota = lax.broadcasted_iota(jnp.int32, (N_EXPERTS, tm), 0).astype(F32)

    def slot_rows(table):
        rows = [jnp.sum(jnp.where(p, table, 0.0), axis=0, keepdims=True) for p in picks]
        return jnp.concatenate(rows, axis=0)

    idx_ref[...] = slot_rows(e_iota).astype(jnp.int32)
    rank_ref[...] = slot_rows(rank_full).astype(jnp.int32)
    w_rows = slot_rows(gates_t)
    wts_ref[...] = jnp.concatenate([w_rows, jnp.zeros((LANES - TOP_K, tm), F32)], axis=0).T
    new_carry = carry + jnp.sum(picked, axis=1, keepdims=True)
    carry_ref[...] = jnp.broadcast_to(new_carry, carry_ref.shape)
    cnt_ref[...] = jnp.broadcast_to(new_carry, cnt_ref.shape)


def _router(x, gain, shift, rw_t, bias, tm=256):
    T, D = x.shape
    tm = min(tm, T)
    P = _pitch(D // LANES)
    return pl.pallas_call(
        _router_kernel,
        grid=(T // tm,),
        in_specs=[pl.BlockSpec((tm, D), lambda i: (i, 0)),
                  pl.BlockSpec((1, D), lambda i: (0, 0)),
                  pl.BlockSpec((1, D), lambda i: (0, 0)),
                  pl.BlockSpec((N_EXPERTS, D), lambda i: (0, 0)),
                  pl.BlockSpec((N_EXPERTS, 1), lambda i: (0, 0))],
        out_specs=[pl.BlockSpec((tm * P, LANES), lambda i: (i, 0)),
                   pl.BlockSpec((tm, LANES), lambda i: (i, 0)),
                   pl.BlockSpec((TOP_K, tm), lambda i: (0, i)),
                   pl.BlockSpec((TOP_K, tm), lambda i: (0, i)),
                   pl.BlockSpec((N_EXPERTS, LANES), lambda i: (0, 0))],
        out_shape=[jax.ShapeDtypeStruct((T * P, LANES), F32),
                   jax.ShapeDtypeStruct((T, LANES), F32),
                   jax.ShapeDtypeStruct((TOP_K, T), jnp.int32),
                   jax.ShapeDtypeStruct((TOP_K, T), jnp.int32),
                   jax.ShapeDtypeStruct((N_EXPERTS, LANES), F32)],
        scratch_shapes=[pltpu.VMEM((N_EXPERTS, LANES), F32)],
        compiler_params=_cparams(("arbitrary",)),
        name="router",
    )(x, gain, shift, rw_t, bias)


def _dispatch_kernel(P, dest_ref, pad_start_ref, pad_len_ref, nu_ref, h_hbm, xs_hbm, zero_ref, sem, pad_sem):
    i = pl.program_id(0)
    tc = dest_ref.shape[0] // TOP_K
    tile_rows = zero_ref.shape[0]

    def slab(ref, row):
        return ref.at[pl.ds(pl.multiple_of(row * P, math.gcd(P, 8)), P)]

    def token(n, carry):
        src = slab(h_hbm, i * tc + n)
        for j in range(TOP_K):
            pltpu.make_async_copy(src, slab(xs_hbm, dest_ref[n * TOP_K + j]), sem).start()
        return carry

    lax.fori_loop(0, tc, token, 0)

    @pl.when(i == 0)
    def _():
        zero_ref[...] = jnp.zeros_like(zero_ref)
        zero_slab = zero_ref.at[pl.ds(0, P)]

        def expert(e, carry):
            def fill(r, c):
                pltpu.make_async_copy(zero_slab, slab(xs_hbm, pad_start_ref[e] + r), pad_sem).start()
                return c

            def drain(r, c):
                pltpu.make_async_copy(zero_slab, slab(xs_hbm, 0), pad_sem).wait()
                return c

            lax.fori_loop(0, pad_len_ref[e], fill, 0)
            lax.fori_loop(0, pad_len_ref[e], drain, 0)
            return carry

        lax.fori_loop(0, N_EXPERTS, expert, 0)

        def tile_dst(t):
            return xs_hbm.at[pl.ds(pl.multiple_of(t * tile_rows, 8), tile_rows)]

        def fill_tile(t, c):
            pltpu.make_async_copy(zero_ref, tile_dst(t), pad_sem).start()
            return c

        def drain_tile(t, c):
            pltpu.make_async_copy(zero_ref, tile_dst(0), pad_sem).wait()
            return c

        n_tiles = xs_hbm.shape[0] // tile_rows
        lax.fori_loop(nu_ref[0], n_tiles, fill_tile, 0)
        lax.fori_loop(nu_ref[0], n_tiles, drain_tile, 0)

    n_rows = TOP_K * tc * P
    pltpu.make_async_copy(h_hbm.at[pl.ds(0, n_rows)], xs_hbm.at[pl.ds(0, n_rows)], sem).wait()


def _dispatch(dest_flat, pad_start, pad_len, n_used, h_tok, n_tiles, tm_e, tc):
    T = dest_flat.shape[0] // TOP_K
    P = h_tok.shape[0] // T
    return pl.pallas_call(
        functools.partial(_dispatch_kernel, P),
        grid=(T // tc,),
        in_specs=[pl.BlockSpec((tc * TOP_K,), lambda i: (i,), memory_space=pltpu.SMEM),
                  pl.BlockSpec(memory_space=pltpu.SMEM),
                  pl.BlockSpec(memory_space=pltpu.SMEM),
                  pl.BlockSpec(memory_space=pltpu.SMEM),
                  pl.BlockSpec(memory_space=pl.ANY)],
        out_specs=pl.BlockSpec(memory_space=pl.ANY),
        out_shape=jax.ShapeDtypeStruct((n_tiles * tm_e * P, LANES), F32),
        scratch_shapes=[pltpu.VMEM((tm_e * P, LANES), F32), pltpu.SemaphoreType.DMA,
                        pltpu.SemaphoreType.DMA],
        compiler_params=_cparams(("arbitrary",)),
        name="dispatch",
    )(dest_flat, pad_start, pad_len, n_used, h_tok)


def _experts_kernel(te_ref, nu_ref, x_ref, wg_ref, wu_ref, wd_ref, o_ref):
    S = wg_ref.shape[1] // LANES
    P = _pitch(S)
    tm = x_ref.shape[0] // P
    kc = 8 if S % 8 == 0 else S

    @pl.when(pl.program_id(0) >= nu_ref[0])
    def _():
        o_ref[...] = jnp.zeros_like(o_ref)

    @pl.when(pl.program_id(0) < nu_ref[0])
    def _():
        g = u = None
        for c0 in range(0, S, kc):
            xk = jnp.concatenate([x_ref[pl.ds(s, tm, stride=P), :] for s in range(c0, c0 + kc)],
                                 axis=1).astype(BF16)
            ks = slice(c0 * LANES, (c0 + kc) * LANES)
            gk = jnp.dot(xk, wg_ref[0, ks, :].astype(BF16), preferred_element_type=F32)
            uk = jnp.dot(xk, wu_ref[0, ks, :].astype(BF16), preferred_element_type=F32)
            g = gk if g is None else g + gk
            u = uk if u is None else u + uk
        act = (_silu(g) * u).astype(BF16)
        for c0 in range(0, S, kc):
            ns = slice(c0 * LANES, (c0 + kc) * LANES)
            y = jnp.dot(act, wd_ref[0, :, ns].astype(BF16), preferred_element_type=F32)
            for s in range(kc):
                o_ref[pl.ds(c0 + s, tm, stride=P), :] = y[:, s * LANES:(s + 1) * LANES]
        for s in range(S, P):
            o_ref[pl.ds(s, tm, stride=P), :] = jnp.zeros((tm, LANES), F32)


def _experts(tile_expert, n_used, x_tok, wg, wu, wd, tm):
    E, D, ff = wg.shape
    P = _pitch(D // LANES)
    n_tiles = x_tok.shape[0] // (tm * P)

    def row_map(i, te, nu):
        return (jnp.minimum(i, nu[0] - 1), 0)

    def w_map(i, te, nu):
        return (te[i], 0, 0)

    grid_spec = pltpu.PrefetchScalarGridSpec(
        num_scalar_prefetch=2,
        grid=(n_tiles,),
        in_specs=[pl.BlockSpec((tm * P, LANES), row_map),
                  pl.BlockSpec((1, D, ff), w_map),
                  pl.BlockSpec((1, D, ff), w_map),
                  pl.BlockSpec((1, ff, D), w_map)],
        out_specs=pl.BlockSpec((tm * P, LANES), lambda i, te, nu: (i, 0)),
    )
    return pl.pallas_call(
        _experts_kernel,
        grid_spec=grid_spec,
        out_shape=jax.ShapeDtypeStruct(x_tok.shape, F32),
        compiler_params=_cparams(("arbitrary",)),
        name="experts",
    )(tile_expert, n_used, x_tok, wg, wu, wd)


def _combine_kernel(dest_ref, wts_ref, ysh_ref, x_ref, gate2_ref, y_hbm, o_ref, buf_ref, sem):
    tc, D = x_ref.shape
    S = D // LANES
    P = _pitch(S)
    align = math.gcd(P, 8)
    slot_rows = tc * P

    def token(n, carry):
        for j in range(TOP_K):
            src = y_hbm.at[pl.ds(pl.multiple_of(dest_ref[n * TOP_K + j] * P, align), S)]
            dst = buf_ref.at[pl.ds(pl.multiple_of(j * slot_rows + n * P, align), S)]
            pltpu.make_async_copy(src, dst, sem).start()
        return carry

    lax.fori_loop(0, tc, token, 0)
    n_rows = TOP_K * tc * S
    pltpu.make_async_copy(y_hbm.at[pl.ds(0, n_rows)], buf_ref.at[pl.ds(0, n_rows)], sem).wait()

    w = wts_ref[...]
    w_cols = [jnp.broadcast_to(w[:, j:j + 1], (tc, LANES)) for j in range(TOP_K)]
    for s in range(S):
        acc = ysh_ref[pl.ds(s, tc, stride=P), :]
        for j in range(TOP_K):
            acc = acc + w_cols[j] * buf_ref[pl.ds(j * slot_rows + s, tc, stride=P), :]
        ls = slice(s * LANES, (s + 1) * LANES)
        o_ref[:, ls] = x_ref[:, ls] + gate2_ref[:, ls] * acc


def _combine(dest_flat, wts, y_shared, x, gate2, y_sorted, tc=64):
    T, D = x.shape
    tc = min(tc, T)
    P = _pitch(D // LANES)
    return pl.pallas_call(
        _combine_kernel,
        grid=(T // tc,),
        in_specs=[pl.BlockSpec((tc * TOP_K,), lambda i: (i,), memory_space=pltpu.SMEM),
                  pl.BlockSpec((tc, LANES), lambda i: (i, 0)),
                  pl.BlockSpec((tc * P, LANES), lambda i: (i, 0)),
                  pl.BlockSpec((tc, D), lambda i: (i, 0)),
                  pl.BlockSpec((1, D), lambda i: (0, 0)),
                  pl.BlockSpec(memory_space=pl.ANY)],
        out_specs=pl.BlockSpec((tc, D), lambda i: (i, 0)),
        out_shape=jax.ShapeDtypeStruct((T, D), F32),
        scratch_shapes=[pltpu.VMEM((TOP_K * tc * P, LANES), F32), pltpu.SemaphoreType.DMA],
        compiler_params=_cparams(("arbitrary",)),
        name="combine",
    )(dest_flat, wts, y_shared, x, gate2, y_sorted)


def _moe(x, gain, shift, gate2, rw_t, bias, wg, wu, wd, swg, swu, swd, tm_e=256, tc_d=256):
    T, D = x.shape
    tm_e = min(tm_e, T)
    tc_d = min(tc_d, T)
    h_tok, wts, idx, rank, cnt = _router(x, gain, shift, rw_t, bias)
    counts = cnt[:, 0].astype(jnp.int32)
    tiles = (counts + tm_e - 1) // tm_e
    tiles_cum = jnp.cumsum(tiles)
    row_start = (tiles_cum - tiles) * tm_e
    n_tiles = (T * TOP_K) // tm_e + N_EXPERTS
    dest = jnp.take(row_start, idx, axis=0) + rank
    dest_flat = dest.T.reshape(-1)
    tile_expert = jnp.minimum(jnp.searchsorted(tiles_cum, jnp.arange(n_tiles, dtype=jnp.int32), side="right"),
                              N_EXPERTS - 1).astype(jnp.int32)
    n_used = tiles_cum[-1:].astype(jnp.int32)
    x_sorted = _dispatch(dest_flat, row_start + counts, tiles * tm_e - counts, n_used, h_tok, n_tiles, tm_e, tc_d)
    y_sorted = _experts(tile_expert, n_used, x_sorted, wg, wu, wd, tm_e)
    n_sh = T // tm_e
    y_shared = _experts(jnp.zeros((n_sh,), jnp.int32), jnp.full((1,), n_sh, jnp.int32), h_tok,
                        swg[None], swu[None], swd[None], tm_e)
    return _combine(dest_flat, wts, y_shared, x, gate2, y_sorted)


def _final_norm_kernel(x_ref, g_ref, o_ref):
    x = x_ref[...]
    o_ref[...] = x * lax.rsqrt(jnp.mean(x * x, axis=-1, keepdims=True) + NORM_EPS) * g_ref[...]


def _final_norm(x, gain, tm=512):
    T, D = x.shape
    return pl.pallas_call(
        _final_norm_kernel,
        grid=(T // tm,),
        in_specs=[pl.BlockSpec((tm, D), lambda i: (i, 0)), pl.BlockSpec((1, D), lambda i: (0, 0))],
        out_specs=pl.BlockSpec((tm, D), lambda i: (i, 0)),
        out_shape=jax.ShapeDtypeStruct((T, D), F32),
        compiler_params=_cparams(("arbitrary",)),
        name="final_norm",
    )(x, gain)


def _pad_cols(a, n):
    return jnp.pad(a, ((0, 0), (0, n - a.shape[1])))


def _pad_rows(a, n):
    return jnp.pad(a, ((0, n - a.shape[0]), (0, 0)))


def _pad_rwkv_cols(a, W, w_lora, a_lora):
    c0 = 3 * W
    return jnp.concatenate([a[:, :c0], _pad_cols(a[:, c0:c0 + w_lora], LORA_PAD),
                            _pad_cols(a[:, c0 + w_lora:c0 + w_lora + a_lora], LORA_PAD),
                            a[:, c0 + w_lora + a_lora:]], axis=1)


def kernel(x, c, w_mod, b_mod, norm_mix, norm_ffn, w_in, rwkv_mu, rwkv_w0, rwkv_w_up, rwkv_a0, rwkv_a_up, rwkv_g_up, rwkv_k_k, rwkv_k_a, rwkv_r_k, rwkv_gn_w, rwkv_gn_b, rwkv_v0, rwkv_v_down, rwkv_v_up, hgrn_lower_bounds, hgrn_norm, w_out, router_w, router_bias, expert_w_gate, expert_w_up, expert_w_down, shared_w_gate, shared_w_up, shared_w_down, final_norm):
    B, T, D = x.shape
    L = w_mod.shape[0]
    W = rwkv_w0.shape[1]
    w_lora, a_lora = rwkv_w_up.shape[1], rwkv_a_up.shape[1]
    rwkv_cols = rwkv_mu.shape[1]
    rwkv_cols_p = 3 * W + 2 * LORA_PAD + G_LORA

    lb_soft = jax.nn.softmax(hgrn_lower_bounds.astype(F32), axis=0)
    lbs = jnp.cumsum(lb_soft, axis=0) - lb_soft[0]
    mod = _modulation(c, w_mod, b_mod)

    xs = x.reshape(B * T, D)
    v_first = None
    for l in range(L):
        sh1, sc1, g1, sh2, sc2, g2 = [mod[l, n * D:(n + 1) * D].reshape(1, D) for n in range(6)]
        w_in_p = jnp.concatenate([_pad_rwkv_cols(w_in[l][:, :rwkv_cols], W, w_lora, a_lora),
                                  w_in[l][:, rwkv_cols:]], axis=1).astype(BF16)
        p = _norm_proj(xs, norm_mix[l].reshape(1, D) * (1.0 + sc1), sh1, w_in_p)
        mu_p = _pad_rwkv_cols(rwkv_mu[l].reshape(1, -1), W, w_lora, a_lora)
        vres = None
        if l > 0:
            vres = (rwkv_v0[l - 1].reshape(1, W), _pad_cols(rwkv_v_down[l - 1], V_LORA_PAD).astype(BF16),
                    _pad_rows(rwkv_v_up[l - 1], V_LORA_PAD).astype(BF16), v_first)
        r, lw, k, v, kk, kka, g = _rwkv_prep(
            p, mu_p, rwkv_w0[l].reshape(1, W), _pad_rows(rwkv_w_up[l], LORA_PAD).astype(BF16),
            rwkv_a0[l].reshape(1, W), _pad_rows(rwkv_a_up[l], LORA_PAD).astype(BF16),
            rwkv_g_up[l].astype(BF16), rwkv_k_k[l].reshape(1, W), rwkv_k_a[l].reshape(1, W), vres)
        if l == 0:
            v_first = v
        y_r = _rwkv_recurrence(r, lw, k, v, kk, kka, g, rwkv_r_k[l].reshape(1, W),
                               rwkv_gn_w[l].reshape(1, W), rwkv_gn_b[l].reshape(1, W))
        y_h = _hgrn(p, rwkv_cols_p, lbs[l].reshape(1, -1), hgrn_norm[l].reshape(1, -1))
        xs = _out_proj(y_r, y_h, w_out[l].astype(BF16), xs, g1)
        xs = _moe(xs, norm_ffn[l].reshape(1, D) * (1.0 + sc2), sh2, g2, router_w[l].T,
                  router_bias[l].reshape(-1, 1), expert_w_gate[l], expert_w_up[l], expert_w_down[l],
                  shared_w_gate[l], shared_w_up[l], shared_w_down[l])
    return _final_norm(xs, final_norm.reshape(1, D)).reshape(B, T, D)
```

```python
import functools
import math

import jax
import jax.numpy as jnp
from jax import lax
from jax.experimental import pallas as pl
from jax.experimental.pallas import tpu as pltpu

F32 = jnp.float32
BF16 = jnp.bfloat16

LANES = 128
VMEM_LIMIT = 56 * 1024 * 1024

RWKV_HEAD = 64
HGRN_HEAD = 128
CHUNK = 64
SUB = 16
LORA_PAD = 128
G_LORA = 256
V_LORA_PAD = 128
N_EXPERTS = 64
N_GROUPS = 8
TOPK_GROUPS = 4
TOP_K = 8
ROUTE_SCALE = 2.5
MASKED_SCORE = -1e4
MIN_FORGET = 1e-30
NORM_EPS = 1e-6
GN_EPS = 64e-5
DECAY_SCALE = 0.6065306597126334

NN = (((1,), (0,)), ((), ()))
NT = (((1,), (1,)), ((), ()))
TN = (((0,), (0,)), ((), ()))


def _mm(a, b, dims=NN):
    return lax.dot_general(a.astype(BF16), b.astype(BF16), dims, preferred_element_type=F32)


def _mmx(a, b, dims=NN):
    return lax.dot_general(a.astype(F32), b.astype(F32), dims, preferred_element_type=F32,
                           precision=lax.Precision.HIGHEST)


def _split3(x):
    hi = x.astype(BF16)
    r1 = x - hi.astype(F32)
    mid = r1.astype(BF16)
    lo = (r1 - mid.astype(F32)).astype(BF16)
    return hi, mid, lo


def _mm_left01(a01, b, dims=NN):
    a = a01.astype(BF16)
    d = functools.partial(lax.dot_general, dimension_numbers=dims, preferred_element_type=F32)
    hi, mid, lo = _split3(b)
    return d(a, hi) + d(a, mid) + d(a, lo)


def _mm_right01(a, b01, dims=NN):
    b = b01.astype(BF16)
    d = functools.partial(lax.dot_general, dimension_numbers=dims, preferred_element_type=F32)
    hi, mid, lo = _split3(a)
    return d(hi, b) + d(mid, b) + d(lo, b)


def _sigmoid(x):
    return 1.0 / (1.0 + jnp.exp(-x))


def _silu(x):
    return x * _sigmoid(x)


def _cparams(sem):
    return pltpu.CompilerParams(dimension_semantics=sem, vmem_limit_bytes=VMEM_LIMIT)


def _modulation(c, w_mod, b_mod):
    L, D, N = w_mod.shape
    tn = 512
    cond8 = jnp.broadcast_to(c, (8, D))
    out = pl.pallas_call(
        _mod_silu_kernel,
        grid=(L, N // tn),
        in_specs=[pl.BlockSpec((8, D), lambda l, j: (0, 0)),
                  pl.BlockSpec((1, D, tn), lambda l, j: (l, 0, j)),
                  pl.BlockSpec((1, 1, tn), lambda l, j: (l, 0, j))],
        out_specs=pl.BlockSpec((1, 8, tn), lambda l, j: (l, 0, j)),
        out_shape=jax.ShapeDtypeStruct((L, 8, N), F32),
        compiler_params=_cparams(("arbitrary", "arbitrary")),
        name="modulation",
    )(cond8, w_mod, b_mod.reshape(L, 1, N))
    return out[:, 0, :]


def _mod_silu_kernel(c_ref, w_ref, b_ref, o_ref):
    cond = _silu(c_ref[...])
    o_ref[0] = _mm(cond, w_ref[0]) + b_ref[0]


def _normmod(x, gain, shift):
    y = x * lax.rsqrt(jnp.mean(x * x, axis=-1, keepdims=True) + NORM_EPS)
    return y * gain + shift


def _norm_proj_kernel(x_ref, gain_ref, shift_ref, w_ref, o_ref, h_ref):
    @pl.when(pl.program_id(1) == 0)
    def _():
        h_ref[...] = _normmod(x_ref[...], gain_ref[...], shift_ref[...]).astype(BF16)

    o_ref[...] = jnp.dot(h_ref[...], w_ref[...], preferred_element_type=F32)


def _norm_proj(x, gain, shift, w, tm=512, tn=512):
    T, D = x.shape
    N = w.shape[1]
    return pl.pallas_call(
        _norm_proj_kernel,
        grid=(T // tm, N // tn),
        in_specs=[pl.BlockSpec((tm, D), lambda i, j: (i, 0)),
                  pl.BlockSpec((1, D), lambda i, j: (0, 0)),
                  pl.BlockSpec((1, D), lambda i, j: (0, 0)),
                  pl.BlockSpec((D, tn), lambda i, j: (0, j))],
        out_specs=pl.BlockSpec((tm, tn), lambda i, j: (i, j)),
        out_shape=jax.ShapeDtypeStruct((T, N), F32),
        scratch_shapes=[pltpu.VMEM((tm, D), BF16)],
        compiler_params=_cparams(("arbitrary", "arbitrary")),
        name="norm_proj",
    )(x, gain, shift, w)


def _pair_ones():
    r = lax.broadcasted_iota(jnp.int32, (LANES, LANES), 0) // RWKV_HEAD
    c = lax.broadcasted_iota(jnp.int32, (LANES, LANES), 1) // RWKV_HEAD
    return jnp.where(r == c, 1.0, 0.0).astype(F32)


def _rwkv_prep_kernel(has_vres, *refs):
    n_in = 20 if has_vres else 16
    (r_ref, k_ref, v_ref, lo_ref, rp_ref, kp_ref, vp_ref, lop_ref, mu_ref, w0_ref, wup_ref,
     a0_ref, aup_ref, gup_ref, kk_ref, ka_ref) = refs[:16]
    if has_vres:
        v0_ref, vdn_ref, vup_ref, vf_ref = refs[16:20]
    ro_ref, lwo_ref, ko_ref, vo_ref, kko_ref, kkao_ref, go_ref = refs[n_in:]
    first = pl.program_id(0) == 0
    W = r_ref.shape[1]

    def shift(cur_ref, prev_ref, mu):
        cur = cur_ref[...]
        prev_last = jnp.where(first, 0.0, prev_ref[7:8, :])
        rolled = pltpu.roll(cur, 1, axis=0)
        row = lax.broadcasted_iota(jnp.int32, cur.shape, 0)
        prev = jnp.where(row == 0, prev_last, rolled)
        return cur + (prev - cur) * mu

    r = shift(r_ref, rp_ref, mu_ref[:, 0:W])
    k = shift(k_ref, kp_ref, mu_ref[:, W:2 * W])
    v = shift(v_ref, vp_ref, mu_ref[:, 2 * W:3 * W])
    lo = shift(lo_ref, lop_ref, mu_ref[:, 3 * W:])
    w_lo = lo[:, 0:LORA_PAD]
    a_lo = lo[:, LORA_PAD:2 * LORA_PAD]
    g_lo = lo[:, 2 * LORA_PAD:]

    z = w0_ref[...] + _mm(jnp.tanh(w_lo), wup_ref[...])
    lw = -DECAY_SCALE * _sigmoid(z)
    a = _sigmoid(a0_ref[...] + _mm(a_lo, aup_ref[...]))
    g = _mm(_sigmoid(g_lo), gup_ref[...])
    if has_vres:
        mix = _sigmoid(v0_ref[...] + _mm(_mm(v, vdn_ref[...]), vup_ref[...]))
        v = v + (vf_ref[...] - v) * mix

    kkr = k * kk_ref[...]
    sq = kkr * kkr
    ones = _pair_ones()
    ss = jnp.concatenate(
        [_mm_right01(sq[:, j * LANES:(j + 1) * LANES], ones) for j in range(W // LANES)], axis=1)
    kk = kkr / jnp.maximum(jnp.sqrt(ss), 1e-12)
    k2 = k * (1.0 + (a - 1.0) * ka_ref[...])

    ro_ref[...] = r
    lwo_ref[...] = lw
    ko_ref[...] = k2
    vo_ref[...] = v
    kko_ref[...] = kk
    kkao_ref[...] = kk * a
    go_ref[...] = g


def _rwkv_prep(p, mu_p, w0, wup, a0, aup, gup, k_k, k_a, vres, tp=128):
    T = p.shape[0]
    W = w0.shape[1]
    lo_w = 2 * LORA_PAD + G_LORA
    nb = tp // 8
    lo_blk = 3 * W // lo_w

    def cur(cb, width):
        return pl.BlockSpec((tp, width), lambda i, cb=cb: (i, cb))

    def prev(cb, width):
        return pl.BlockSpec((8, width), lambda i, cb=cb: (jnp.maximum(i * nb - 1, 0), cb))

    def vec(n):
        return pl.BlockSpec((1, n), lambda i: (0, 0))

    def mat(a, b):
        return pl.BlockSpec((a, b), lambda i: (0, 0))

    in_specs = [cur(0, W), cur(1, W), cur(2, W), cur(lo_blk, lo_w),
                prev(0, W), prev(1, W), prev(2, W), prev(lo_blk, lo_w),
                vec(3 * W + lo_w), vec(W), mat(LORA_PAD, W), vec(W), mat(LORA_PAD, W),
                mat(G_LORA, W), vec(W), vec(W)]
    args = [p, p, p, p, p, p, p, p, mu_p, w0, wup, a0, aup, gup, k_k, k_a]
    if vres is not None:
        v0, vdn, vup, v_first = vres
        in_specs += [vec(W), mat(W, V_LORA_PAD), mat(V_LORA_PAD, W),
                     pl.BlockSpec((tp, W), lambda i: (i, 0))]
        args += [v0, vdn, vup, v_first]
    out_spec = pl.BlockSpec((tp, W), lambda i: (i, 0))
    sds = jax.ShapeDtypeStruct((T, W), F32)
    return pl.pallas_call(
        functools.partial(_rwkv_prep_kernel, vres is not None),
        grid=(T // tp,),
        in_specs=in_specs,
        out_specs=[out_spec] * 7,
        out_shape=[sds] * 7,
        compiler_params=_cparams(("arbitrary",)),
        name="rwkv_prep",
    )(*args)


def _tri_incl(n):
    row = lax.broadcasted_iota(jnp.int32, (n, n), 0)
    col = lax.broadcasted_iota(jnp.int32, (n, n), 1)
    return jnp.where(col <= row, 1.0, 0.0).astype(BF16)


def _rwkv_pair(r, c, lw, k, v, kk, kka, g, rk, gnw, gnb, st):
    C = r.shape[0]
    C2 = 2 * C
    cp = c - lw
    c_last = c[C - 1:C, :]
    e_c, e_cp, e_nc, e_dl = jnp.exp(c), jnp.exp(cp), jnp.exp(-c), jnp.exp(c_last - c)

    lane = lax.broadcasted_iota(jnp.int32, (C, LANES), 1)
    m0 = lane < RWKV_HEAD

    def stack(x):
        return jnp.concatenate([jnp.where(m0, x, 0.0), jnp.where(m0, 0.0, x)], axis=0)

    aq_rt = jnp.concatenate([stack(kk * e_cp), stack(r * e_c)], axis=0).astype(BF16)
    bt_kt = jnp.concatenate([stack(kka * e_nc), stack(k * e_nc)], axis=0).astype(BF16)
    kw_bw = jnp.concatenate([stack(k * e_dl), stack(kka * e_dl)], axis=0).astype(BF16)
    vs = stack(v)
    vs_b = vs.astype(BF16)

    row = lax.broadcasted_iota(jnp.int32, (C2, C2), 0)
    col = lax.broadcasted_iota(jnp.int32, (C2, C2), 1)
    t_i, s_j = row % C, col % C
    strict, incl, eye = s_j < t_i, s_j <= t_i, row == col

    yield
    p = _mm(aq_rt, bt_kt, NT)
    yield
    l_ab = jnp.where(strict, p[:C2, :C2], 0.0)
    l_ak = jnp.where(strict, p[:C2, C2:], 0.0)
    l_rb = jnp.where(incl, p[C2:, :C2], 0.0)
    l_rk = jnp.where(incl, p[C2:, C2:], 0.0)

    pw = -l_ab
    inv = jnp.where(eye, 1.0, 0.0) + pw
    n = 2
    while n < C:
        pw = _mm(pw, pw)
        yield
        inv = inv + _mm(inv, pw)
        n *= 2

    xy = _mm(aq_rt, st) + _mm(jnp.concatenate([l_ak, l_rk], axis=0), vs_b)
    yield
    u = _mm(inv, xy[:C2])
    yield
    y = xy[C2:] - _mm(l_rb, u)
    w_col = jnp.broadcast_to(jnp.exp(c_last), (C2, C2)).T
    st_new = st * w_col + _mm(kw_bw, jnp.concatenate([vs, -u], axis=0), TN)
    yield

    mh = (row // C) == (col // RWKV_HEAD)
    inv_n = 1.0 / RWKV_HEAD
    mean = jnp.sum(y, axis=1, keepdims=True) * inv_n
    yc = jnp.where(mh, y - mean, 0.0)
    var = jnp.sum(yc * yc, axis=1, keepdims=True) * inv_n
    yn = yc * lax.rsqrt(var + GN_EPS)
    bonus = jnp.sum(stack(r * k * rk), axis=1, keepdims=True) * vs
    yn_t = yn[:C] + yn[C:]
    bonus_t = bonus[:C] + bonus[C:]
    return (yn_t * gnw + gnb + bonus_t) * g, st_new


def _interleave(gens):
    results = [None] * len(gens)
    live = list(enumerate(gens))
    while live:
        still = []
        for j, gen in live:
            try:
                next(gen)
                still.append((j, gen))
            except StopIteration as stop:
                results[j] = stop.value
        live = still
    return results


def _rwkv_rec_kernel(r_ref, lw_ref, k_ref, v_ref, kk_ref, kka_ref, g_ref, rk_ref, gnw_ref, gnb_ref,
                     o_ref, st_ref):
    @pl.when(pl.program_id(1) == 0)
    def _():
        st_ref[...] = jnp.zeros_like(st_ref)

    rows, lanes = r_ref.shape
    tri = _tri_incl(CHUNK)

    def chunk(ci, carry):
        rs = pl.ds(pl.multiple_of(ci * CHUNK, CHUNK), CHUNK)
        lw_all = lw_ref[rs, :]
        c_all = _mm_left01(tri, lw_all)
        r, k, v, kk, kka, g = (ref[rs, :] for ref in (r_ref, k_ref, v_ref, kk_ref, kka_ref, g_ref))
        rk, gnw, gnb = rk_ref[...], gnw_ref[...], gnb_ref[...]
        gens = []
        for j in range(lanes // LANES):
            sl = slice(j * LANES, (j + 1) * LANES)
            gens.append(_rwkv_pair(r[:, sl], c_all[:, sl], lw_all[:, sl], k[:, sl], v[:, sl], kk[:, sl],
                                   kka[:, sl], g[:, sl], rk[:, sl], gnw[:, sl], gnb[:, sl], st_ref[j]))
        results = _interleave(gens)
        for j, (_, st_new) in enumerate(results):
            st_ref[j] = st_new
        o_ref[rs, :] = jnp.concatenate([out for out, _ in results], axis=1).astype(o_ref.dtype)
        return carry

    lax.fori_loop(0, rows // CHUNK, chunk, 0)


def _rwkv_recurrence(r, lw, k, v, kk, kka, g, rk, gnw, gnb, hb_lanes=512, rows=512):
    T, W = r.shape
    rows = min(rows, T)
    seq = pl.BlockSpec((rows, hb_lanes), lambda h, t: (t, h))
    vec = pl.BlockSpec((1, hb_lanes), lambda h, t: (0, h))
    return pl.pallas_call(
        _rwkv_rec_kernel,
        grid=(W // hb_lanes, T // rows),
        in_specs=[seq] * 7 + [vec] * 3,
        out_specs=seq,
        out_shape=jax.ShapeDtypeStruct((T, W), BF16),
        scratch_shapes=[pltpu.VMEM((hb_lanes // LANES, LANES, LANES), F32)],
        compiler_params=_cparams(("arbitrary", "arbitrary")),
        name="rwkv_recurrence",
    )(r, lw, k, v, kk, kka, g, rk, gnw, gnb)


def _hgrn_head(q_raw, f_raw, i_c, g_raw, lb, gain, tri, s):
    C = q_raw.shape[0]
    q = _silu(q_raw)
    forget = lb + (1.0 - lb) * _sigmoid(f_raw)
    lf = jnp.log(jnp.maximum(forget, MIN_FORGET))
    kin = (1.0 - lb) * _sigmoid(-f_raw)
    b = _mm_left01(tri, lf)

    o = _mm(q * jnp.exp(b), s)

    lane_s = lax.broadcasted_iota(jnp.int32, (SUB, C), 1)
    row_s = lax.broadcasted_iota(jnp.int32, (SUB, C), 0)
    score_rows = []
    for i in range(C // SUB):
        lo, hi = i * SUB, (i + 1) * SUB
        b_i, q_i = b[lo:hi], q[lo:hi]
        blk = jnp.zeros((SUB, C), F32)
        for sidx in range(SUB):
            srow = lo + sidx
            d = jnp.exp(jnp.minimum(b_i - b[srow:srow + 1], 0.0)) * q_i * kin[srow:srow + 1]
            blk = jnp.where(lane_s == srow, jnp.sum(d, axis=1, keepdims=True), blk)
        blk = jnp.where(lane_s <= row_s + lo, blk, 0.0)
        if i > 0:
            b_st = b[lo - 1:lo]
            qs = q_i * jnp.exp(b_i - b_st)
            ks = kin * jnp.exp(jnp.minimum(b_st - b, 0.0))
            blk = blk + jnp.where(lane_s < lo, _mm(qs, ks, NT), 0.0)
        score_rows.append(blk)
    scores = jnp.concatenate(score_rows, axis=0)
    o = o + _mm(scores, i_c)

    b_last = b[C - 1:C]
    f_col = jnp.broadcast_to(jnp.exp(b_last), (LANES, LANES)).T
    s_new = s * f_col + _mm(kin * jnp.exp(b_last - b), i_c, TN)

    o = o * lax.rsqrt(jnp.mean(o * o, axis=1, keepdims=True) + NORM_EPS)
    return o * gain * _sigmoid(g_raw), s_new


def _hgrn_kernel(q_ref, f_ref, i_ref, g_ref, lb_ref, gain_ref, o_ref, s_ref):
    @pl.when(pl.program_id(1) == 0)
    def _():
        s_ref[...] = jnp.zeros_like(s_ref)

    rows, lanes = q_ref.shape
    tri = _tri_incl(CHUNK)

    def chunk(ci, carry):
        rs = pl.ds(pl.multiple_of(ci * CHUNK, CHUNK), CHUNK)
        q, f, i_c, g = (ref[rs, :] for ref in (q_ref, f_ref, i_ref, g_ref))
        lb, gain = lb_ref[...], gain_ref[...]
        results = []
        for j in range(lanes // LANES):
            sl = slice(j * LANES, (j + 1) * LANES)
            results.append(_hgrn_head(q[:, sl], f[:, sl], i_c[:, sl], g[:, sl], lb[:, sl], gain[:, sl],
                                      tri, s_ref[j]))
        for j, (_, s_new) in enumerate(results):
            s_ref[j] = s_new
        o_ref[rs, :] = jnp.concatenate([out for out, _ in results], axis=1).astype(o_ref.dtype)
        return carry

    lax.fori_loop(0, rows // CHUNK, chunk, 0)


def _hgrn(p, col0, lb, gain, hb_lanes=512, rows=512):
    T = p.shape[0]
    W = lb.shape[1]
    rows = min(rows, T)
    nb = W // hb_lanes
    b0 = col0 // hb_lanes

    def sec(n):
        return pl.BlockSpec((rows, hb_lanes), lambda h, t, n=n: (t, b0 + n * nb + h))

    vec = pl.BlockSpec((1, hb_lanes), lambda h, t: (0, h))
    return pl.pallas_call(
        _hgrn_kernel,
        grid=(nb, T // rows),
        in_specs=[sec(0), sec(1), sec(2), sec(3), vec, vec],
        out_specs=pl.BlockSpec((rows, hb_lanes), lambda h, t: (t, h)),
        out_shape=jax.ShapeDtypeStruct((T, W), BF16),
        scratch_shapes=[pltpu.VMEM((hb_lanes // LANES, LANES, LANES), F32)],
        compiler_params=_cparams(("arbitrary", "arbitrary")),
        name="hgrn2",
    )(p, p, p, p, lb, gain)


def _out_proj_kernel(ya_ref, yb_ref, wa_ref, wb_ref, x_ref, gate_ref, o_ref):
    acc = jnp.dot(ya_ref[...], wa_ref[...], preferred_element_type=F32)
    acc = acc + jnp.dot(yb_ref[...], wb_ref[...], preferred_element_type=F32)
    o_ref[...] = x_ref[...] + gate_ref[...] * acc


def _out_proj(ya, yb, w, x, gate, tm=512, tn=1024):
    T, Ka = ya.shape
    Kb = yb.shape[1]
    D = w.shape[1]
    return pl.pallas_call(
        _out_proj_kernel,
        grid=(T // tm, D // tn),
        in_specs=[pl.BlockSpec((tm, Ka), lambda i, j: (i, 0)),
                  pl.BlockSpec((tm, Kb), lambda i, j: (i, 0)),
                  pl.BlockSpec((Ka, tn), lambda i, j: (0, j)),
                  pl.BlockSpec((Kb, tn), lambda i, j: (Ka // Kb, j)),
                  pl.BlockSpec((tm, tn), lambda i, j: (i, j)),
                  pl.BlockSpec((1, tn), lambda i, j: (0, j))],
        out_specs=pl.BlockSpec((tm, tn), lambda i, j: (i, j)),
        out_shape=jax.ShapeDtypeStruct((T, D), F32),
        compiler_params=_cparams(("arbitrary", "arbitrary")),
        name="out_proj",
    )(ya, yb, w, w, x, gate)


def _first_max_onehot(work, axis, size):
    m = jnp.max(work, axis=axis, keepdims=True)
    idx = lax.broadcasted_iota(jnp.int32, work.shape, axis)
    first = jnp.min(jnp.where(work == m, idx, size), axis=axis, keepdims=True)
    return idx == first, m


def _pitch(s_per):
    return s_per + 8 if s_per % 16 == 0 else s_per


def _store_token_major(ref, val):
    n, d = val.shape
    pitch = ref.shape[0] // n
    for s in range(d // LANES):
        ref[pl.ds(s, n, stride=pitch), :] = val[:, s * LANES:(s + 1) * LANES]
    for s in range(d // LANES, pitch):
        ref[pl.ds(s, n, stride=pitch), :] = jnp.zeros((n, LANES), F32)


def _router_kernel(x_ref, gain_ref, shift_ref, rwt_ref, bias_ref,
                   h_ref, wts_ref, idx_ref, rank_ref, cnt_ref, carry_ref):
    @pl.when(pl.program_id(0) == 0)
    def _():
        carry_ref[...] = jnp.zeros_like(carry_ref)

    h = _normmod(x_ref[...], gain_ref[...], shift_ref[...])
    _store_token_major(h_ref, h)
    tm = h.shape[0]
    per_group = N_EXPERTS // N_GROUPS
    logits = _mmx(rwt_ref[...], h, NT)
    scores = _sigmoid(logits)
    biased = scores + bias_ref[...]

    b3 = biased.reshape(N_GROUPS, per_group, tm)
    pick1, m1 = _first_max_onehot(b3, 1, per_group)
    m2 = jnp.max(jnp.where(pick1, -jnp.inf, b3), axis=1, keepdims=True)
    gscore = (m1 + m2).reshape(N_GROUPS, tm)

    work = gscore
    gsel = jnp.zeros_like(gscore)
    for _ in range(TOPK_GROUPS):
        pick, _m = _first_max_onehot(work, 0, N_GROUPS)
        gsel = jnp.where(pick, 1.0, gsel)
        work = jnp.where(pick, -jnp.inf, work)
    ok = jnp.broadcast_to(gsel.reshape(N_GROUPS, 1, tm), (N_GROUPS, per_group, tm)).reshape(N_EXPERTS, tm)
    work = jnp.where(ok > 0.5, biased, MASKED_SCORE)
    picks = []
    for _ in range(TOP_K):
        pick, _m = _first_max_onehot(work, 0, N_EXPERTS)
        picks.append(pick)
        work = jnp.where(pick, -jnp.inf, work)
    picked = functools.reduce(lambda a, b: a + b, [jnp.where(p, 1.0, 0.0) for p in picks])
    sel = picked * scores
    gates_t = sel / jnp.sum(sel, axis=0, keepdims=True) * ROUTE_SCALE

    ra = lax.broadcasted_iota(jnp.int32, (tm, tm), 0)
    rb = lax.broadcasted_iota(jnp.int32, (tm, tm), 1)
    before = jnp.where(ra < rb, 1.0, 0.0).astype(BF16)
    carry = carry_ref[:, 0:1]
    rank_full = _mm(picked, before) + carry
    e_iota = lax.broadcasted_iota(jnp.int32, (N_EXPERTS, tm), 0).astype(F32)

    def slot_rows(table):
        rows = [jnp.sum(jnp.where(p, table, 0.0), axis=0, keepdims=True) for p in picks]
        return jnp.concatenate(rows, axis=0)

    idx_ref[...] = slot_rows(e_iota).astype(jnp.int32)
    rank_ref[...] = slot_rows(rank_full).astype(jnp.int32)
    w_rows = slot_rows(gates_t)
    wts_ref[...] = jnp.concatenate([w_rows, jnp.zeros((LANES - TOP_K, tm), F32)], axis=0).T
    new_carry = carry + jnp.sum(picked, axis=1, keepdims=True)
    carry_ref[...] = jnp.broadcast_to(new_carry, carry_ref.shape)
    cnt_ref[...] = jnp.broadcast_to(new_carry, cnt_ref.shape)


def _router(x, gain, shift, rw_t, bias, tm=256):
    T, D = x.shape
    tm = min(tm, T)
    P = _pitch(D // LANES)
    return pl.pallas_call(
        _router_kernel,
        grid=(T // tm,),
        in_specs=[pl.BlockSpec((tm, D), lambda i: (i, 0)),
                  pl.BlockSpec((1, D), lambda i: (0, 0)),
                  pl.BlockSpec((1, D), lambda i: (0, 0)),
                  pl.BlockSpec((N_EXPERTS, D), lambda i: (0, 0)),
                  pl.BlockSpec((N_EXPERTS, 1), lambda i: (0, 0))],
        out_specs=[pl.BlockSpec((tm * P, LANES), lambda i: (i, 0)),
                   pl.BlockSpec((tm, LANES), lambda i: (i, 0)),
                   pl.BlockSpec((TOP_K, tm), lambda i: (0, i)),
                   pl.BlockSpec((TOP_K, tm), lambda i: (0, i)),
                   pl.BlockSpec((N_EXPERTS, LANES), lambda i: (0, 0))],
        out_shape=[jax.ShapeDtypeStruct((T * P, LANES), F32),
                   jax.ShapeDtypeStruct((T, LANES), F32),
                   jax.ShapeDtypeStruct((TOP_K, T), jnp.int32),
                   jax.ShapeDtypeStruct((TOP_K, T), jnp.int32),
                   jax.ShapeDtypeStruct((N_EXPERTS, LANES), F32)],
        scratch_shapes=[pltpu.VMEM((N_EXPERTS, LANES), F32)],
        compiler_params=_cparams(("arbitrary",)),
        name="router",
    )(x, gain, shift, rw_t, bias)


def _dispatch_kernel(P, idx_ref, rank_ref, row_start_ref, pad_start_ref, pad_len_ref, nu_ref, h_ref, xs_hbm,
                     zero_ref, sem, pad_sem):
    i = pl.program_id(0)
    tc = idx_ref.shape[1]
    tile_rows = zero_ref.shape[0]

    def slab(ref, row):
        return ref.at[pl.ds(pl.multiple_of(row * P, math.gcd(P, 8)), P)]

    def token(n, carry):
        src = slab(h_ref, n)
        for j in range(TOP_K):
            dest = row_start_ref[idx_ref[j, n]] + rank_ref[j, n]
            pltpu.make_async_copy(src, slab(xs_hbm, dest), sem).start()
        return carry

    lax.fori_loop(0, tc, token, 0)

    @pl.when(i == 0)
    def _():
        zero_ref[...] = jnp.zeros_like(zero_ref)
        zero_slab = zero_ref.at[pl.ds(0, P)]

        def expert(e, carry):
            def fill(r, c):
                pltpu.make_async_copy(zero_slab, slab(xs_hbm, pad_start_ref[e] + r), pad_sem).start()
                return c

            def drain(r, c):
                pltpu.make_async_copy(zero_slab, slab(xs_hbm, 0), pad_sem).wait()
                return c

            lax.fori_loop(0, pad_len_ref[e], fill, 0)
            lax.fori_loop(0, pad_len_ref[e], drain, 0)
            return carry

        lax.fori_loop(0, N_EXPERTS, expert, 0)

        def tile_dst(t):
            return xs_hbm.at[pl.ds(pl.multiple_of(t * tile_rows, 8), tile_rows)]

        def fill_tile(t, c):
            pltpu.make_async_copy(zero_ref, tile_dst(t), pad_sem).start()
            return c

        def drain_tile(t, c):
            pltpu.make_async_copy(zero_ref, tile_dst(0), pad_sem).wait()
            return c

        n_tiles = xs_hbm.shape[0] // tile_rows
        lax.fori_loop(nu_ref[0], n_tiles, fill_tile, 0)
        lax.fori_loop(nu_ref[0], n_tiles, drain_tile, 0)

    for j in range(TOP_K):
        pltpu.make_async_copy(h_ref, xs_hbm.at[pl.ds(0, tc * P)], sem).wait()


def _dispatch(idx, rank, row_start, pad_start, pad_len, n_used, h_tok, n_tiles, tm_e, tc):
    T = idx.shape[1]
    P = h_tok.shape[0] // T
    smem = pl.BlockSpec(memory_space=pltpu.SMEM)
    slots = pl.BlockSpec((TOP_K, tc), lambda i: (0, i), memory_space=pltpu.SMEM)
    return pl.pallas_call(
        functools.partial(_dispatch_kernel, P),
        grid=(T // tc,),
        in_specs=[slots, slots, smem, smem, smem, smem,
                  pl.BlockSpec((tc * P, LANES), lambda i: (i, 0))],
        out_specs=pl.BlockSpec(memory_space=pl.ANY),
        out_shape=jax.ShapeDtypeStruct((n_tiles * tm_e * P, LANES), F32),
        scratch_shapes=[pltpu.VMEM((tm_e * P, LANES), F32), pltpu.SemaphoreType.DMA,
                        pltpu.SemaphoreType.DMA],
        compiler_params=_cparams(("arbitrary",)),
        name="dispatch",
    )(idx, rank, row_start, pad_start, pad_len, n_used, h_tok)


def _experts_kernel(te_ref, nu_ref, x_ref, wg_ref, wu_ref, wd_ref, o_ref):
    S = wg_ref.shape[1] // LANES
    P = _pitch(S)
    tm = x_ref.shape[0] // P
    kc = 8 if S % 8 == 0 else S

    @pl.when(pl.program_id(0) >= nu_ref[0])
    def _():
        o_ref[...] = jnp.zeros_like(o_ref)

    @pl.when(pl.program_id(0) < nu_ref[0])
    def _():
        g = u = None
        for c0 in range(0, S, kc):
            xk = jnp.concatenate([x_ref[pl.ds(s, tm, stride=P), :] for s in range(c0, c0 + kc)],
                                 axis=1).astype(BF16)
            ks = slice(c0 * LANES, (c0 + kc) * LANES)
            gk = jnp.dot(xk, wg_ref[0, ks, :].astype(BF16), preferred_element_type=F32)
            uk = jnp.dot(xk, wu_ref[0, ks, :].astype(BF16), preferred_element_type=F32)
            g = gk if g is None else g + gk
            u = uk if u is None else u + uk
        act = (_silu(g) * u).astype(BF16)
        for c0 in range(0, S, kc):
            ns = slice(c0 * LANES, (c0 + kc) * LANES)
            y = jnp.dot(act, wd_ref[0, :, ns].astype(BF16), preferred_element_type=F32)
            for s in range(kc):
                o_ref[pl.ds(c0 + s, tm, stride=P), :] = y[:, s * LANES:(s + 1) * LANES]
        for s in range(S, P):
            o_ref[pl.ds(s, tm, stride=P), :] = jnp.zeros((tm, LANES), F32)


def _experts(tile_expert, n_used, x_tok, wg, wu, wd, tm):
    E, D, ff = wg.shape
    P = _pitch(D // LANES)
    n_tiles = x_tok.shape[0] // (tm * P)

    def row_map(i, te, nu):
        return (jnp.minimum(i, nu[0] - 1), 0)

    def w_map(i, te, nu):
        return (te[i], 0, 0)

    grid_spec = pltpu.PrefetchScalarGridSpec(
        num_scalar_prefetch=2,
        grid=(n_tiles,),
        in_specs=[pl.BlockSpec((tm * P, LANES), row_map),
                  pl.BlockSpec((1, D, ff), w_map),
                  pl.BlockSpec((1, D, ff), w_map),
                  pl.BlockSpec((1, ff, D), w_map)],
        out_specs=pl.BlockSpec((tm * P, LANES), lambda i, te, nu: (i, 0)),
    )
    return pl.pallas_call(
        _experts_kernel,
        grid_spec=grid_spec,
        out_shape=jax.ShapeDtypeStruct(x_tok.shape, F32),
        compiler_params=_cparams(("arbitrary",)),
        name="experts",
    )(tile_expert, n_used, x_tok, wg, wu, wd)


def _combine_kernel(idx_ref, rank_ref, row_start_ref, wts_ref, ysh_ref, x_ref, gate2_ref, y_hbm, o_ref,
                    buf_ref, sem):
    tc, D = x_ref.shape
    S = D // LANES
    P = _pitch(S)
    align = math.gcd(P, 8)
    slot_rows = tc * P

    def token(n, carry):
        for j in range(TOP_K):
            dest = row_start_ref[idx_ref[j, n]] + rank_ref[j, n]
            src = y_hbm.at[pl.ds(pl.multiple_of(dest * P, align), S)]
            dst = buf_ref.at[pl.ds(pl.multiple_of(j * slot_rows + n * P, align), S)]
            pltpu.make_async_copy(src, dst, sem).start()
        return carry

    lax.fori_loop(0, tc, token, 0)
    n_rows = TOP_K * tc * S
    pltpu.make_async_copy(y_hbm.at[pl.ds(0, n_rows)], buf_ref.at[pl.ds(0, n_rows)], sem).wait()

    w = wts_ref[...]
    w_cols = [jnp.broadcast_to(w[:, j:j + 1], (tc, LANES)) for j in range(TOP_K)]
    for s in range(S):
        acc = ysh_ref[pl.ds(s, tc, stride=P), :]
        for j in range(TOP_K):
            acc = acc + w_cols[j] * buf_ref[pl.ds(j * slot_rows + s, tc, stride=P), :]
        ls = slice(s * LANES, (s + 1) * LANES)
        o_ref[:, ls] = x_ref[:, ls] + gate2_ref[:, ls] * acc


def _combine(idx, rank, row_start, wts, y_shared, x, gate2, y_sorted, tc=128):
    T, D = x.shape
    tc = min(tc, T)
    P = _pitch(D // LANES)
    return pl.pallas_call(
        _combine_kernel,
        grid=(T // tc,),
        in_specs=[pl.BlockSpec((TOP_K, tc), lambda i: (0, i), memory_space=pltpu.SMEM),
                  pl.BlockSpec((TOP_K, tc), lambda i: (0, i), memory_space=pltpu.SMEM),
                  pl.BlockSpec(memory_space=pltpu.SMEM),
                  pl.BlockSpec((tc, LANES), lambda i: (i, 0)),
                  pl.BlockSpec((tc * P, LANES), lambda i: (i, 0)),
                  pl.BlockSpec((tc, D), lambda i: (i, 0)),
                  pl.BlockSpec((1, D), lambda i: (0, 0)),
                  pl.BlockSpec(memory_space=pl.ANY)],
        out_specs=pl.BlockSpec((tc, D), lambda i: (i, 0)),
        out_shape=jax.ShapeDtypeStruct((T, D), F32),
        scratch_shapes=[pltpu.VMEM((TOP_K * tc * P, LANES), F32), pltpu.SemaphoreType.DMA],
        compiler_params=_cparams(("arbitrary",)),
        name="combine",
    )(idx, rank, row_start, wts, y_shared, x, gate2, y_sorted)


def _moe(x, gain, shift, gate2, rw_t, bias, wg, wu, wd, swg, swu, swd, tm_e=256, tc_d=256):
    T, D = x.shape
    tm_e = min(tm_e, T)
    tc_d = min(tc_d, T)
    h_tok, wts, idx, rank, cnt = _router(x, gain, shift, rw_t, bias)
    counts = cnt[:, 0].astype(jnp.int32)
    tiles = (counts + tm_e - 1) // tm_e
    tiles_cum = jnp.cumsum(tiles)
    row_start = (tiles_cum - tiles) * tm_e
    n_tiles = (T * TOP_K) // tm_e + N_EXPERTS
    tile_ids = jnp.arange(n_tiles, dtype=jnp.int32)
    tile_expert = jnp.minimum(jnp.sum((tiles_cum[None, :] <= tile_ids[:, None]).astype(jnp.int32), axis=1),
                              N_EXPERTS - 1)
    n_used = tiles_cum[-1:].astype(jnp.int32)
    x_sorted = _dispatch(idx, rank, row_start, row_start + counts, tiles * tm_e - counts, n_used, h_tok,
                         n_tiles, tm_e, tc_d)
    y_sorted = _experts(tile_expert, n_used, x_sorted, wg, wu, wd, tm_e)
    n_sh = T // tm_e
    y_shared = _experts(jnp.zeros((n_sh,), jnp.int32), jnp.full((1,), n_sh, jnp.int32), h_tok,
                        swg[None], swu[None], swd[None], tm_e)
    return _combine(idx, rank, row_start, wts, y_shared, x, gate2, y_sorted)


def _final_norm_kernel(x_ref, g_ref, o_ref):
    x = x_ref[...]
    o_ref[...] = x * lax.rsqrt(jnp.mean(x * x, axis=-1, keepdims=True) + NORM_EPS) * g_ref[...]


def _final_norm(x, gain, tm=512):
    T, D = x.shape
    return pl.pallas_call(
        _final_norm_kernel,
        grid=(T // tm,),
        in_specs=[pl.BlockSpec((tm, D), lambda i: (i, 0)), pl.BlockSpec((1, D), lambda i: (0, 0))],
        out_specs=pl.BlockSpec((tm, D), lambda i: (i, 0)),
        out_shape=jax.ShapeDtypeStruct((T, D), F32),
        compiler_params=_cparams(("arbitrary",)),
        name="final_norm",
    )(x, gain)


def _pad_cols(a, n):
    return jnp.pad(a, ((0, 0), (0, n - a.shape[1])))


def _pad_rows(a, n):
    return jnp.pad(a, ((0, n - a.shape[0]), (0, 0)))


def _pad_rwkv_cols(a, W, w_lora, a_lora):
    c0 = 3 * W
    return jnp.concatenate([a[:, :c0], _pad_cols(a[:, c0:c0 + w_lora], LORA_PAD),
                            _pad_cols(a[:, c0 + w_lora:c0 + w_lora + a_lora], LORA_PAD),
                            a[:, c0 + w_lora + a_lora:]], axis=1)


def kernel(x, c, w_mod, b_mod, norm_mix, norm_ffn, w_in, rwkv_mu, rwkv_w0, rwkv_w_up, rwkv_a0, rwkv_a_up, rwkv_g_up, rwkv_k_k, rwkv_k_a, rwkv_r_k, rwkv_gn_w, rwkv_gn_b, rwkv_v0, rwkv_v_down, rwkv_v_up, hgrn_lower_bounds, hgrn_norm, w_out, router_w, router_bias, expert_w_gate, expert_w_up, expert_w_down, shared_w_gate, shared_w_up, shared_w_down, final_norm):
    B, T, D = x.shape
    L = w_mod.shape[0]
    W = rwkv_w0.shape[1]
    w_lora, a_lora = rwkv_w_up.shape[1], rwkv_a_up.shape[1]
    rwkv_cols = rwkv_mu.shape[1]
    rwkv_cols_p = 3 * W + 2 * LORA_PAD + G_LORA

    lb_soft = jax.nn.softmax(hgrn_lower_bounds.astype(F32), axis=0)
    lbs = jnp.cumsum(lb_soft, axis=0) - lb_soft[0]
    mod = _modulation(c, w_mod, b_mod)

    xs = x.reshape(B * T, D)
    v_first = None
    for l in range(L):
        sh1, sc1, g1, sh2, sc2, g2 = [mod[l, n * D:(n + 1) * D].reshape(1, D) for n in range(6)]
        w_in_p = jnp.concatenate([_pad_rwkv_cols(w_in[l][:, :rwkv_cols], W, w_lora, a_lora),
                                  w_in[l][:, rwkv_cols:]], axis=1).astype(BF16)
        p = _norm_proj(xs, norm_mix[l].reshape(1, D) * (1.0 + sc1), sh1, w_in_p)
        mu_p = _pad_rwkv_cols(rwkv_mu[l].reshape(1, -1), W, w_lora, a_lora)
        vres = None
        if l > 0:
            vres = (rwkv_v0[l - 1].reshape(1, W), _pad_cols(rwkv_v_down[l - 1], V_LORA_PAD).astype(BF16),
                    _pad_rows(rwkv_v_up[l - 1], V_LORA_PAD).astype(BF16), v_first)
        r, lw, k, v, kk, kka, g = _rwkv_prep(
            p, mu_p, rwkv_w0[l].reshape(1, W), _pad_rows(rwkv_w_up[l], LORA_PAD).astype(BF16),
            rwkv_a0[l].reshape(1, W), _pad_rows(rwkv_a_up[l], LORA_PAD).astype(BF16),
            rwkv_g_up[l].astype(BF16), rwkv_k_k[l].reshape(1, W), rwkv_k_a[l].reshape(1, W), vres)
        if l == 0:
            v_first = v
        y_r = _rwkv_recurrence(r, lw, k, v, kk, kka, g, rwkv_r_k[l].reshape(1, W),
                               rwkv_gn_w[l].reshape(1, W), rwkv_gn_b[l].reshape(1, W))
        y_h = _hgrn(p, rwkv_cols_p, lbs[l].reshape(1, -1), hgrn_norm[l].reshape(1, -1))
        xs = _out_proj(y_r, y_h, w_out[l].astype(BF16), xs, g1)
        xs = _moe(xs, norm_ffn[l].reshape(1, D) * (1.0 + sc2), sh2, g2, router_w[l].T,
                  router_bias[l].reshape(-1, 1), expert_w_gate[l], expert_w_up[l], expert_w_down[l],
                  shared_w_gate[l], shared_w_up[l], shared_w_down[l])
    return _final_norm(xs, final_norm.reshape(1, D)).reshape(B, T, D)
```

```python
import functools
import math

import jax
import jax.numpy as jnp
from jax import lax
from jax.experimental import pallas as pl
from jax.experimental.pallas import tpu as pltpu

F32 = jnp.float32
BF16 = jnp.bfloat16

LANES = 128
VMEM_LIMIT = 56 * 1024 * 1024

RWKV_HEAD = 64
HGRN_HEAD = 128
CHUNK = 64
SUB = 16
LORA_PAD = 128
G_LORA = 256
V_LORA_PAD = 128
N_EXPERTS = 64
N_GROUPS = 8
TOPK_GROUPS = 4
TOP_K = 8
ROUTE_SCALE = 2.5
MASKED_SCORE = -1e4
MIN_FORGET = 1e-30
NORM_EPS = 1e-6
GN_EPS = 64e-5
DECAY_SCALE = 0.6065306597126334

NN = (((1,), (0,)), ((), ()))
NT = (((1,), (1,)), ((), ()))
TN = (((0,), (0,)), ((), ()))


def _mm(a, b, dims=NN):
    return lax.dot_general(a.astype(BF16), b.astype(BF16), dims, preferred_element_type=F32)


def _mmx(a, b, dims=NN):
    return lax.dot_general(a.astype(F32), b.astype(F32), dims, preferred_element_type=F32,
                           precision=lax.Precision.HIGHEST)


def _split3(x):
    hi = x.astype(BF16)
    r1 = x - hi.astype(F32)
    mid = r1.astype(BF16)
    lo = (r1 - mid.astype(F32)).astype(BF16)
    return hi, mid, lo


def _mm_left01(a01, b, dims=NN):
    a = a01.astype(BF16)
    d = functools.partial(lax.dot_general, dimension_numbers=dims, preferred_element_type=F32)
    hi, mid, lo = _split3(b)
    return d(a, hi) + d(a, mid) + d(a, lo)


def _mm_right01(a, b01, dims=NN):
    b = b01.astype(BF16)
    d = functools.partial(lax.dot_general, dimension_numbers=dims, preferred_element_type=F32)
    hi, mid, lo = _split3(a)
    return d(hi, b) + d(mid, b) + d(lo, b)


def _sigmoid(x):
    return 1.0 / (1.0 + jnp.exp(-x))


def _silu(x):
    return x * _sigmoid(x)


def _cparams(sem):
    return pltpu.CompilerParams(dimension_semantics=sem, vmem_limit_bytes=VMEM_LIMIT)


def _mod_kernel(c_ref, w_ref, b_ref, o_ref):
    cond = _silu(c_ref[...])
    tn = w_ref.shape[2]
    cols = [jnp.sum(w_ref[0, :, j * LANES:(j + 1) * LANES] * cond, axis=0, keepdims=True)
            for j in range(tn // LANES)]
    o_ref[0] = jnp.concatenate(cols, axis=1) + b_ref[0]


def _modulation(c, w_mod, b_mod):
    L, D, N = w_mod.shape
    tn = 512
    c_lanes = jnp.broadcast_to(c.reshape(D, 1), (D, LANES))
    out = pl.pallas_call(
        _mod_kernel,
        grid=(L, N // tn),
        in_specs=[pl.BlockSpec((D, LANES), lambda l, j: (0, 0)),
                  pl.BlockSpec((1, D, tn), lambda l, j: (l, 0, j)),
                  pl.BlockSpec((1, 1, tn), lambda l, j: (l, 0, j))],
        out_specs=pl.BlockSpec((1, 1, tn), lambda l, j: (l, 0, j)),
        out_shape=jax.ShapeDtypeStruct((L, 1, N), F32),
        compiler_params=_cparams(("arbitrary", "arbitrary")),
        name="modulation",
    )(c_lanes, w_mod, b_mod.reshape(L, 1, N))
    return out[:, 0, :]


def _normmod(x, gain, shift):
    y = x * lax.rsqrt(jnp.mean(x * x, axis=-1, keepdims=True) + NORM_EPS)
    return y * gain + shift


def _norm_proj_kernel(x_ref, gain_ref, shift_ref, w_ref, o_ref, h_ref):
    @pl.when(pl.program_id(1) == 0)
    def _():
        h_ref[...] = _normmod(x_ref[...], gain_ref[...], shift_ref[...]).astype(BF16)

    o_ref[...] = jnp.dot(h_ref[...], w_ref[...], preferred_element_type=F32)


def _norm_proj(x, gain, shift, w, tm=512, tn=512):
    T, D = x.shape
    N = w.shape[1]
    return pl.pallas_call(
        _norm_proj_kernel,
        grid=(T // tm, N // tn),
        in_specs=[pl.BlockSpec((tm, D), lambda i, j: (i, 0)),
                  pl.BlockSpec((1, D), lambda i, j: (0, 0)),
                  pl.BlockSpec((1, D), lambda i, j: (0, 0)),
                  pl.BlockSpec((D, tn), lambda i, j: (0, j))],
        out_specs=pl.BlockSpec((tm, tn), lambda i, j: (i, j)),
        out_shape=jax.ShapeDtypeStruct((T, N), F32),
        scratch_shapes=[pltpu.VMEM((tm, D), BF16)],
        compiler_params=_cparams(("arbitrary", "arbitrary")),
        name="norm_proj",
    )(x, gain, shift, w)


def _pair_ones():
    r = lax.broadcasted_iota(jnp.int32, (LANES, LANES), 0) // RWKV_HEAD
    c = lax.broadcasted_iota(jnp.int32, (LANES, LANES), 1) // RWKV_HEAD
    return jnp.where(r == c, 1.0, 0.0).astype(F32)


def _rwkv_prep_kernel(has_vres, *refs):
    n_in = 20 if has_vres else 16
    (r_ref, k_ref, v_ref, lo_ref, rp_ref, kp_ref, vp_ref, lop_ref, mu_ref, w0_ref, wup_ref,
     a0_ref, aup_ref, gup_ref, kk_ref, ka_ref) = refs[:16]
    if has_vres:
        v0_ref, vdn_ref, vup_ref, vf_ref = refs[16:20]
    ro_ref, lwo_ref, ko_ref, vo_ref, kko_ref, kkao_ref, go_ref = refs[n_in:]
    first = pl.program_id(0) == 0
    W = r_ref.shape[1]

    def shift(cur_ref, prev_ref, mu):
        cur = cur_ref[...]
        prev_last = jnp.where(first, 0.0, prev_ref[7:8, :])
        rolled = pltpu.roll(cur, 1, axis=0)
        row = lax.broadcasted_iota(jnp.int32, cur.shape, 0)
        prev = jnp.where(row == 0, prev_last, rolled)
        return cur + (prev - cur) * mu

    r = shift(r_ref, rp_ref, mu_ref[:, 0:W])
    k = shift(k_ref, kp_ref, mu_ref[:, W:2 * W])
    v = shift(v_ref, vp_ref, mu_ref[:, 2 * W:3 * W])
    lo = shift(lo_ref, lop_ref, mu_ref[:, 3 * W:])
    w_lo = lo[:, 0:LORA_PAD]
    a_lo = lo[:, LORA_PAD:2 * LORA_PAD]
    g_lo = lo[:, 2 * LORA_PAD:]

    z = w0_ref[...] + _mm(jnp.tanh(w_lo), wup_ref[...])
    lw = -DECAY_SCALE * _sigmoid(z)
    a = _sigmoid(a0_ref[...] + _mm(a_lo, aup_ref[...]))
    g = _mm(_sigmoid(g_lo), gup_ref[...])
    if has_vres:
        mix = _sigmoid(v0_ref[...] + _mm(_mm(v, vdn_ref[...]), vup_ref[...]))
        v = v + (vf_ref[...] - v) * mix

    kkr = k * kk_ref[...]
    sq = kkr * kkr
    ones = _pair_ones()
    ss = jnp.concatenate(
        [_mm_right01(sq[:, j * LANES:(j + 1) * LANES], ones) for j in range(W // LANES)], axis=1)
    kk = kkr / jnp.maximum(jnp.sqrt(ss), 1e-12)
    k2 = k * (1.0 + (a - 1.0) * ka_ref[...])

    ro_ref[...] = r
    lwo_ref[...] = lw
    ko_ref[...] = k2
    vo_ref[...] = v
    kko_ref[...] = kk
    kkao_ref[...] = kk * a
    go_ref[...] = g


def _rwkv_prep(p, mu_p, w0, wup, a0, aup, gup, k_k, k_a, vres, tp=128):
    T = p.shape[0]
    W = w0.shape[1]
    lo_w = 2 * LORA_PAD + G_LORA
    nb = tp // 8
    lo_blk = 3 * W // lo_w

    def cur(cb, width):
        return pl.BlockSpec((tp, width), lambda i, cb=cb: (i, cb))

    def prev(cb, width):
        return pl.BlockSpec((8, width), lambda i, cb=cb: (jnp.maximum(i * nb - 1, 0), cb))

    def vec(n):
        return pl.BlockSpec((1, n), lambda i: (0, 0))

    def mat(a, b):
        return pl.BlockSpec((a, b), lambda i: (0, 0))

    in_specs = [cur(0, W), cur(1, W), cur(2, W), cur(lo_blk, lo_w),
                prev(0, W), prev(1, W), prev(2, W), prev(lo_blk, lo_w),
                vec(3 * W + lo_w), vec(W), mat(LORA_PAD, W), vec(W), mat(LORA_PAD, W),
                mat(G_LORA, W), vec(W), vec(W)]
    args = [p, p, p, p, p, p, p, p, mu_p, w0, wup, a0, aup, gup, k_k, k_a]
    if vres is not None:
        v0, vdn, vup, v_first = vres
        in_specs += [vec(W), mat(W, V_LORA_PAD), mat(V_LORA_PAD, W),
                     pl.BlockSpec((tp, W), lambda i: (i, 0))]
        args += [v0, vdn, vup, v_first]
    out_spec = pl.BlockSpec((tp, W), lambda i: (i, 0))
    sds = jax.ShapeDtypeStruct((T, W), F32)
    return pl.pallas_call(
        functools.partial(_rwkv_prep_kernel, vres is not None),
        grid=(T // tp,),
        in_specs=in_specs,
        out_specs=[out_spec] * 7,
        out_shape=[sds] * 7,
        compiler_params=_cparams(("arbitrary",)),
        name="rwkv_prep",
    )(*args)


def _tri_incl(n):
    row = lax.broadcasted_iota(jnp.int32, (n, n), 0)
    col = lax.broadcasted_iota(jnp.int32, (n, n), 1)
    return jnp.where(col <= row, 1.0, 0.0).astype(BF16)


def _rwkv_pair(r, c, lw, k, v, kk, kka, g, rk, gnw, gnb, st):
    C = r.shape[0]
    C2 = 2 * C
    cp = c - lw
    c_last = c[C - 1:C, :]
    e_c, e_cp, e_nc, e_dl = jnp.exp(c), jnp.exp(cp), jnp.exp(-c), jnp.exp(c_last - c)

    lane = lax.broadcasted_iota(jnp.int32, (C, LANES), 1)
    m0 = lane < RWKV_HEAD

    def stack(x):
        return jnp.concatenate([jnp.where(m0, x, 0.0), jnp.where(m0, 0.0, x)], axis=0)

    aq_rt = jnp.concatenate([stack(kk * e_cp), stack(r * e_c)], axis=0).astype(BF16)
    bt_kt = jnp.concatenate([stack(kka * e_nc), stack(k * e_nc)], axis=0).astype(BF16)
    kw_bw = jnp.concatenate([stack(k * e_dl), stack(kka * e_dl)], axis=0).astype(BF16)
    vs = stack(v)
    vs_b = vs.astype(BF16)

    row = lax.broadcasted_iota(jnp.int32, (C2, C2), 0)
    col = lax.broadcasted_iota(jnp.int32, (C2, C2), 1)
    t_i, s_j = row % C, col % C
    strict, incl, eye = s_j < t_i, s_j <= t_i, row == col

    yield
    p = _mm(aq_rt, bt_kt, NT)
    yield
    l_ab = jnp.where(strict, p[:C2, :C2], 0.0)
    l_ak = jnp.where(strict, p[:C2, C2:], 0.0)
    l_rb = jnp.where(incl, p[C2:, :C2], 0.0)
    l_rk = jnp.where(incl, p[C2:, C2:], 0.0)

    pw = -l_ab
    inv = jnp.where(eye, 1.0, 0.0) + pw
    n = 2
    while n < C:
        pw = _mm(pw, pw)
        yield
        inv = inv + _mm(inv, pw)
        n *= 2

    xy = _mm(aq_rt, st) + _mm(jnp.concatenate([l_ak, l_rk], axis=0), vs_b)
    yield
    u = _mm(inv, xy[:C2])
    yield
    y = xy[C2:] - _mm(l_rb, u)
    w_col = jnp.broadcast_to(jnp.exp(c_last), (C2, C2)).T
    st_new = st * w_col + _mm(kw_bw, jnp.concatenate([vs, -u], axis=0), TN)
    yield

    mh = (row // C) == (col // RWKV_HEAD)
    inv_n = 1.0 / RWKV_HEAD
    mean = jnp.sum(y, axis=1, keepdims=True) * inv_n
    yc = jnp.where(mh, y - mean, 0.0)
    var = jnp.sum(yc * yc, axis=1, keepdims=True) * inv_n
    yn = yc * lax.rsqrt(var + GN_EPS)
    bonus = jnp.sum(stack(r * k * rk), axis=1, keepdims=True) * vs
    yn_t = yn[:C] + yn[C:]
    bonus_t = bonus[:C] + bonus[C:]
    return (yn_t * gnw + gnb + bonus_t) * g, st_new


def _interleave(gens):
    results = [None] * len(gens)
    live = list(enumerate(gens))
    while live:
        still = []
        for j, gen in live:
            try:
                next(gen)
                still.append((j, gen))
            except StopIteration as stop:
                results[j] = stop.value
        live = still
    return results


def _rwkv_rec_kernel(r_ref, lw_ref, k_ref, v_ref, kk_ref, kka_ref, g_ref, rk_ref, gnw_ref, gnb_ref,
                     o_ref, st_ref):
    @pl.when(pl.program_id(1) == 0)
    def _():
        st_ref[...] = jnp.zeros_like(st_ref)

    rows, lanes = r_ref.shape
    tri = _tri_incl(CHUNK)

    def chunk(ci, carry):
        rs = pl.ds(pl.multiple_of(ci * CHUNK, CHUNK), CHUNK)
        lw_all = lw_ref[rs, :]
        c_all = _mm_left01(tri, lw_all)
        r, k, v, kk, kka, g = (ref[rs, :] for ref in (r_ref, k_ref, v_ref, kk_ref, kka_ref, g_ref))
        rk, gnw, gnb = rk_ref[...], gnw_ref[...], gnb_ref[...]
        gens = []
        for j in range(lanes // LANES):
            sl = slice(j * LANES, (j + 1) * LANES)
            gens.append(_rwkv_pair(r[:, sl], c_all[:, sl], lw_all[:, sl], k[:, sl], v[:, sl], kk[:, sl],
                                   kka[:, sl], g[:, sl], rk[:, sl], gnw[:, sl], gnb[:, sl], st_ref[j]))
        results = _interleave(gens)
        for j, (_, st_new) in enumerate(results):
            st_ref[j] = st_new
        o_ref[rs, :] = jnp.concatenate([out for out, _ in results], axis=1).astype(o_ref.dtype)
        return carry

    lax.fori_loop(0, rows // CHUNK, chunk, 0)


def _rwkv_recurrence(r, lw, k, v, kk, kka, g, rk, gnw, gnb, hb_lanes=1024, rows=512):
    T, W = r.shape
    rows = min(rows, T)
    seq = pl.BlockSpec((rows, hb_lanes), lambda h, t: (t, h))
    vec = pl.BlockSpec((1, hb_lanes), lambda h, t: (0, h))
    return pl.pallas_call(
        _rwkv_rec_kernel,
        grid=(W // hb_lanes, T // rows),
        in_specs=[seq] * 7 + [vec] * 3,
        out_specs=seq,
        out_shape=jax.ShapeDtypeStruct((T, W), BF16),
        scratch_shapes=[pltpu.VMEM((hb_lanes // LANES, LANES, LANES), F32)],
        compiler_params=_cparams(("arbitrary", "arbitrary")),
        name="rwkv_recurrence",
    )(r, lw, k, v, kk, kka, g, rk, gnw, gnb)


def _hgrn_head(q_raw, f_raw, i_c, g_raw, lb, gain, tri, s):
    C = q_raw.shape[0]
    q = _silu(q_raw)
    forget = lb + (1.0 - lb) * _sigmoid(f_raw)
    lf = jnp.log(jnp.maximum(forget, MIN_FORGET))
    kin = (1.0 - lb) * _sigmoid(-f_raw)
    b = _mm_left01(tri, lf)

    o = _mm(q * jnp.exp(b), s)

    lane_s = lax.broadcasted_iota(jnp.int32, (SUB, C), 1)
    row_s = lax.broadcasted_iota(jnp.int32, (SUB, C), 0)
    score_rows = []
    for i in range(C // SUB):
        lo, hi = i * SUB, (i + 1) * SUB
        b_i, q_i = b[lo:hi], q[lo:hi]
        blk = jnp.zeros((SUB, C), F32)
        for sidx in range(SUB):
            srow = lo + sidx
            d = jnp.exp(jnp.minimum(b_i - b[srow:srow + 1], 0.0)) * q_i * kin[srow:srow + 1]
            blk = jnp.where(lane_s == srow, jnp.sum(d, axis=1, keepdims=True), blk)
        blk = jnp.where(lane_s <= row_s + lo, blk, 0.0)
        if i > 0:
            b_st = b[lo - 1:lo]
            qs = q_i * jnp.exp(b_i - b_st)
            ks = kin * jnp.exp(jnp.minimum(b_st - b, 0.0))
            blk = blk + jnp.where(lane_s < lo, _mm(qs, ks, NT), 0.0)
        score_rows.append(blk)
    scores = jnp.concatenate(score_rows, axis=0)
    o = o + _mm(scores, i_c)

    b_last = b[C - 1:C]
    f_col = jnp.broadcast_to(jnp.exp(b_last), (LANES, LANES)).T
    s_new = s * f_col + _mm(kin * jnp.exp(b_last - b), i_c, TN)

    o = o * lax.rsqrt(jnp.mean(o * o, axis=1, keepdims=True) + NORM_EPS)
    return o * gain * _sigmoid(g_raw), s_new


def _hgrn_kernel(q_ref, f_ref, i_ref, g_ref, lb_ref, gain_ref, o_ref, s_ref):
    @pl.when(pl.program_id(1) == 0)
    def _():
        s_ref[...] = jnp.zeros_like(s_ref)

    rows, lanes = q_ref.shape
    tri = _tri_incl(CHUNK)

    def chunk(ci, carry):
        rs = pl.ds(pl.multiple_of(ci * CHUNK, CHUNK), CHUNK)
        q, f, i_c, g = (ref[rs, :] for ref in (q_ref, f_ref, i_ref, g_ref))
        lb, gain = lb_ref[...], gain_ref[...]
        results = []
        for j in range(lanes // LANES):
            sl = slice(j * LANES, (j + 1) * LANES)
            results.append(_hgrn_head(q[:, sl], f[:, sl], i_c[:, sl], g[:, sl], lb[:, sl], gain[:, sl],
                                      tri, s_ref[j]))
        for j, (_, s_new) in enumerate(results):
            s_ref[j] = s_new
        o_ref[rs, :] = jnp.concatenate([out for out, _ in results], axis=1).astype(o_ref.dtype)
        return carry

    lax.fori_loop(0, rows // CHUNK, chunk, 0)


def _hgrn(p, col0, lb, gain, hb_lanes=512, rows=512):
    T = p.shape[0]
    W = lb.shape[1]
    rows = min(rows, T)
    nb = W // hb_lanes
    b0 = col0 // hb_lanes

    def sec(n):
        return pl.BlockSpec((rows, hb_lanes), lambda h, t, n=n: (t, b0 + n * nb + h))

    vec = pl.BlockSpec((1, hb_lanes), lambda h, t: (0, h))
    return pl.pallas_call(
        _hgrn_kernel,
        grid=(nb, T // rows),
        in_specs=[sec(0), sec(1), sec(2), sec(3), vec, vec],
        out_specs=pl.BlockSpec((rows, hb_lanes), lambda h, t: (t, h)),
        out_shape=jax.ShapeDtypeStruct((T, W), BF16),
        scratch_shapes=[pltpu.VMEM((hb_lanes // LANES, LANES, LANES), F32)],
        compiler_params=_cparams(("arbitrary", "arbitrary")),
        name="hgrn2",
    )(p, p, p, p, lb, gain)


def _out_proj_kernel(ya_ref, yb_ref, wa_ref, wb_ref, x_ref, gate_ref, o_ref):
    acc = jnp.dot(ya_ref[...], wa_ref[...], preferred_element_type=F32)
    acc = acc + jnp.dot(yb_ref[...], wb_ref[...], preferred_element_type=F32)
    o_ref[...] = x_ref[...] + gate_ref[...] * acc


def _out_proj(ya, yb, w, x, gate, tm=512, tn=1024):
    T, Ka = ya.shape
    Kb = yb.shape[1]
    D = w.shape[1]
    return pl.pallas_call(
        _out_proj_kernel,
        grid=(T // tm, D // tn),
        in_specs=[pl.BlockSpec((tm, Ka), lambda i, j: (i, 0)),
                  pl.BlockSpec((tm, Kb), lambda i, j: (i, 0)),
                  pl.BlockSpec((Ka, tn), lambda i, j: (0, j)),
                  pl.BlockSpec((Kb, tn), lambda i, j: (Ka // Kb, j)),
                  pl.BlockSpec((tm, tn), lambda i, j: (i, j)),
                  pl.BlockSpec((1, tn), lambda i, j: (0, j))],
        out_specs=pl.BlockSpec((tm, tn), lambda i, j: (i, j)),
        out_shape=jax.ShapeDtypeStruct((T, D), F32),
        compiler_params=_cparams(("arbitrary", "arbitrary")),
        name="out_proj",
    )(ya, yb, w, w, x, gate)


def _first_max_onehot(work, axis, size):
    m = jnp.max(work, axis=axis, keepdims=True)
    idx = lax.broadcasted_iota(jnp.int32, work.shape, axis)
    first = jnp.min(jnp.where(work == m, idx, size), axis=axis, keepdims=True)
    return idx == first, m


def _pitch(s_per):
    return s_per + 8 if s_per % 16 == 0 else s_per


def _store_token_major(ref, val):
    n, d = val.shape
    pitch = ref.shape[0] // n
    for s in range(d // LANES):
        ref[pl.ds(s, n, stride=pitch), :] = val[:, s * LANES:(s + 1) * LANES]
    for s in range(d // LANES, pitch):
        ref[pl.ds(s, n, stride=pitch), :] = jnp.zeros((n, LANES), F32)


def _router_kernel(x_ref, gain_ref, shift_ref, rwt_ref, bias_ref,
                   h_ref, wts_ref, idx_ref, rank_ref, cnt_ref, carry_ref):
    @pl.when(pl.program_id(0) == 0)
    def _():
        carry_ref[...] = jnp.zeros_like(carry_ref)

    h = _normmod(x_ref[...], gain_ref[...], shift_ref[...])
    _store_token_major(h_ref, h)
    tm = h.shape[0]
    per_group = N_EXPERTS // N_GROUPS
    logits = _mmx(rwt_ref[...], h, NT)
    scores = _sigmoid(logits)
    biased = scores + bias_ref[...]

    b3 = biased.reshape(N_GROUPS, per_group, tm)
    pick1, m1 = _first_max_onehot(b3, 1, per_group)
    m2 = jnp.max(jnp.where(pick1, -jnp.inf, b3), axis=1, keepdims=True)
    gscore = (m1 + m2).reshape(N_GROUPS, tm)

    work = gscore
    gsel = jnp.zeros_like(gscore)
    for _ in range(TOPK_GROUPS):
        pick, _m = _first_max_onehot(work, 0, N_GROUPS)
        gsel = jnp.where(pick, 1.0, gsel)
        work = jnp.where(pick, -jnp.inf, work)
    ok = jnp.broadcast_to(gsel.reshape(N_GROUPS, 1, tm), (N_GROUPS, per_group, tm)).reshape(N_EXPERTS, tm)
    work = jnp.where(ok > 0.5, biased, MASKED_SCORE)
    picks = []
    for _ in range(TOP_K):
        pick, _m = _first_max_onehot(work, 0, N_EXPERTS)
        picks.append(pick)
        work = jnp.where(pick, -jnp.inf, work)
    picked = functools.reduce(lambda a, b: a + b, [jnp.where(p, 1.0, 0.0) for p in picks])
    sel = picked * scores
    gates_t = sel / jnp.sum(sel, axis=0, keepdims=True) * ROUTE_SCALE

    ra = lax.broadcasted_iota(jnp.int32, (tm, tm), 0)
    rb = lax.broadcasted_iota(jnp.int32, (tm, tm), 1)
    before = jnp.where(ra < rb, 1.0, 0.0).astype(BF16)
    carry = carry_ref[:, 0:1]
    rank_full = _mm(picked, before) + carry
    e_iota = lax.broadcasted_iota(jnp.int32, (N_EXPERTS, tm), 0).astype(F32)

    def slot_rows(table):
        rows = [jnp.sum(jnp.where(p, table, 0.0), axis=0, keepdims=True) for p in picks]
        return jnp.concatenate(rows, axis=0)

    idx_ref[...] = slot_rows(e_iota).astype(jnp.int32)
    rank_ref[...] = slot_rows(rank_full).astype(jnp.int32)
    wts_ref[...] = slot_rows(gates_t)
    new_carry = carry + jnp.sum(picked, axis=1, keepdims=True)
    carry_ref[...] = jnp.broadcast_to(new_carry, carry_ref.shape)
    cnt_ref[...] = jnp.broadcast_to(new_carry, cnt_ref.shape)


def _router(x, gain, shift, rw_t, bias, tm=256):
    T, D = x.shape
    tm = min(tm, T)
    P = _pitch(D // LANES)
    return pl.pallas_call(
        _router_kernel,
        grid=(T // tm,),
        in_specs=[pl.BlockSpec((tm, D), lambda i: (i, 0)),
                  pl.BlockSpec((1, D), lambda i: (0, 0)),
                  pl.BlockSpec((1, D), lambda i: (0, 0)),
                  pl.BlockSpec((N_EXPERTS, D), lambda i: (0, 0)),
                  pl.BlockSpec((N_EXPERTS, 1), lambda i: (0, 0))],
        out_specs=[pl.BlockSpec((tm * P, LANES), lambda i: (i, 0)),
                   pl.BlockSpec((TOP_K, tm), lambda i: (0, i)),
                   pl.BlockSpec((TOP_K, tm), lambda i: (0, i)),
                   pl.BlockSpec((TOP_K, tm), lambda i: (0, i)),
                   pl.BlockSpec((N_EXPERTS, LANES), lambda i: (0, 0))],
        out_shape=[jax.ShapeDtypeStruct((T * P, LANES), F32),
                   jax.ShapeDtypeStruct((TOP_K, T), F32),
                   jax.ShapeDtypeStruct((TOP_K, T), jnp.int32),
                   jax.ShapeDtypeStruct((TOP_K, T), jnp.int32),
                   jax.ShapeDtypeStruct((N_EXPERTS, LANES), F32)],
        scratch_shapes=[pltpu.VMEM((N_EXPERTS, LANES), F32)],
        compiler_params=_cparams(("arbitrary",)),
        name="router",
    )(x, gain, shift, rw_t, bias)


def _dispatch_kernel(P, idx_ref, rank_ref, row_start_ref, pad_start_ref, pad_len_ref, nu_ref, h_ref, xs_hbm,
                     zero_ref, sem, pad_sem):
    i = pl.program_id(0)
    tc = idx_ref.shape[1]
    tile_rows = zero_ref.shape[0]

    def slab(ref, row):
        return ref.at[pl.ds(pl.multiple_of(row * P, math.gcd(P, 8)), P)]

    def token(n, carry):
        src = slab(h_ref, n)
        for j in range(TOP_K):
            dest = row_start_ref[idx_ref[j, n]] + rank_ref[j, n]
            pltpu.make_async_copy(src, slab(xs_hbm, dest), sem).start()
        return carry

    lax.fori_loop(0, tc, token, 0)

    @pl.when(i == 0)
    def _():
        zero_ref[...] = jnp.zeros_like(zero_ref)
        zero_slab = zero_ref.at[pl.ds(0, P)]

        def expert(e, carry):
            def fill(r, c):
                pltpu.make_async_copy(zero_slab, slab(xs_hbm, pad_start_ref[e] + r), pad_sem).start()
                return c

            def drain(r, c):
                pltpu.make_async_copy(zero_slab, slab(xs_hbm, 0), pad_sem).wait()
                return c

            lax.fori_loop(0, pad_len_ref[e], fill, 0)
            lax.fori_loop(0, pad_len_ref[e], drain, 0)
            return carry

        lax.fori_loop(0, N_EXPERTS, expert, 0)

        def tile_dst(t):
            return xs_hbm.at[pl.ds(pl.multiple_of(t * tile_rows, 8), tile_rows)]

        def fill_tile(t, c):
            pltpu.make_async_copy(zero_ref, tile_dst(t), pad_sem).start()
            return c

        def drain_tile(t, c):
            pltpu.make_async_copy(zero_ref, tile_dst(0), pad_sem).wait()
            return c

        n_tiles = xs_hbm.shape[0] // tile_rows
        lax.fori_loop(nu_ref[0], n_tiles, fill_tile, 0)
        lax.fori_loop(nu_ref[0], n_tiles, drain_tile, 0)

    for j in range(TOP_K):
        pltpu.make_async_copy(h_ref, xs_hbm.at[pl.ds(0, tc * P)], sem).wait()


def _dispatch(idx, rank, row_start, pad_start, pad_len, n_used, h_tok, n_tiles, tm_e, tc):
    T = idx.shape[1]
    P = h_tok.shape[0] // T
    smem = pl.BlockSpec(memory_space=pltpu.SMEM)
    slots = pl.BlockSpec((TOP_K, tc), lambda i: (0, i), memory_space=pltpu.SMEM)
    return pl.pallas_call(
        functools.partial(_dispatch_kernel, P),
        grid=(T // tc,),
        in_specs=[slots, slots, smem, smem, smem, smem,
                  pl.BlockSpec((tc * P, LANES), lambda i: (i, 0))],
        out_specs=pl.BlockSpec(memory_space=pl.ANY),
        out_shape=jax.ShapeDtypeStruct((n_tiles * tm_e * P, LANES), F32),
        scratch_shapes=[pltpu.VMEM((tm_e * P, LANES), F32), pltpu.SemaphoreType.DMA,
                        pltpu.SemaphoreType.DMA],
        compiler_params=_cparams(("arbitrary",)),
        name="dispatch",
    )(idx, rank, row_start, pad_start, pad_len, n_used, h_tok)


def _experts_kernel(te_ref, nu_ref, x_ref, wg_ref, wu_ref, wd_ref, o_ref):
    S = wg_ref.shape[2] // LANES
    P = _pitch(S)
    tm = x_ref.shape[0] // P
    kc = 8 if S % 8 == 0 else S

    @pl.when(pl.program_id(0) >= nu_ref[0])
    def _():
        o_ref[...] = jnp.zeros_like(o_ref)

    @pl.when(pl.program_id(0) < nu_ref[0])
    def _():
        g = u = None
        for c0 in range(0, S, kc):
            xk = jnp.concatenate([x_ref[pl.ds(s, tm, stride=P), :] for s in range(c0, c0 + kc)],
                                 axis=1).astype(BF16)
            ks = slice(c0 * LANES, (c0 + kc) * LANES)
            gk = jnp.dot(xk, wg_ref[0, 0, ks, :].astype(BF16), preferred_element_type=F32)
            uk = jnp.dot(xk, wu_ref[0, 0, ks, :].astype(BF16), preferred_element_type=F32)
            g = gk if g is None else g + gk
            u = uk if u is None else u + uk
        act = (_silu(g) * u).astype(BF16)
        for c0 in range(0, S, kc):
            ns = slice(c0 * LANES, (c0 + kc) * LANES)
            y = jnp.dot(act, wd_ref[0, 0, :, ns].astype(BF16), preferred_element_type=F32)
            for s in range(kc):
                o_ref[pl.ds(c0 + s, tm, stride=P), :] = y[:, s * LANES:(s + 1) * LANES]
        for s in range(S, P):
            o_ref[pl.ds(s, tm, stride=P), :] = jnp.zeros((tm, LANES), F32)


def _experts(tile_expert, n_used, x_tok, wg, wu, wd, layer, tm):
    _, E, D, ff = wg.shape
    P = _pitch(D // LANES)
    n_tiles = x_tok.shape[0] // (tm * P)

    def row_map(i, te, nu):
        return (jnp.minimum(i, nu[0] - 1), 0)

    def w_map(i, te, nu):
        return (layer, te[i], 0, 0)

    grid_spec = pltpu.PrefetchScalarGridSpec(
        num_scalar_prefetch=2,
        grid=(n_tiles,),
        in_specs=[pl.BlockSpec((tm * P, LANES), row_map),
                  pl.BlockSpec((1, 1, D, ff), w_map),
                  pl.BlockSpec((1, 1, D, ff), w_map),
                  pl.BlockSpec((1, 1, ff, D), w_map)],
        out_specs=pl.BlockSpec((tm * P, LANES), lambda i, te, nu: (i, 0)),
    )
    return pl.pallas_call(
        _experts_kernel,
        grid_spec=grid_spec,
        out_shape=jax.ShapeDtypeStruct(x_tok.shape, F32),
        compiler_params=_cparams(("arbitrary",)),
        name="experts",
    )(tile_expert, n_used, x_tok, wg, wu, wd)


def _combine_kernel(has_norm, idx_ref, rank_ref, row_start_ref, w_ref, ysh_ref, x_ref, gate2_ref, *rest):
    if has_norm:
        norm_gain_ref, y_hbm, o_ref, buf_ref, acc_ref, sem = rest
    else:
        y_hbm, o_ref, buf_ref, acc_ref, sem = rest
    tc, D = x_ref.shape
    S = D // LANES
    P = _pitch(S)
    align = math.gcd(P, 8)
    slot_rows = tc * P

    def token(n, carry):
        for j in range(TOP_K):
            dest = row_start_ref[idx_ref[j, n]] + rank_ref[j, n]
            src = y_hbm.at[pl.ds(pl.multiple_of(dest * P, align), S)]
            dst = buf_ref.at[pl.ds(pl.multiple_of(j * slot_rows + n * P, align), S)]
            pltpu.make_async_copy(src, dst, sem).start()
        return carry

    lax.fori_loop(0, tc, token, 0)
    n_rows = TOP_K * tc * S
    pltpu.make_async_copy(y_hbm.at[pl.ds(0, n_rows)], buf_ref.at[pl.ds(0, n_rows)], sem).wait()

    def accumulate(n, carry):
        row = pl.multiple_of(n * P, align)
        acc = ysh_ref[pl.ds(row, S), :]
        for j in range(TOP_K):
            slab = buf_ref[pl.ds(pl.multiple_of(j * slot_rows + n * P, align), S), :]
            acc = acc + w_ref[j, n] * slab
        acc_ref[pl.ds(row, S), :] = acc
        return carry

    lax.fori_loop(0, tc, accumulate, 0, unroll=2)
    for s in range(S):
        ls = slice(s * LANES, (s + 1) * LANES)
        o_ref[:, ls] = x_ref[:, ls] + gate2_ref[:, ls] * acc_ref[pl.ds(s, tc, stride=P), :]
    if has_norm:
        xo = o_ref[...]
        o_ref[...] = xo * lax.rsqrt(jnp.mean(xo * xo, axis=-1, keepdims=True) + NORM_EPS) * norm_gain_ref[...]


def _combine(idx, rank, row_start, wts, y_shared, x, gate2, y_sorted, norm_gain=None, tc=128):
    T, D = x.shape
    tc = min(tc, T)
    P = _pitch(D // LANES)
    slots = pl.BlockSpec((TOP_K, tc), lambda i: (0, i), memory_space=pltpu.SMEM)
    vec = pl.BlockSpec((1, D), lambda i: (0, 0))
    in_specs = [slots, slots, pl.BlockSpec(memory_space=pltpu.SMEM), slots,
                pl.BlockSpec((tc * P, LANES), lambda i: (i, 0)),
                pl.BlockSpec((tc, D), lambda i: (i, 0)), vec]
    args = [idx, rank, row_start, wts, y_shared, x, gate2]
    if norm_gain is not None:
        in_specs.append(vec)
        args.append(norm_gain)
    in_specs.append(pl.BlockSpec(memory_space=pl.ANY))
    args.append(y_sorted)
    return pl.pallas_call(
        functools.partial(_combine_kernel, norm_gain is not None),
        grid=(T // tc,),
        in_specs=in_specs,
        out_specs=pl.BlockSpec((tc, D), lambda i: (i, 0)),
        out_shape=jax.ShapeDtypeStruct((T, D), F32),
        scratch_shapes=[pltpu.VMEM((TOP_K * tc * P, LANES), F32), pltpu.VMEM((tc * P, LANES), F32),
                        pltpu.SemaphoreType.DMA],
        compiler_params=_cparams(("arbitrary",)),
        name="combine",
    )(*args)


def _moe(x, gain, shift, gate2, rw_t, bias, wg, wu, wd, swg, swu, swd, layer, norm_gain=None, tm_e=256,
         tc_d=256):
    T, D = x.shape
    tm_e = min(tm_e, T)
    tc_d = min(tc_d, T)
    h_tok, wts, idx, rank, cnt = _router(x, gain, shift, rw_t, bias)
    counts = cnt[:, 0].astype(jnp.int32)
    tiles = (counts + tm_e - 1) // tm_e
    tiles_cum = jnp.cumsum(tiles)
    row_start = (tiles_cum - tiles) * tm_e
    n_tiles = (T * TOP_K) // tm_e + N_EXPERTS
    tile_ids = jnp.arange(n_tiles, dtype=jnp.int32)
    tile_expert = jnp.minimum(jnp.sum((tiles_cum[None, :] <= tile_ids[:, None]).astype(jnp.int32), axis=1),
                              N_EXPERTS - 1)
    n_used = tiles_cum[-1:].astype(jnp.int32)
    x_sorted = _dispatch(idx, rank, row_start, row_start + counts, tiles * tm_e - counts, n_used, h_tok,
                         n_tiles, tm_e, tc_d)
    y_sorted = _experts(tile_expert, n_used, x_sorted, wg, wu, wd, layer, tm_e)
    n_sh = T // tm_e
    y_shared = _experts(jnp.zeros((n_sh,), jnp.int32), jnp.full((1,), n_sh, jnp.int32), h_tok,
                        swg[:, None], swu[:, None], swd[:, None], layer, tm_e)
    return _combine(idx, rank, row_start, wts, y_shared, x, gate2, y_sorted, norm_gain)


def _pad_cols(a, n):
    return jnp.pad(a, ((0, 0), (0, n - a.shape[1])))


def _pad_rows(a, n):
    return jnp.pad(a, ((0, n - a.shape[0]), (0, 0)))


def _pad_rwkv_cols(a, W, w_lora, a_lora):
    c0 = 3 * W
    return jnp.concatenate([a[:, :c0], _pad_cols(a[:, c0:c0 + w_lora], LORA_PAD),
                            _pad_cols(a[:, c0 + w_lora:c0 + w_lora + a_lora], LORA_PAD),
                            a[:, c0 + w_lora + a_lora:]], axis=1)


def kernel(x, c, w_mod, b_mod, norm_mix, norm_ffn, w_in, rwkv_mu, rwkv_w0, rwkv_w_up, rwkv_a0, rwkv_a_up, rwkv_g_up, rwkv_k_k, rwkv_k_a, rwkv_r_k, rwkv_gn_w, rwkv_gn_b, rwkv_v0, rwkv_v_down, rwkv_v_up, hgrn_lower_bounds, hgrn_norm, w_out, router_w, router_bias, expert_w_gate, expert_w_up, expert_w_down, shared_w_gate, shared_w_up, shared_w_down, final_norm):
    B, T, D = x.shape
    L = w_mod.shape[0]
    W = rwkv_w0.shape[1]
    w_lora, a_lora = rwkv_w_up.shape[1], rwkv_a_up.shape[1]
    rwkv_cols = rwkv_mu.shape[1]
    rwkv_cols_p = 3 * W + 2 * LORA_PAD + G_LORA

    lb_soft = jax.nn.softmax(hgrn_lower_bounds.astype(F32), axis=0)
    lbs = jnp.cumsum(lb_soft, axis=0) - lb_soft[0]
    mod = _modulation(c, w_mod, b_mod)

    xs = x.reshape(B * T, D)
    v_first = None
    for l in range(L):
        sh1, sc1, g1, sh2, sc2, g2 = [mod[l, n * D:(n + 1) * D].reshape(1, D) for n in range(6)]
        w_in_p = jnp.concatenate([_pad_rwkv_cols(w_in[l][:, :rwkv_cols], W, w_lora, a_lora),
                                  w_in[l][:, rwkv_cols:]], axis=1).astype(BF16)
        p = _norm_proj(xs, norm_mix[l].reshape(1, D) * (1.0 + sc1), sh1, w_in_p)
        mu_p = _pad_rwkv_cols(rwkv_mu[l].reshape(1, -1), W, w_lora, a_lora)
        vres = None
        if l > 0:
            vres = (rwkv_v0[l - 1].reshape(1, W), _pad_cols(rwkv_v_down[l - 1], V_LORA_PAD).astype(BF16),
                    _pad_rows(rwkv_v_up[l - 1], V_LORA_PAD).astype(BF16), v_first)
        r, lw, k, v, kk, kka, g = _rwkv_prep(
            p, mu_p, rwkv_w0[l].reshape(1, W), _pad_rows(rwkv_w_up[l], LORA_PAD).astype(BF16),
            rwkv_a0[l].reshape(1, W), _pad_rows(rwkv_a_up[l], LORA_PAD).astype(BF16),
            rwkv_g_up[l].astype(BF16), rwkv_k_k[l].reshape(1, W), rwkv_k_a[l].reshape(1, W), vres)
        if l == 0:
            v_first = v
        y_r = _rwkv_recurrence(r, lw, k, v, kk, kka, g, rwkv_r_k[l].reshape(1, W),
                               rwkv_gn_w[l].reshape(1, W), rwkv_gn_b[l].reshape(1, W))
        y_h = _hgrn(p, rwkv_cols_p, lbs[l].reshape(1, -1), hgrn_norm[l].reshape(1, -1))
        xs = _out_proj(y_r, y_h, w_out[l].astype(BF16), xs, g1)
        xs = _moe(xs, norm_ffn[l].reshape(1, D) * (1.0 + sc2), sh2, g2, router_w[l].T,
                  router_bias[l].reshape(-1, 1), expert_w_gate, expert_w_up, expert_w_down,
                  shared_w_gate, shared_w_up, shared_w_down, l,
                  final_norm.reshape(1, D) if l == L - 1 else None)
    return xs.reshape(B, T, D)
```

```python
import functools
import math

import jax
import jax.numpy as jnp
from jax import lax
from jax.experimental import pallas as pl
from jax.experimental.pallas import tpu as pltpu

F32 = jnp.float32
BF16 = jnp.bfloat16

LANES = 128
VMEM_LIMIT = 56 * 1024 * 1024

RWKV_HEAD = 64
HGRN_HEAD = 128
CHUNK = 64
SUB = 16
LORA_PAD = 128
G_LORA = 256
V_LORA_PAD = 128
N_EXPERTS = 64
N_GROUPS = 8
TOPK_GROUPS = 4
TOP_K = 8
ROUTE_SCALE = 2.5
MASKED_SCORE = -1e4
MIN_FORGET = 1e-30
NORM_EPS = 1e-6
GN_EPS = 64e-5
DECAY_SCALE = 0.6065306597126334

NN = (((1,), (0,)), ((), ()))
NT = (((1,), (1,)), ((), ()))
TN = (((0,), (0,)), ((), ()))


def _mm(a, b, dims=NN):
    return lax.dot_general(a.astype(BF16), b.astype(BF16), dims, preferred_element_type=F32)


def _mmx(a, b, dims=NN):
    return lax.dot_general(a.astype(F32), b.astype(F32), dims, preferred_element_type=F32,
                           precision=lax.Precision.HIGHEST)


def _split3(x):
    hi = x.astype(BF16)
    r1 = x - hi.astype(F32)
    mid = r1.astype(BF16)
    lo = (r1 - mid.astype(F32)).astype(BF16)
    return hi, mid, lo


def _mm_left01(a01, b, dims=NN):
    a = a01.astype(BF16)
    d = functools.partial(lax.dot_general, dimension_numbers=dims, preferred_element_type=F32)
    hi, mid, lo = _split3(b)
    return d(a, hi) + d(a, mid) + d(a, lo)


def _mm_right01(a, b01, dims=NN):
    b = b01.astype(BF16)
    d = functools.partial(lax.dot_general, dimension_numbers=dims, preferred_element_type=F32)
    hi, mid, lo = _split3(a)
    return d(hi, b) + d(mid, b) + d(lo, b)


def _sigmoid(x):
    return 1.0 / (1.0 + jnp.exp(-x))


def _silu(x):
    return x * _sigmoid(x)


def _cparams(sem):
    return pltpu.CompilerParams(dimension_semantics=sem, vmem_limit_bytes=VMEM_LIMIT)


def _mod_kernel(c_ref, w_ref, b_ref, o_ref):
    cond = _silu(c_ref[...])
    tn = w_ref.shape[2]
    cols = [jnp.sum(w_ref[0, :, j * LANES:(j + 1) * LANES] * cond, axis=0, keepdims=True)
            for j in range(tn // LANES)]
    o_ref[0] = jnp.concatenate(cols, axis=1) + b_ref[0]


def _modulation(c, w_mod, b_mod):
    L, D, N = w_mod.shape
    tn = 512
    c_lanes = jnp.broadcast_to(c.reshape(D, 1), (D, LANES))
    out = pl.pallas_call(
        _mod_kernel,
        grid=(L, N // tn),
        in_specs=[pl.BlockSpec((D, LANES), lambda l, j: (0, 0)),
                  pl.BlockSpec((1, D, tn), lambda l, j: (l, 0, j)),
                  pl.BlockSpec((1, 1, tn), lambda l, j: (l, 0, j))],
        out_specs=pl.BlockSpec((1, 1, tn), lambda l, j: (l, 0, j)),
        out_shape=jax.ShapeDtypeStruct((L, 1, N), F32),
        compiler_params=_cparams(("arbitrary", "arbitrary")),
        name="modulation",
    )(c_lanes, w_mod, b_mod.reshape(L, 1, N))
    return out[:, 0, :]


def _normmod(x, gain, shift):
    y = x * lax.rsqrt(jnp.mean(x * x, axis=-1, keepdims=True) + NORM_EPS)
    return y * gain + shift


def _norm_proj_kernel(x_ref, gain_ref, shift_ref, w_ref, o_ref, h_ref):
    @pl.when(pl.program_id(1) == 0)
    def _():
        h_ref[...] = _normmod(x_ref[...], gain_ref[...], shift_ref[...]).astype(BF16)

    o_ref[...] = jnp.dot(h_ref[...], w_ref[...], preferred_element_type=F32)


def _norm_proj(x, gain, shift, w, tm=512, tn=512):
    T, D = x.shape
    N = w.shape[1]
    return pl.pallas_call(
        _norm_proj_kernel,
        grid=(T // tm, N // tn),
        in_specs=[pl.BlockSpec((tm, D), lambda i, j: (i, 0)),
                  pl.BlockSpec((1, D), lambda i, j: (0, 0)),
                  pl.BlockSpec((1, D), lambda i, j: (0, 0)),
                  pl.BlockSpec((D, tn), lambda i, j: (0, j))],
        out_specs=pl.BlockSpec((tm, tn), lambda i, j: (i, j)),
        out_shape=jax.ShapeDtypeStruct((T, N), F32),
        scratch_shapes=[pltpu.VMEM((tm, D), BF16)],
        compiler_params=_cparams(("arbitrary", "arbitrary")),
        name="norm_proj",
    )(x, gain, shift, w)


def _pair_ones():
    r = lax.broadcasted_iota(jnp.int32, (LANES, LANES), 0) // RWKV_HEAD
    c = lax.broadcasted_iota(jnp.int32, (LANES, LANES), 1) // RWKV_HEAD
    return jnp.where(r == c, 1.0, 0.0).astype(F32)


def _rwkv_prep_kernel(has_vres, *refs):
    n_in = 20 if has_vres else 16
    (r_ref, k_ref, v_ref, lo_ref, rp_ref, kp_ref, vp_ref, lop_ref, mu_ref, w0_ref, wup_ref,
     a0_ref, aup_ref, gup_ref, kk_ref, ka_ref) = refs[:16]
    if has_vres:
        v0_ref, vdn_ref, vup_ref, vf_ref = refs[16:20]
    ro_ref, lwo_ref, ko_ref, vo_ref, kko_ref, kkao_ref, go_ref = refs[n_in:]
    first = pl.program_id(0) == 0
    W = r_ref.shape[1]

    def shift(cur_ref, prev_ref, mu):
        cur = cur_ref[...]
        prev_last = jnp.where(first, 0.0, prev_ref[7:8, :])
        rolled = pltpu.roll(cur, 1, axis=0)
        row = lax.broadcasted_iota(jnp.int32, cur.shape, 0)
        prev = jnp.where(row == 0, prev_last, rolled)
        return cur + (prev - cur) * mu

    r = shift(r_ref, rp_ref, mu_ref[:, 0:W])
    k = shift(k_ref, kp_ref, mu_ref[:, W:2 * W])
    v = shift(v_ref, vp_ref, mu_ref[:, 2 * W:3 * W])
    lo = shift(lo_ref, lop_ref, mu_ref[:, 3 * W:])
    w_lo = lo[:, 0:LORA_PAD]
    a_lo = lo[:, LORA_PAD:2 * LORA_PAD]
    g_lo = lo[:, 2 * LORA_PAD:]

    z = w0_ref[...] + _mm(jnp.tanh(w_lo), wup_ref[...])
    lw = -DECAY_SCALE * _sigmoid(z)
    a = _sigmoid(a0_ref[...] + _mm(a_lo, aup_ref[...]))
    g = _mm(_sigmoid(g_lo), gup_ref[...])
    if has_vres:
        mix = _sigmoid(v0_ref[...] + _mm(_mm(v, vdn_ref[...]), vup_ref[...]))
        v = v + (vf_ref[...] - v) * mix

    kkr = k * kk_ref[...]
    sq = kkr * kkr
    ones = _pair_ones()
    ss = jnp.concatenate(
        [_mm_right01(sq[:, j * LANES:(j + 1) * LANES], ones) for j in range(W // LANES)], axis=1)
    kk = kkr / jnp.maximum(jnp.sqrt(ss), 1e-12)
    k2 = k * (1.0 + (a - 1.0) * ka_ref[...])

    ro_ref[...] = r
    lwo_ref[...] = lw
    ko_ref[...] = k2
    vo_ref[...] = v
    kko_ref[...] = kk
    kkao_ref[...] = kk * a
    go_ref[...] = g


def _rwkv_prep(p, mu_p, w0, wup, a0, aup, gup, k_k, k_a, vres, tp=128):
    T = p.shape[0]
    W = w0.shape[1]
    lo_w = 2 * LORA_PAD + G_LORA
    nb = tp // 8
    lo_blk = 3 * W // lo_w

    def cur(cb, width):
        return pl.BlockSpec((tp, width), lambda i, cb=cb: (i, cb))

    def prev(cb, width):
        return pl.BlockSpec((8, width), lambda i, cb=cb: (jnp.maximum(i * nb - 1, 0), cb))

    def vec(n):
        return pl.BlockSpec((1, n), lambda i: (0, 0))

    def mat(a, b):
        return pl.BlockSpec((a, b), lambda i: (0, 0))

    in_specs = [cur(0, W), cur(1, W), cur(2, W), cur(lo_blk, lo_w),
                prev(0, W), prev(1, W), prev(2, W), prev(lo_blk, lo_w),
                vec(3 * W + lo_w), vec(W), mat(LORA_PAD, W), vec(W), mat(LORA_PAD, W),
                mat(G_LORA, W), vec(W), vec(W)]
    args = [p, p, p, p, p, p, p, p, mu_p, w0, wup, a0, aup, gup, k_k, k_a]
    if vres is not None:
        v0, vdn, vup, v_first = vres
        in_specs += [vec(W), mat(W, V_LORA_PAD), mat(V_LORA_PAD, W),
                     pl.BlockSpec((tp, W), lambda i: (i, 0))]
        args += [v0, vdn, vup, v_first]
    out_spec = pl.BlockSpec((tp, W), lambda i: (i, 0))
    sds = jax.ShapeDtypeStruct((T, W), F32)
    return pl.pallas_call(
        functools.partial(_rwkv_prep_kernel, vres is not None),
        grid=(T // tp,),
        in_specs=in_specs,
        out_specs=[out_spec] * 7,
        out_shape=[sds] * 7,
        compiler_params=_cparams(("arbitrary",)),
        name="rwkv_prep",
    )(*args)


def _tri_incl(n):
    row = lax.broadcasted_iota(jnp.int32, (n, n), 0)
    col = lax.broadcasted_iota(jnp.int32, (n, n), 1)
    return jnp.where(col <= row, 1.0, 0.0).astype(BF16)


def _rwkv_pair(r, c, lw, k, v, kk, kka, g, rk, gnw, gnb, st):
    C = r.shape[0]
    C2 = 2 * C
    cp = c - lw
    c_last = c[C - 1:C, :]
    e_c, e_cp, e_nc, e_dl = jnp.exp(c), jnp.exp(cp), jnp.exp(-c), jnp.exp(c_last - c)

    lane = lax.broadcasted_iota(jnp.int32, (C, LANES), 1)
    m0 = lane < RWKV_HEAD

    def stack(x):
        return jnp.concatenate([jnp.where(m0, x, 0.0), jnp.where(m0, 0.0, x)], axis=0)

    aq_rt = jnp.concatenate([stack(kk * e_cp), stack(r * e_c)], axis=0).astype(BF16)
    bt_kt = jnp.concatenate([stack(kka * e_nc), stack(k * e_nc)], axis=0).astype(BF16)
    kw_bw = jnp.concatenate([stack(k * e_dl), stack(kka * e_dl)], axis=0).astype(BF16)
    vs = stack(v)
    vs_b = vs.astype(BF16)

    row = lax.broadcasted_iota(jnp.int32, (C2, C2), 0)
    col = lax.broadcasted_iota(jnp.int32, (C2, C2), 1)
    t_i, s_j = row % C, col % C
    strict, incl, eye = s_j < t_i, s_j <= t_i, row == col

    yield
    p = _mm(aq_rt, bt_kt, NT)
    yield
    l_ab = jnp.where(strict, p[:C2, :C2], 0.0)
    l_ak = jnp.where(strict, p[:C2, C2:], 0.0)
    l_rb = jnp.where(incl, p[C2:, :C2], 0.0)
    l_rk = jnp.where(incl, p[C2:, C2:], 0.0)

    pw = -l_ab
    inv = jnp.where(eye, 1.0, 0.0) + pw
    n = 2
    while n < C:
        pw = _mm(pw, pw)
        yield
        inv = inv + _mm(inv, pw)
        n *= 2

    xy = _mm(aq_rt, st) + _mm(jnp.concatenate([l_ak, l_rk], axis=0), vs_b)
    yield
    u = _mm(inv, xy[:C2])
    yield
    y = xy[C2:] - _mm(l_rb, u)
    w_col = jnp.broadcast_to(jnp.exp(c_last), (C2, C2)).T
    st_new = st * w_col + _mm(kw_bw, jnp.concatenate([vs, -u], axis=0), TN)
    yield

    mh = (row // C) == (col // RWKV_HEAD)
    inv_n = 1.0 / RWKV_HEAD
    mean = jnp.sum(y, axis=1, keepdims=True) * inv_n
    yc = jnp.where(mh, y - mean, 0.0)
    var = jnp.sum(yc * yc, axis=1, keepdims=True) * inv_n
    yn = yc * lax.rsqrt(var + GN_EPS)
    bonus = jnp.sum(stack(r * k * rk), axis=1, keepdims=True) * vs
    yn_t = yn[:C] + yn[C:]
    bonus_t = bonus[:C] + bonus[C:]
    return (yn_t * gnw + gnb + bonus_t) * g, st_new


def _interleave(gens):
    results = [None] * len(gens)
    live = list(enumerate(gens))
    while live:
        still = []
        for j, gen in live:
            try:
                next(gen)
                still.append((j, gen))
            except StopIteration as stop:
                results[j] = stop.value
        live = still
    return results


def _rwkv_rec_kernel(r_ref, lw_ref, k_ref, v_ref, kk_ref, kka_ref, g_ref, rk_ref, gnw_ref, gnb_ref,
                     o_ref, st_ref):
    @pl.when(pl.program_id(1) == 0)
    def _():
        st_ref[...] = jnp.zeros_like(st_ref)

    rows, lanes = r_ref.shape
    tri = _tri_incl(CHUNK)

    def chunk(ci, carry):
        rs = pl.ds(pl.multiple_of(ci * CHUNK, CHUNK), CHUNK)
        lw_all = lw_ref[rs, :]
        c_all = _mm_left01(tri, lw_all)
        r, k, v, kk, kka, g = (ref[rs, :] for ref in (r_ref, k_ref, v_ref, kk_ref, kka_ref, g_ref))
        rk, gnw, gnb = rk_ref[...], gnw_ref[...], gnb_ref[...]
        gens = []
        for j in range(lanes // LANES):
            sl = slice(j * LANES, (j + 1) * LANES)
            gens.append(_rwkv_pair(r[:, sl], c_all[:, sl], lw_all[:, sl], k[:, sl], v[:, sl], kk[:, sl],
                                   kka[:, sl], g[:, sl], rk[:, sl], gnw[:, sl], gnb[:, sl], st_ref[j]))
        results = _interleave(gens)
        for j, (_, st_new) in enumerate(results):
            st_ref[j] = st_new
        o_ref[rs, :] = jnp.concatenate([out for out, _ in results], axis=1).astype(o_ref.dtype)
        return carry

    lax.fori_loop(0, rows // CHUNK, chunk, 0)


def _rwkv_recurrence(r, lw, k, v, kk, kka, g, rk, gnw, gnb, hb_lanes=1024, rows=512):
    T, W = r.shape
    rows = min(rows, T)
    seq = pl.BlockSpec((rows, hb_lanes), lambda h, t: (t, h))
    vec = pl.BlockSpec((1, hb_lanes), lambda h, t: (0, h))
    return pl.pallas_call(
        _rwkv_rec_kernel,
        grid=(W // hb_lanes, T // rows),
        in_specs=[seq] * 7 + [vec] * 3,
        out_specs=seq,
        out_shape=jax.ShapeDtypeStruct((T, W), BF16),
        scratch_shapes=[pltpu.VMEM((hb_lanes // LANES, LANES, LANES), F32)],
        compiler_params=_cparams(("arbitrary", "arbitrary")),
        name="rwkv_recurrence",
    )(r, lw, k, v, kk, kka, g, rk, gnw, gnb)


def _hgrn_head(q_raw, f_raw, i_c, g_raw, lb, gain, tri, s):
    C = q_raw.shape[0]
    q = _silu(q_raw)
    forget = lb + (1.0 - lb) * _sigmoid(f_raw)
    lf = jnp.log(jnp.maximum(forget, MIN_FORGET))
    kin = (1.0 - lb) * _sigmoid(-f_raw)
    b = _mm_left01(tri, lf)
    yield
    o = _mm(q * jnp.exp(b), s)

    lane_s = lax.broadcasted_iota(jnp.int32, (SUB, C), 1)
    row_s = lax.broadcasted_iota(jnp.int32, (SUB, C), 0)
    score_rows = []
    for i in range(C // SUB):
        lo, hi = i * SUB, (i + 1) * SUB
        b_i, q_i = b[lo:hi], q[lo:hi]
        blk = jnp.zeros((SUB, C), F32)
        for sidx in range(SUB):
            srow = lo + sidx
            d = jnp.exp(b_i - b[srow:srow + 1]) * q_i * kin[srow:srow + 1]
            blk = jnp.where(lane_s == srow, jnp.sum(d, axis=1, keepdims=True), blk)
        blk = jnp.where(lane_s <= row_s + lo, blk, 0.0)
        yield
        if i > 0:
            b_st = b[lo - 1:lo]
            qs = q_i * jnp.exp(b_i - b_st)
            ks = kin * jnp.exp(jnp.minimum(b_st - b, 0.0))
            blk = blk + jnp.where(lane_s < lo, _mm(qs, ks, NT), 0.0)
        score_rows.append(blk)
    scores = jnp.concatenate(score_rows, axis=0)
    o = o + _mm(scores, i_c)

    b_last = b[C - 1:C]
    f_col = jnp.broadcast_to(jnp.exp(b_last), (LANES, LANES)).T
    s_new = s * f_col + _mm(kin * jnp.exp(b_last - b), i_c, TN)

    o = o * lax.rsqrt(jnp.mean(o * o, axis=1, keepdims=True) + NORM_EPS)
    return o * gain * _sigmoid(g_raw), s_new


def _hgrn_kernel(q_ref, f_ref, i_ref, g_ref, lb_ref, gain_ref, o_ref, s_ref):
    @pl.when(pl.program_id(1) == 0)
    def _():
        s_ref[...] = jnp.zeros_like(s_ref)

    rows, lanes = q_ref.shape
    tri = _tri_incl(CHUNK)

    def chunk(ci, carry):
        rs = pl.ds(pl.multiple_of(ci * CHUNK, CHUNK), CHUNK)
        q, f, i_c, g = (ref[rs, :] for ref in (q_ref, f_ref, i_ref, g_ref))
        lb, gain = lb_ref[...], gain_ref[...]
        gens = []
        for j in range(lanes // LANES):
            sl = slice(j * LANES, (j + 1) * LANES)
            gens.append(_hgrn_head(q[:, sl], f[:, sl], i_c[:, sl], g[:, sl], lb[:, sl], gain[:, sl],
                                   tri, s_ref[j]))
        results = _interleave(gens)
        for j, (_, s_new) in enumerate(results):
            s_ref[j] = s_new
        o_ref[rs, :] = jnp.concatenate([out for out, _ in results], axis=1).astype(o_ref.dtype)
        return carry

    lax.fori_loop(0, rows // CHUNK, chunk, 0)


def _hgrn(p, col0, lb, gain, hb_lanes=512, rows=512):
    T = p.shape[0]
    W = lb.shape[1]
    rows = min(rows, T)
    nb = W // hb_lanes
    b0 = col0 // hb_lanes

    def sec(n):
        return pl.BlockSpec((rows, hb_lanes), lambda h, t, n=n: (t, b0 + n * nb + h))

    vec = pl.BlockSpec((1, hb_lanes), lambda h, t: (0, h))
    return pl.pallas_call(
        _hgrn_kernel,
        grid=(nb, T // rows),
        in_specs=[sec(0), sec(1), sec(2), sec(3), vec, vec],
        out_specs=pl.BlockSpec((rows, hb_lanes), lambda h, t: (t, h)),
        out_shape=jax.ShapeDtypeStruct((T, W), BF16),
        scratch_shapes=[pltpu.VMEM((hb_lanes // LANES, LANES, LANES), F32)],
        compiler_params=_cparams(("arbitrary", "arbitrary")),
        name="hgrn2",
    )(p, p, p, p, lb, gain)


def _out_proj_kernel(ya_ref, yb_ref, wa_ref, wb_ref, x_ref, gate_ref, o_ref):
    acc = jnp.dot(ya_ref[...], wa_ref[...], preferred_element_type=F32)
    acc = acc + jnp.dot(yb_ref[...], wb_ref[...], preferred_element_type=F32)
    o_ref[...] = x_ref[...] + gate_ref[...] * acc


def _out_proj(ya, yb, w, x, gate, tm=512, tn=1024):
    T, Ka = ya.shape
    Kb = yb.shape[1]
    D = w.shape[1]
    return pl.pallas_call(
        _out_proj_kernel,
        grid=(T // tm, D // tn),
        in_specs=[pl.BlockSpec((tm, Ka), lambda i, j: (i, 0)),
                  pl.BlockSpec((tm, Kb), lambda i, j: (i, 0)),
                  pl.BlockSpec((Ka, tn), lambda i, j: (0, j)),
                  pl.BlockSpec((Kb, tn), lambda i, j: (Ka // Kb, j)),
                  pl.BlockSpec((tm, tn), lambda i, j: (i, j)),
                  pl.BlockSpec((1, tn), lambda i, j: (0, j))],
        out_specs=pl.BlockSpec((tm, tn), lambda i, j: (i, j)),
        out_shape=jax.ShapeDtypeStruct((T, D), F32),
        compiler_params=_cparams(("arbitrary", "arbitrary")),
        name="out_proj",
    )(ya, yb, w, w, x, gate)


def _first_max_onehot(work, axis, size):
    m = jnp.max(work, axis=axis, keepdims=True)
    idx = lax.broadcasted_iota(jnp.int32, work.shape, axis)
    first = jnp.min(jnp.where(work == m, idx, size), axis=axis, keepdims=True)
    return idx == first, m


def _pitch(s_per):
    return s_per + 8 if s_per % 16 == 0 else s_per


def _store_token_major(ref, val):
    n, d = val.shape
    pitch = ref.shape[0] // n
    for s in range(d // LANES):
        ref[pl.ds(s, n, stride=pitch), :] = val[:, s * LANES:(s + 1) * LANES]
    for s in range(d // LANES, pitch):
        ref[pl.ds(s, n, stride=pitch), :] = jnp.zeros((n, LANES), F32)


def _router_kernel(x_ref, gain_ref, shift_ref, rwt_ref, bias_ref,
                   h_ref, wts_ref, idx_ref, rank_ref, cnt_ref, carry_ref):
    @pl.when(pl.program_id(0) == 0)
    def _():
        carry_ref[...] = jnp.zeros_like(carry_ref)

    h = _normmod(x_ref[...], gain_ref[...], shift_ref[...])
    _store_token_major(h_ref, h)
    tm = h.shape[0]
    per_group = N_EXPERTS // N_GROUPS
    logits = _mmx(rwt_ref[...], h, NT)
    scores = _sigmoid(logits)
    biased = scores + bias_ref[...]

    b3 = biased.reshape(N_GROUPS, per_group, tm)
    pick1, m1 = _first_max_onehot(b3, 1, per_group)
    m2 = jnp.max(jnp.where(pick1, -jnp.inf, b3), axis=1, keepdims=True)
    gscore = (m1 + m2).reshape(N_GROUPS, tm)

    work = gscore
    gsel = jnp.zeros_like(gscore)
    for _ in range(TOPK_GROUPS):
        pick, _m = _first_max_onehot(work, 0, N_GROUPS)
        gsel = jnp.where(pick, 1.0, gsel)
        work = jnp.where(pick, -jnp.inf, work)
    ok = jnp.broadcast_to(gsel.reshape(N_GROUPS, 1, tm), (N_GROUPS, per_group, tm)).reshape(N_EXPERTS, tm)
    work = jnp.where(ok > 0.5, biased, MASKED_SCORE)
    picks = []
    for _ in range(TOP_K):
        pick, _m = _first_max_onehot(work, 0, N_EXPERTS)
        picks.append(pick)
        work = jnp.where(pick, -jnp.inf, work)
    picked = functools.reduce(lambda a, b: a + b, [jnp.where(p, 1.0, 0.0) for p in picks])
    sel = picked * scores
    gates_t = sel / jnp.sum(sel, axis=0, keepdims=True) * ROUTE_SCALE

    ra = lax.broadcasted_iota(jnp.int32, (tm, tm), 0)
    rb = lax.broadcasted_iota(jnp.int32, (tm, tm), 1)
    before = jnp.where(ra < rb, 1.0, 0.0).astype(BF16)
    carry = carry_ref[:, 0:1]
    rank_full = _mm(picked, before) + carry
    e_iota = lax.broadcasted_iota(jnp.int32, (N_EXPERTS, tm), 0).astype(F32)

    def slot_rows(table):
        rows = [jnp.sum(jnp.where(p, table, 0.0), axis=0, keepdims=True) for p in picks]
        return jnp.concatenate(rows, axis=0)

    idx_ref[...] = slot_rows(e_iota).astype(jnp.int32)
    rank_ref[...] = slot_rows(rank_full).astype(jnp.int32)
    wts_ref[...] = slot_rows(gates_t)
    new_carry = carry + jnp.sum(picked, axis=1, keepdims=True)
    carry_ref[...] = jnp.broadcast_to(new_carry, carry_ref.shape)
    cnt_ref[...] = jnp.broadcast_to(new_carry, cnt_ref.shape)


def _router(x, gain, shift, rw_t, bias, tm=256):
    T, D = x.shape
    tm = min(tm, T)
    P = _pitch(D // LANES)
    return pl.pallas_call(
        _router_kernel,
        grid=(T // tm,),
        in_specs=[pl.BlockSpec((tm, D), lambda i: (i, 0)),
                  pl.BlockSpec((1, D), lambda i: (0, 0)),
                  pl.BlockSpec((1, D), lambda i: (0, 0)),
                  pl.BlockSpec((N_EXPERTS, D), lambda i: (0, 0)),
                  pl.BlockSpec((N_EXPERTS, 1), lambda i: (0, 0))],
        out_specs=[pl.BlockSpec((tm * P, LANES), lambda i: (i, 0)),
                   pl.BlockSpec((TOP_K, tm), lambda i: (0, i)),
                   pl.BlockSpec((TOP_K, tm), lambda i: (0, i)),
                   pl.BlockSpec((TOP_K, tm), lambda i: (0, i)),
                   pl.BlockSpec((N_EXPERTS, LANES), lambda i: (0, 0))],
        out_shape=[jax.ShapeDtypeStruct((T * P, LANES), F32),
                   jax.ShapeDtypeStruct((TOP_K, T), F32),
                   jax.ShapeDtypeStruct((TOP_K, T), jnp.int32),
                   jax.ShapeDtypeStruct((TOP_K, T), jnp.int32),
                   jax.ShapeDtypeStruct((N_EXPERTS, LANES), F32)],
        scratch_shapes=[pltpu.VMEM((N_EXPERTS, LANES), F32)],
        compiler_params=_cparams(("arbitrary",)),
        name="router",
    )(x, gain, shift, rw_t, bias)


def _dispatch_kernel(P, idx_ref, rank_ref, row_start_ref, pad_start_ref, pad_len_ref, nu_ref, h_ref, xs_hbm,
                     zero_ref, sem, pad_sem):
    i = pl.program_id(0)
    tc = idx_ref.shape[1]
    tile_rows = zero_ref.shape[0]

    def slab(ref, row):
        return ref.at[pl.ds(pl.multiple_of(row * P, math.gcd(P, 8)), P)]

    def token(n, carry):
        src = slab(h_ref, n)
        for j in range(TOP_K):
            dest = row_start_ref[idx_ref[j, n]] + rank_ref[j, n]
            pltpu.make_async_copy(src, slab(xs_hbm, dest), sem).start()
        return carry

    lax.fori_loop(0, tc, token, 0)

    @pl.when(i == 0)
    def _():
        zero_ref[...] = jnp.zeros_like(zero_ref)
        zero_slab = zero_ref.at[pl.ds(0, P)]

        def expert(e, carry):
            def fill(r, c):
                pltpu.make_async_copy(zero_slab, slab(xs_hbm, pad_start_ref[e] + r), pad_sem).start()
                return c

            def drain(r, c):
                pltpu.make_async_copy(zero_slab, slab(xs_hbm, 0), pad_sem).wait()
                return c

            lax.fori_loop(0, pad_len_ref[e], fill, 0)
            lax.fori_loop(0, pad_len_ref[e], drain, 0)
            return carry

        lax.fori_loop(0, N_EXPERTS, expert, 0)

        def tile_dst(t):
            return xs_hbm.at[pl.ds(pl.multiple_of(t * tile_rows, 8), tile_rows)]

        def fill_tile(t, c):
            pltpu.make_async_copy(zero_ref, tile_dst(t), pad_sem).start()
            return c

        def drain_tile(t, c):
            pltpu.make_async_copy(zero_ref, tile_dst(0), pad_sem).wait()
            return c

        n_tiles = xs_hbm.shape[0] // tile_rows
        lax.fori_loop(nu_ref[0], n_tiles, fill_tile, 0)
        lax.fori_loop(nu_ref[0], n_tiles, drain_tile, 0)

    for j in range(TOP_K):
        pltpu.make_async_copy(h_ref, xs_hbm.at[pl.ds(0, tc * P)], sem).wait()


def _dispatch(idx, rank, row_start, pad_start, pad_len, n_used, h_tok, n_tiles, tm_e, tc):
    T = idx.shape[1]
    P = h_tok.shape[0] // T
    smem = pl.BlockSpec(memory_space=pltpu.SMEM)
    slots = pl.BlockSpec((TOP_K, tc), lambda i: (0, i), memory_space=pltpu.SMEM)
    return pl.pallas_call(
        functools.partial(_dispatch_kernel, P),
        grid=(T // tc,),
        in_specs=[slots, slots, smem, smem, smem, smem,
                  pl.BlockSpec((tc * P, LANES), lambda i: (i, 0))],
        out_specs=pl.BlockSpec(memory_space=pl.ANY),
        out_shape=jax.ShapeDtypeStruct((n_tiles * tm_e * P, LANES), F32),
        scratch_shapes=[pltpu.VMEM((tm_e * P, LANES), F32), pltpu.SemaphoreType.DMA,
                        pltpu.SemaphoreType.DMA],
        compiler_params=_cparams(("arbitrary",)),
        name="dispatch",
    )(idx, rank, row_start, pad_start, pad_len, n_used, h_tok)


def _experts_kernel(te_ref, nu_ref, x_ref, wg_ref, wu_ref, wd_ref, o_ref):
    S = wg_ref.shape[2] // LANES
    P = _pitch(S)
    tm = x_ref.shape[0] // P
    kc = 8 if S % 8 == 0 else S

    @pl.when(pl.program_id(0) >= nu_ref[0])
    def _():
        o_ref[...] = jnp.zeros_like(o_ref)

    @pl.when(pl.program_id(0) < nu_ref[0])
    def _():
        g = u = None
        for c0 in range(0, S, kc):
            xk = jnp.concatenate([x_ref[pl.ds(s, tm, stride=P), :] for s in range(c0, c0 + kc)],
                                 axis=1).astype(BF16)
            ks = slice(c0 * LANES, (c0 + kc) * LANES)
            gk = jnp.dot(xk, wg_ref[0, 0, ks, :].astype(BF16), preferred_element_type=F32)
            uk = jnp.dot(xk, wu_ref[0, 0, ks, :].astype(BF16), preferred_element_type=F32)
            g = gk if g is None else g + gk
            u = uk if u is None else u + uk
        act = (_silu(g) * u).astype(BF16)
        for c0 in range(0, S, kc):
            ns = slice(c0 * LANES, (c0 + kc) * LANES)
            y = jnp.dot(act, wd_ref[0, 0, :, ns].astype(BF16), preferred_element_type=F32)
            for s in range(kc):
                o_ref[pl.ds(c0 + s, tm, stride=P), :] = y[:, s * LANES:(s + 1) * LANES]
        for s in range(S, P):
            o_ref[pl.ds(s, tm, stride=P), :] = jnp.zeros((tm, LANES), F32)


def _experts(tile_expert, n_used, x_tok, wg, wu, wd, layer, tm):
    _, E, D, ff = wg.shape
    P = _pitch(D // LANES)
    n_tiles = x_tok.shape[0] // (tm * P)

    def row_map(i, te, nu):
        return (jnp.minimum(i, nu[0] - 1), 0)

    def w_map(i, te, nu):
        return (layer, te[i], 0, 0)

    grid_spec = pltpu.PrefetchScalarGridSpec(
        num_scalar_prefetch=2,
        grid=(n_tiles,),
        in_specs=[pl.BlockSpec((tm * P, LANES), row_map),
                  pl.BlockSpec((1, 1, D, ff), w_map),
                  pl.BlockSpec((1, 1, D, ff), w_map),
                  pl.BlockSpec((1, 1, ff, D), w_map)],
        out_specs=pl.BlockSpec((tm * P, LANES), lambda i, te, nu: (i, 0)),
    )
    return pl.pallas_call(
        _experts_kernel,
        grid_spec=grid_spec,
        out_shape=jax.ShapeDtypeStruct(x_tok.shape, F32),
        compiler_params=_cparams(("arbitrary",)),
        name="experts",
    )(tile_expert, n_used, x_tok, wg, wu, wd)


def _combine_kernel(has_norm, idx_ref, rank_ref, row_start_ref, w_ref, ysh_ref, x_ref, gate2_ref, *rest):
    if has_norm:
        norm_gain_ref, y_hbm, o_ref, buf_ref, acc_ref, sem = rest
    else:
        y_hbm, o_ref, buf_ref, acc_ref, sem = rest
    tc, D = x_ref.shape
    S = D // LANES
    P = _pitch(S)
    align = math.gcd(P, 8)
    slot_rows = tc * P

    def token(n, carry):
        for j in range(TOP_K):
            dest = row_start_ref[idx_ref[j, n]] + rank_ref[j, n]
            src = y_hbm.at[pl.ds(pl.multiple_of(dest * P, align), S)]
            dst = buf_ref.at[pl.ds(pl.multiple_of(j * slot_rows + n * P, align), S)]
            pltpu.make_async_copy(src, dst, sem).start()
        return carry

    lax.fori_loop(0, tc, token, 0)
    n_rows = TOP_K * tc * S
    pltpu.make_async_copy(y_hbm.at[pl.ds(0, n_rows)], buf_ref.at[pl.ds(0, n_rows)], sem).wait()

    def accumulate(n, carry):
        row = pl.multiple_of(n * P, align)
        acc = ysh_ref[pl.ds(row, S), :]
        for j in range(TOP_K):
            slab = buf_ref[pl.ds(pl.multiple_of(j * slot_rows + n * P, align), S), :]
            acc = acc + w_ref[j, n] * slab
        acc_ref[pl.ds(row, S), :] = acc
        return carry

    lax.fori_loop(0, tc, accumulate, 0, unroll=2)
    for s in range(S):
        ls = slice(s * LANES, (s + 1) * LANES)
        o_ref[:, ls] = x_ref[:, ls] + gate2_ref[:, ls] * acc_ref[pl.ds(s, tc, stride=P), :]
    if has_norm:
        xo = o_ref[...]
        o_ref[...] = xo * lax.rsqrt(jnp.mean(xo * xo, axis=-1, keepdims=True) + NORM_EPS) * norm_gain_ref[...]


def _combine(idx, rank, row_start, wts, y_shared, x, gate2, y_sorted, norm_gain=None, tc=128):
    T, D = x.shape
    tc = min(tc, T)
    P = _pitch(D // LANES)
    slots = pl.BlockSpec((TOP_K, tc), lambda i: (0, i), memory_space=pltpu.SMEM)
    vec = pl.BlockSpec((1, D), lambda i: (0, 0))
    in_specs = [slots, slots, pl.BlockSpec(memory_space=pltpu.SMEM), slots,
                pl.BlockSpec((tc * P, LANES), lambda i: (i, 0)),
                pl.BlockSpec((tc, D), lambda i: (i, 0)), vec]
    args = [idx, rank, row_start, wts, y_shared, x, gate2]
    if norm_gain is not None:
        in_specs.append(vec)
        args.append(norm_gain)
    in_specs.append(pl.BlockSpec(memory_space=pl.ANY))
    args.append(y_sorted)
    return pl.pallas_call(
        functools.partial(_combine_kernel, norm_gain is not None),
        grid=(T // tc,),
        in_specs=in_specs,
        out_specs=pl.BlockSpec((tc, D), lambda i: (i, 0)),
        out_shape=jax.ShapeDtypeStruct((T, D), F32),
        scratch_shapes=[pltpu.VMEM((TOP_K * tc * P, LANES), F32), pltpu.VMEM((tc * P, LANES), F32),
                        pltpu.SemaphoreType.DMA],
        compiler_params=_cparams(("arbitrary",)),
        name="combine",
    )(*args)


def _moe(x, gain, shift, gate2, rw_t, bias, wg, wu, wd, swg, swu, swd, layer, norm_gain=None, tm_e=256,
         tc_d=256):
    T, D = x.shape
    tm_e = min(tm_e, T)
    tc_d = min(tc_d, T)
    h_tok, wts, idx, rank, cnt = _router(x, gain, shift, rw_t, bias)
    counts = cnt[:, 0].astype(jnp.int32)
    tiles = (counts + tm_e - 1) // tm_e
    tiles_cum = jnp.cumsum(tiles)
    row_start = (tiles_cum - tiles) * tm_e
    n_tiles = (T * TOP_K) // tm_e + N_EXPERTS
    tile_ids = jnp.arange(n_tiles, dtype=jnp.int32)
    tile_expert = jnp.minimum(jnp.sum((tiles_cum[None, :] <= tile_ids[:, None]).astype(jnp.int32), axis=1),
                              N_EXPERTS - 1)
    n_used = tiles_cum[-1:].astype(jnp.int32)
    x_sorted = _dispatch(idx, rank, row_start, row_start + counts, tiles * tm_e - counts, n_used, h_tok,
                         n_tiles, tm_e, tc_d)
    y_sorted = _experts(tile_expert, n_used, x_sorted, wg, wu, wd, layer, tm_e)
    n_sh = T // tm_e
    y_shared = _experts(jnp.zeros((n_sh,), jnp.int32), jnp.full((1,), n_sh, jnp.int32), h_tok,
                        swg[:, None], swu[:, None], swd[:, None], layer, tm_e)
    return _combine(idx, rank, row_start, wts, y_shared, x, gate2, y_sorted, norm_gain)


def _pad_cols(a, n):
    return jnp.pad(a, ((0, 0), (0, n - a.shape[1])))


def _pad_rows(a, n):
    return jnp.pad(a, ((0, n - a.shape[0]), (0, 0)))


def _pad_rwkv_cols(a, W, w_lora, a_lora):
    c0 = 3 * W
    return jnp.concatenate([a[:, :c0], _pad_cols(a[:, c0:c0 + w_lora], LORA_PAD),
                            _pad_cols(a[:, c0 + w_lora:c0 + w_lora + a_lora], LORA_PAD),
                            a[:, c0 + w_lora + a_lora:]], axis=1)


def _pack_w_in_kernel(W, w_lora, a_lora, w_ref, o_ref):
    tk = o_ref.shape[0]
    c0 = 3 * W
    c1, c2 = c0 + w_lora, c0 + w_lora + a_lora
    rw = c2 + G_LORA
    o_ref[:, 0:c0] = w_ref[0, :, 0:c0].astype(BF16)
    lo = jnp.concatenate([w_ref[0, :, c0:c1], jnp.zeros((tk, LORA_PAD - w_lora), F32),
                          w_ref[0, :, c1:c2], jnp.zeros((tk, LORA_PAD - a_lora), F32),
                          w_ref[0, :, c2:rw]], axis=1)
    o_ref[:, c0:c0 + 2 * LORA_PAD + G_LORA] = lo.astype(BF16)
    o_ref[:, c0 + 2 * LORA_PAD + G_LORA:] = w_ref[0, :, rw:].astype(BF16)


def _pack_w_in(w_in, layer, W, w_lora, a_lora, tk=128):
    _, D, N = w_in.shape
    n_out = N + 2 * LORA_PAD - w_lora - a_lora
    return pl.pallas_call(
        functools.partial(_pack_w_in_kernel, W, w_lora, a_lora),
        grid=(D // tk,),
        in_specs=[pl.BlockSpec((1, tk, N), lambda i: (layer, i, 0))],
        out_specs=pl.BlockSpec((tk, n_out), lambda i: (i, 0)),
        out_shape=jax.ShapeDtypeStruct((D, n_out), BF16),
        compiler_params=_cparams(("arbitrary",)),
        name="pack_w_in",
    )(w_in)


def kernel(x, c, w_mod, b_mod, norm_mix, norm_ffn, w_in, rwkv_mu, rwkv_w0, rwkv_w_up, rwkv_a0, rwkv_a_up, rwkv_g_up, rwkv_k_k, rwkv_k_a, rwkv_r_k, rwkv_gn_w, rwkv_gn_b, rwkv_v0, rwkv_v_down, rwkv_v_up, hgrn_lower_bounds, hgrn_norm, w_out, router_w, router_bias, expert_w_gate, expert_w_up, expert_w_down, shared_w_gate, shared_w_up, shared_w_down, final_norm):
    B, T, D = x.shape
    L = w_mod.shape[0]
    W = rwkv_w0.shape[1]
    w_lora, a_lora = rwkv_w_up.shape[1], rwkv_a_up.shape[1]
    rwkv_cols = rwkv_mu.shape[1]
    rwkv_cols_p = 3 * W + 2 * LORA_PAD + G_LORA

    lb_soft = jax.nn.softmax(hgrn_lower_bounds.astype(F32), axis=0)
    lbs = jnp.cumsum(lb_soft, axis=0) - lb_soft[0]
    mod = _modulation(c, w_mod, b_mod)

    xs = x.reshape(B * T, D)
    v_first = None
    for l in range(L):
        sh1, sc1, g1, sh2, sc2, g2 = [mod[l, n * D:(n + 1) * D].reshape(1, D) for n in range(6)]
        w_in_p = _pack_w_in(w_in, l, W, w_lora, a_lora)
        p = _norm_proj(xs, norm_mix[l].reshape(1, D) * (1.0 + sc1), sh1, w_in_p)
        mu_p = _pad_rwkv_cols(rwkv_mu[l].reshape(1, -1), W, w_lora, a_lora)
        vres = None
        if l > 0:
            vres = (rwkv_v0[l - 1].reshape(1, W), _pad_cols(rwkv_v_down[l - 1], V_LORA_PAD).astype(BF16),
                    _pad_rows(rwkv_v_up[l - 1], V_LORA_PAD).astype(BF16), v_first)
        r, lw, k, v, kk, kka, g = _rwkv_prep(
            p, mu_p, rwkv_w0[l].reshape(1, W), _pad_rows(rwkv_w_up[l], LORA_PAD).astype(BF16),
            rwkv_a0[l].reshape(1, W), _pad_rows(rwkv_a_up[l], LORA_PAD).astype(BF16),
            rwkv_g_up[l].astype(BF16), rwkv_k_k[l].reshape(1, W), rwkv_k_a[l].reshape(1, W), vres)
        if l == 0:
            v_first = v
        y_r = _rwkv_recurrence(r, lw, k, v, kk, kka, g, rwkv_r_k[l].reshape(1, W),
                               rwkv_gn_w[l].reshape(1, W), rwkv_gn_b[l].reshape(1, W))
        y_h = _hgrn(p, rwkv_cols_p, lbs[l].reshape(1, -1), hgrn_norm[l].reshape(1, -1))
        xs = _out_proj(y_r, y_h, w_out[l].astype(BF16), xs, g1)
        xs = _moe(xs, norm_ffn[l].reshape(1, D) * (1.0 + sc2), sh2, g2, router_w[l].T,
                  router_bias[l].reshape(-1, 1), expert_w_gate, expert_w_up, expert_w_down,
                  shared_w_gate, shared_w_up, shared_w_down, l,
                  final_norm.reshape(1, D) if l == L - 1 else None)
    return xs.reshape(B, T, D)
```

```python
import functools
import math

import jax
import jax.numpy as jnp
from jax import lax
from jax.experimental import pallas as pl
from jax.experimental.pallas import tpu as pltpu

F32 = jnp.float32
BF16 = jnp.bfloat16

LANES = 128
VMEM_LIMIT = 56 * 1024 * 1024

RWKV_HEAD = 64
HGRN_HEAD = 128
CHUNK = 64
SUB = 16
LORA_PAD = 128
G_LORA = 256
V_LORA_PAD = 128
N_EXPERTS = 64
N_GROUPS = 8
TOPK_GROUPS = 4
TOP_K = 8
ROUTE_SCALE = 2.5
MASKED_SCORE = -1e4
MIN_FORGET = 1e-30
NORM_EPS = 1e-6
GN_EPS = 64e-5
DECAY_SCALE = 0.6065306597126334

NN = (((1,), (0,)), ((), ()))
NT = (((1,), (1,)), ((), ()))
TN = (((0,), (0,)), ((), ()))


def _mm(a, b, dims=NN):
    return lax.dot_general(a.astype(BF16), b.astype(BF16), dims, preferred_element_type=F32)


def _mmx(a, b, dims=NN):
    return lax.dot_general(a.astype(F32), b.astype(F32), dims, preferred_element_type=F32,
                           precision=lax.Precision.HIGHEST)


def _split3(x):
    hi = x.astype(BF16)
    r1 = x - hi.astype(F32)
    mid = r1.astype(BF16)
    lo = (r1 - mid.astype(F32)).astype(BF16)
    return hi, mid, lo


def _mm_left01(a01, b, dims=NN):
    a = a01.astype(BF16)
    d = functools.partial(lax.dot_general, dimension_numbers=dims, preferred_element_type=F32)
    hi, mid, lo = _split3(b)
    return d(a, hi) + d(a, mid) + d(a, lo)


def _mm_right01(a, b01, dims=NN):
    b = b01.astype(BF16)
    d = functools.partial(lax.dot_general, dimension_numbers=dims, preferred_element_type=F32)
    hi, mid, lo = _split3(a)
    return d(hi, b) + d(mid, b) + d(lo, b)


def _sigmoid(x):
    return 1.0 / (1.0 + jnp.exp(-x))


def _silu(x):
    return x * _sigmoid(x)


def _cparams(sem):
    return pltpu.CompilerParams(dimension_semantics=sem, vmem_limit_bytes=VMEM_LIMIT)


def _mod_kernel(c_ref, w_ref, b_ref, o_ref):
    cond = _silu(c_ref[...])
    tn = w_ref.shape[2]
    cols = [jnp.sum(w_ref[0, :, j * LANES:(j + 1) * LANES] * cond, axis=0, keepdims=True)
            for j in range(tn // LANES)]
    o_ref[0] = jnp.concatenate(cols, axis=1) + b_ref[0]


def _modulation(c, w_mod, b_mod):
    L, D, N = w_mod.shape
    tn = 512
    c_lanes = jnp.broadcast_to(c.reshape(D, 1), (D, LANES))
    out = pl.pallas_call(
        _mod_kernel,
        grid=(L, N // tn),
        in_specs=[pl.BlockSpec((D, LANES), lambda l, j: (0, 0)),
                  pl.BlockSpec((1, D, tn), lambda l, j: (l, 0, j)),
                  pl.BlockSpec((1, 1, tn), lambda l, j: (l, 0, j))],
        out_specs=pl.BlockSpec((1, 1, tn), lambda l, j: (l, 0, j)),
        out_shape=jax.ShapeDtypeStruct((L, 1, N), F32),
        compiler_params=_cparams(("arbitrary", "arbitrary")),
        name="modulation",
    )(c_lanes, w_mod, b_mod.reshape(L, 1, N))
    return out[:, 0, :]


def _normmod(x, gain, shift):
    y = x * lax.rsqrt(jnp.mean(x * x, axis=-1, keepdims=True) + NORM_EPS)
    return y * gain + shift


def _norm_proj_kernel(x_ref, gain_ref, shift_ref, w_ref, o_ref, h_ref):
    @pl.when(pl.program_id(1) == 0)
    def _():
        h_ref[...] = _normmod(x_ref[...], gain_ref[...], shift_ref[...]).astype(BF16)

    o_ref[...] = jnp.dot(h_ref[...], w_ref[...], preferred_element_type=F32)


def _norm_proj(x, gain, shift, w, tm=512, tn=512):
    T, D = x.shape
    N = w.shape[1]
    return pl.pallas_call(
        _norm_proj_kernel,
        grid=(T // tm, N // tn),
        in_specs=[pl.BlockSpec((tm, D), lambda i, j: (i, 0)),
                  pl.BlockSpec((1, D), lambda i, j: (0, 0)),
                  pl.BlockSpec((1, D), lambda i, j: (0, 0)),
                  pl.BlockSpec((D, tn), lambda i, j: (0, j))],
        out_specs=pl.BlockSpec((tm, tn), lambda i, j: (i, j)),
        out_shape=jax.ShapeDtypeStruct((T, N), F32),
        scratch_shapes=[pltpu.VMEM((tm, D), BF16)],
        compiler_params=_cparams(("arbitrary", "arbitrary")),
        name="norm_proj",
    )(x, gain, shift, w)


def _pair_ones():
    r = lax.broadcasted_iota(jnp.int32, (LANES, LANES), 0) // RWKV_HEAD
    c = lax.broadcasted_iota(jnp.int32, (LANES, LANES), 1) // RWKV_HEAD
    return jnp.where(r == c, 1.0, 0.0).astype(F32)


def _rwkv_prep_kernel(has_vres, *refs):
    n_in = 20 if has_vres else 16
    (r_ref, k_ref, v_ref, lo_ref, rp_ref, kp_ref, vp_ref, lop_ref, mu_ref, w0_ref, wup_ref,
     a0_ref, aup_ref, gup_ref, kk_ref, ka_ref) = refs[:16]
    if has_vres:
        v0_ref, vdn_ref, vup_ref, vf_ref = refs[16:20]
    ro_ref, lwo_ref, ko_ref, vo_ref, kko_ref, kkao_ref, go_ref = refs[n_in:]
    first = pl.program_id(0) == 0
    W = r_ref.shape[1]

    def shift(cur_ref, prev_ref, mu):
        cur = cur_ref[...]
        prev_last = jnp.where(first, 0.0, prev_ref[7:8, :])
        rolled = pltpu.roll(cur, 1, axis=0)
        row = lax.broadcasted_iota(jnp.int32, cur.shape, 0)
        prev = jnp.where(row == 0, prev_last, rolled)
        return cur + (prev - cur) * mu

    r = shift(r_ref, rp_ref, mu_ref[:, 0:W])
    k = shift(k_ref, kp_ref, mu_ref[:, W:2 * W])
    v = shift(v_ref, vp_ref, mu_ref[:, 2 * W:3 * W])
    lo = shift(lo_ref, lop_ref, mu_ref[:, 3 * W:])
    w_lo = lo[:, 0:LORA_PAD]
    a_lo = lo[:, LORA_PAD:2 * LORA_PAD]
    g_lo = lo[:, 2 * LORA_PAD:]

    z = w0_ref[...] + _mm(jnp.tanh(w_lo), wup_ref[...])
    lw = -DECAY_SCALE * _sigmoid(z)
    a = _sigmoid(a0_ref[...] + _mm(a_lo, aup_ref[...]))
    g = _mm(_sigmoid(g_lo), gup_ref[...])
    if has_vres:
        mix = _sigmoid(v0_ref[...] + _mm(_mm(v, vdn_ref[...]), vup_ref[...]))
        v = v + (vf_ref[...] - v) * mix

    kkr = k * kk_ref[...]
    sq = kkr * kkr
    ones = _pair_ones()
    ss = jnp.concatenate(
        [_mm_right01(sq[:, j * LANES:(j + 1) * LANES], ones) for j in range(W // LANES)], axis=1)
    kk = kkr / jnp.maximum(jnp.sqrt(ss), 1e-12)
    k2 = k * (1.0 + (a - 1.0) * ka_ref[...])

    ro_ref[...] = r
    lwo_ref[...] = lw
    ko_ref[...] = k2
    vo_ref[...] = v
    kko_ref[...] = kk
    kkao_ref[...] = kk * a
    go_ref[...] = g


def _rwkv_prep(p, mu_p, w0, wup, a0, aup, gup, k_k, k_a, vres, tp=128):
    T = p.shape[0]
    W = w0.shape[1]
    lo_w = 2 * LORA_PAD + G_LORA
    nb = tp // 8
    lo_blk = 3 * W // lo_w

    def cur(cb, width):
        return pl.BlockSpec((tp, width), lambda i, cb=cb: (i, cb))

    def prev(cb, width):
        return pl.BlockSpec((8, width), lambda i, cb=cb: (jnp.maximum(i * nb - 1, 0), cb))

    def vec(n):
        return pl.BlockSpec((1, n), lambda i: (0, 0))

    def mat(a, b):
        return pl.BlockSpec((a, b), lambda i: (0, 0))

    in_specs = [cur(0, W), cur(1, W), cur(2, W), cur(lo_blk, lo_w),
                prev(0, W), prev(1, W), prev(2, W), prev(lo_blk, lo_w),
                vec(3 * W + lo_w), vec(W), mat(LORA_PAD, W), vec(W), mat(LORA_PAD, W),
                mat(G_LORA, W), vec(W), vec(W)]
    args = [p, p, p, p, p, p, p, p, mu_p, w0, wup, a0, aup, gup, k_k, k_a]
    if vres is not None:
        v0, vdn, vup, v_first = vres
        in_specs += [vec(W), mat(W, V_LORA_PAD), mat(V_LORA_PAD, W),
                     pl.BlockSpec((tp, W), lambda i: (i, 0))]
        args += [v0, vdn, vup, v_first]
    out_spec = pl.BlockSpec((tp, W), lambda i: (i, 0))
    sds = jax.ShapeDtypeStruct((T, W), F32)
    return pl.pallas_call(
        functools.partial(_rwkv_prep_kernel, vres is not None),
        grid=(T // tp,),
        in_specs=in_specs,
        out_specs=[out_spec] * 7,
        out_shape=[sds] * 7,
        compiler_params=_cparams(("arbitrary",)),
        name="rwkv_prep",
    )(*args)


def _tri_incl(n):
    row = lax.broadcasted_iota(jnp.int32, (n, n), 0)
    col = lax.broadcasted_iota(jnp.int32, (n, n), 1)
    return jnp.where(col <= row, 1.0, 0.0).astype(BF16)


def _rwkv_pair(r, c, lw, k, v, kk, kka, g, rk, gnw, gnb, st):
    C = r.shape[0]
    C2 = 2 * C
    cp = c - lw
    c_last = c[C - 1:C, :]
    e_c, e_cp, e_nc, e_dl = jnp.exp(c), jnp.exp(cp), jnp.exp(-c), jnp.exp(c_last - c)

    lane = lax.broadcasted_iota(jnp.int32, (C, LANES), 1)
    m0 = lane < RWKV_HEAD

    def stack(x):
        return jnp.concatenate([jnp.where(m0, x, 0.0), jnp.where(m0, 0.0, x)], axis=0)

    aq_rt = jnp.concatenate([stack(kk * e_cp), stack(r * e_c)], axis=0).astype(BF16)
    bt_kt = jnp.concatenate([stack(kka * e_nc), stack(k * e_nc)], axis=0).astype(BF16)
    kw_bw = jnp.concatenate([stack(k * e_dl), stack(kka * e_dl)], axis=0).astype(BF16)
    vs = stack(v)
    vs_b = vs.astype(BF16)

    row = lax.broadcasted_iota(jnp.int32, (C2, C2), 0)
    col = lax.broadcasted_iota(jnp.int32, (C2, C2), 1)
    t_i, s_j = row % C, col % C
    strict, incl, eye = s_j < t_i, s_j <= t_i, row == col

    yield
    p = _mm(aq_rt, bt_kt, NT)
    yield
    l_ab = jnp.where(strict, p[:C2, :C2], 0.0)
    l_ak = jnp.where(strict, p[:C2, C2:], 0.0)
    l_rb = jnp.where(incl, p[C2:, :C2], 0.0)
    l_rk = jnp.where(incl, p[C2:, C2:], 0.0)

    pw = -l_ab
    inv = jnp.where(eye, 1.0, 0.0) + pw
    n = 2
    while n < C:
        pw = _mm(pw, pw)
        yield
        inv = inv + _mm(inv, pw)
        n *= 2

    xy = _mm(aq_rt, st) + _mm(jnp.concatenate([l_ak, l_rk], axis=0), vs_b)
    yield
    u = _mm(inv, xy[:C2])
    yield
    y = xy[C2:] - _mm(l_rb, u)
    w_col = jnp.broadcast_to(jnp.exp(c_last), (C2, C2)).T
    st_new = st * w_col + _mm(kw_bw, jnp.concatenate([vs, -u], axis=0), TN)
    yield

    mh = (row // C) == (col // RWKV_HEAD)
    inv_n = 1.0 / RWKV_HEAD
    mean = jnp.sum(y, axis=1, keepdims=True) * inv_n
    yc = jnp.where(mh, y - mean, 0.0)
    var = jnp.sum(yc * yc, axis=1, keepdims=True) * inv_n
    yn = yc * lax.rsqrt(var + GN_EPS)
    bonus = jnp.sum(stack(r * k * rk), axis=1, keepdims=True) * vs
    yn_t = yn[:C] + yn[C:]
    bonus_t = bonus[:C] + bonus[C:]
    return (yn_t * gnw + gnb + bonus_t) * g, st_new


def _interleave(gens):
    results = [None] * len(gens)
    live = list(enumerate(gens))
    while live:
        still = []
        for j, gen in live:
            try:
                next(gen)
                still.append((j, gen))
            except StopIteration as stop:
                results[j] = stop.value
        live = still
    return results


def _rwkv_rec_kernel(r_ref, lw_ref, k_ref, v_ref, kk_ref, kka_ref, g_ref, rk_ref, gnw_ref, gnb_ref,
                     o_ref, st_ref):
    @pl.when(pl.program_id(1) == 0)
    def _():
        st_ref[...] = jnp.zeros_like(st_ref)

    rows, lanes = r_ref.shape
    tri = _tri_incl(CHUNK)

    def chunk(ci, carry):
        rs = pl.ds(pl.multiple_of(ci * CHUNK, CHUNK), CHUNK)
        lw_all = lw_ref[rs, :]
        c_all = _mm_left01(tri, lw_all)
        r, k, v, kk, kka, g = (ref[rs, :] for ref in (r_ref, k_ref, v_ref, kk_ref, kka_ref, g_ref))
        rk, gnw, gnb = rk_ref[...], gnw_ref[...], gnb_ref[...]
        gens = []
        for j in range(lanes // LANES):
            sl = slice(j * LANES, (j + 1) * LANES)
            gens.append(_rwkv_pair(r[:, sl], c_all[:, sl], lw_all[:, sl], k[:, sl], v[:, sl], kk[:, sl],
                                   kka[:, sl], g[:, sl], rk[:, sl], gnw[:, sl], gnb[:, sl], st_ref[j]))
        results = _interleave(gens)
        for j, (_, st_new) in enumerate(results):
            st_ref[j] = st_new
        o_ref[rs, :] = jnp.concatenate([out for out, _ in results], axis=1).astype(o_ref.dtype)
        return carry

    lax.fori_loop(0, rows // CHUNK, chunk, 0)


def _rwkv_recurrence(r, lw, k, v, kk, kka, g, rk, gnw, gnb, hb_lanes=1024, rows=512):
    T, W = r.shape
    rows = min(rows, T)
    seq = pl.BlockSpec((rows, hb_lanes), lambda h, t: (t, h))
    vec = pl.BlockSpec((1, hb_lanes), lambda h, t: (0, h))
    return pl.pallas_call(
        _rwkv_rec_kernel,
        grid=(W // hb_lanes, T // rows),
        in_specs=[seq] * 7 + [vec] * 3,
        out_specs=seq,
        out_shape=jax.ShapeDtypeStruct((T, W), BF16),
        scratch_shapes=[pltpu.VMEM((hb_lanes // LANES, LANES, LANES), F32)],
        compiler_params=_cparams(("arbitrary", "arbitrary")),
        name="rwkv_recurrence",
    )(r, lw, k, v, kk, kka, g, rk, gnw, gnb)


def _hgrn_head(q_raw, f_raw, i_c, g_raw, lb, gain, tri, s):
    C = q_raw.shape[0]
    q = _silu(q_raw)
    forget = lb + (1.0 - lb) * _sigmoid(f_raw)
    lf = jnp.log(jnp.maximum(forget, MIN_FORGET))
    kin = (1.0 - lb) * _sigmoid(-f_raw)
    b = _mm_left01(tri, lf)
    yield
    o = _mm(q * jnp.exp(b), s)

    lane_s = lax.broadcasted_iota(jnp.int32, (SUB, C), 1)
    row_s = lax.broadcasted_iota(jnp.int32, (SUB, C), 0)
    score_rows = []
    for i in range(C // SUB):
        lo, hi = i * SUB, (i + 1) * SUB
        b_i, q_i = b[lo:hi], q[lo:hi]
        blk = jnp.zeros((SUB, C), F32)
        for sidx in range(SUB):
            srow = lo + sidx
            d = jnp.exp(b_i - b[srow:srow + 1]) * q_i * kin[srow:srow + 1]
            blk = jnp.where(lane_s == srow, jnp.sum(d, axis=1, keepdims=True), blk)
        blk = jnp.where(lane_s <= row_s + lo, blk, 0.0)
        yield
        if i > 0:
            b_st = b[lo - 1:lo]
            qs = q_i * jnp.exp(b_i - b_st)
            ks = kin * jnp.exp(jnp.minimum(b_st - b, 0.0))
            blk = blk + jnp.where(lane_s < lo, _mm(qs, ks, NT), 0.0)
        score_rows.append(blk)
    scores = jnp.concatenate(score_rows, axis=0)
    o = o + _mm(scores, i_c)

    b_last = b[C - 1:C]
    f_col = jnp.broadcast_to(jnp.exp(b_last), (LANES, LANES)).T
    s_new = s * f_col + _mm(kin * jnp.exp(b_last - b), i_c, TN)

    o = o * lax.rsqrt(jnp.mean(o * o, axis=1, keepdims=True) + NORM_EPS)
    return o * gain * _sigmoid(g_raw), s_new


def _hgrn_kernel(q_ref, f_ref, i_ref, g_ref, lb_ref, gain_ref, o_ref, s_ref):
    @pl.when(pl.program_id(1) == 0)
    def _():
        s_ref[...] = jnp.zeros_like(s_ref)

    rows, lanes = q_ref.shape
    tri = _tri_incl(CHUNK)

    def chunk(ci, carry):
        rs = pl.ds(pl.multiple_of(ci * CHUNK, CHUNK), CHUNK)
        q, f, i_c, g = (ref[rs, :] for ref in (q_ref, f_ref, i_ref, g_ref))
        lb, gain = lb_ref[...], gain_ref[...]
        gens = []
        for j in range(lanes // LANES):
            sl = slice(j * LANES, (j + 1) * LANES)
            gens.append(_hgrn_head(q[:, sl], f[:, sl], i_c[:, sl], g[:, sl], lb[:, sl], gain[:, sl],
                                   tri, s_ref[j]))
        results = _interleave(gens)
        for j, (_, s_new) in enumerate(results):
            s_ref[j] = s_new
        o_ref[rs, :] = jnp.concatenate([out for out, _ in results], axis=1).astype(o_ref.dtype)
        return carry

    lax.fori_loop(0, rows // CHUNK, chunk, 0)


def _hgrn(p, col0, lb, gain, hb_lanes=512, rows=512):
    T = p.shape[0]
    W = lb.shape[1]
    rows = min(rows, T)
    nb = W // hb_lanes
    b0 = col0 // hb_lanes

    def sec(n):
        return pl.BlockSpec((rows, hb_lanes), lambda h, t, n=n: (t, b0 + n * nb + h))

    vec = pl.BlockSpec((1, hb_lanes), lambda h, t: (0, h))
    return pl.pallas_call(
        _hgrn_kernel,
        grid=(nb, T // rows),
        in_specs=[sec(0), sec(1), sec(2), sec(3), vec, vec],
        out_specs=pl.BlockSpec((rows, hb_lanes), lambda h, t: (t, h)),
        out_shape=jax.ShapeDtypeStruct((T, W), BF16),
        scratch_shapes=[pltpu.VMEM((hb_lanes // LANES, LANES, LANES), F32)],
        compiler_params=_cparams(("arbitrary", "arbitrary")),
        name="hgrn2",
    )(p, p, p, p, lb, gain)


def _out_proj_kernel(ya_ref, yb_ref, wa_ref, wb_ref, x_ref, gate_ref, o_ref):
    acc = jnp.dot(ya_ref[...], wa_ref[...], preferred_element_type=F32)
    acc = acc + jnp.dot(yb_ref[...], wb_ref[...], preferred_element_type=F32)
    o_ref[...] = x_ref[...] + gate_ref[...] * acc


def _out_proj(ya, yb, w, x, gate, tm=512, tn=1024):
    T, Ka = ya.shape
    Kb = yb.shape[1]
    D = w.shape[1]
    return pl.pallas_call(
        _out_proj_kernel,
        grid=(T // tm, D // tn),
        in_specs=[pl.BlockSpec((tm, Ka), lambda i, j: (i, 0)),
                  pl.BlockSpec((tm, Kb), lambda i, j: (i, 0)),
                  pl.BlockSpec((Ka, tn), lambda i, j: (0, j)),
                  pl.BlockSpec((Kb, tn), lambda i, j: (Ka // Kb, j)),
                  pl.BlockSpec((tm, tn), lambda i, j: (i, j)),
                  pl.BlockSpec((1, tn), lambda i, j: (0, j))],
        out_specs=pl.BlockSpec((tm, tn), lambda i, j: (i, j)),
        out_shape=jax.ShapeDtypeStruct((T, D), F32),
        compiler_params=_cparams(("arbitrary", "arbitrary")),
        name="out_proj",
    )(ya, yb, w, w, x, gate)


def _first_max_onehot(work, axis, size):
    m = jnp.max(work, axis=axis, keepdims=True)
    idx = lax.broadcasted_iota(jnp.int32, work.shape, axis)
    first = jnp.min(jnp.where(work == m, idx, size), axis=axis, keepdims=True)
    return idx == first, m


def _pitch(s_per):
    return s_per + 8 if s_per % 16 == 0 else s_per


def _store_token_major(ref, val):
    n, d = val.shape
    pitch = ref.shape[0] // n
    for s in range(d // LANES):
        ref[pl.ds(s, n, stride=pitch), :] = val[:, s * LANES:(s + 1) * LANES]
    for s in range(d // LANES, pitch):
        ref[pl.ds(s, n, stride=pitch), :] = jnp.zeros((n, LANES), F32)


def _router_kernel(x_ref, gain_ref, shift_ref, rwt_ref, bias_ref,
                   h_ref, wts_ref, idx_ref, rank_ref, cnt_ref, carry_ref):
    @pl.when(pl.program_id(0) == 0)
    def _():
        carry_ref[...] = jnp.zeros_like(carry_ref)

    h = _normmod(x_ref[...], gain_ref[...], shift_ref[...])
    _store_token_major(h_ref, h)
    tm = h.shape[0]
    per_group = N_EXPERTS // N_GROUPS
    logits = _mmx(rwt_ref[...], h, NT)
    scores = _sigmoid(logits)
    biased = scores + bias_ref[...]

    b3 = biased.reshape(N_GROUPS, per_group, tm)
    pick1, m1 = _first_max_onehot(b3, 1, per_group)
    m2 = jnp.max(jnp.where(pick1, -jnp.inf, b3), axis=1, keepdims=True)
    gscore = (m1 + m2).reshape(N_GROUPS, tm)

    work = gscore
    gsel = jnp.zeros_like(gscore)
    for _ in range(TOPK_GROUPS):
        pick, _m = _first_max_onehot(work, 0, N_GROUPS)
        gsel = jnp.where(pick, 1.0, gsel)
        work = jnp.where(pick, -jnp.inf, work)
    ok = jnp.broadcast_to(gsel.reshape(N_GROUPS, 1, tm), (N_GROUPS, per_group, tm)).reshape(N_EXPERTS, tm)
    work = jnp.where(ok > 0.5, biased, MASKED_SCORE)
    picks = []
    for _ in range(TOP_K):
        pick, _m = _first_max_onehot(work, 0, N_EXPERTS)
        picks.append(pick)
        work = jnp.where(pick, -jnp.inf, work)
    picked = functools.reduce(lambda a, b: a + b, [jnp.where(p, 1.0, 0.0) for p in picks])
    sel = picked * scores
    gates_t = sel / jnp.sum(sel, axis=0, keepdims=True) * ROUTE_SCALE

    ra = lax.broadcasted_iota(jnp.int32, (tm, tm), 0)
    rb = lax.broadcasted_iota(jnp.int32, (tm, tm), 1)
    before = jnp.where(ra < rb, 1.0, 0.0).astype(BF16)
    carry = carry_ref[:, 0:1]
    rank_full = _mm(picked, before) + carry
    e_iota = lax.broadcasted_iota(jnp.int32, (N_EXPERTS, tm), 0).astype(F32)

    def slot_rows(table):
        rows = [jnp.sum(jnp.where(p, table, 0.0), axis=0, keepdims=True) for p in picks]
        return jnp.concatenate(rows, axis=0)

    idx_ref[...] = slot_rows(e_iota).astype(jnp.int32)
    rank_ref[...] = slot_rows(rank_full).astype(jnp.int32)
    wts_ref[...] = slot_rows(gates_t)
    new_carry = carry + jnp.sum(picked, axis=1, keepdims=True)
    carry_ref[...] = jnp.broadcast_to(new_carry, carry_ref.shape)
    cnt_ref[...] = jnp.broadcast_to(new_carry, cnt_ref.shape)


def _router(x, gain, shift, rw_t, bias, tm=256):
    T, D = x.shape
    tm = min(tm, T)
    P = _pitch(D // LANES)
    return pl.pallas_call(
        _router_kernel,
        grid=(T // tm,),
        in_specs=[pl.BlockSpec((tm, D), lambda i: (i, 0)),
                  pl.BlockSpec((1, D), lambda i: (0, 0)),
                  pl.BlockSpec((1, D), lambda i: (0, 0)),
                  pl.BlockSpec((N_EXPERTS, D), lambda i: (0, 0)),
                  pl.BlockSpec((N_EXPERTS, 1), lambda i: (0, 0))],
        out_specs=[pl.BlockSpec((tm * P, LANES), lambda i: (i, 0)),
                   pl.BlockSpec((TOP_K, tm), lambda i: (0, i)),
                   pl.BlockSpec((TOP_K, tm), lambda i: (0, i)),
                   pl.BlockSpec((TOP_K, tm), lambda i: (0, i)),
                   pl.BlockSpec((N_EXPERTS, LANES), lambda i: (0, 0))],
        out_shape=[jax.ShapeDtypeStruct((T * P, LANES), F32),
                   jax.ShapeDtypeStruct((TOP_K, T), F32),
                   jax.ShapeDtypeStruct((TOP_K, T), jnp.int32),
                   jax.ShapeDtypeStruct((TOP_K, T), jnp.int32),
                   jax.ShapeDtypeStruct((N_EXPERTS, LANES), F32)],
        scratch_shapes=[pltpu.VMEM((N_EXPERTS, LANES), F32)],
        compiler_params=_cparams(("arbitrary",)),
        name="router",
    )(x, gain, shift, rw_t, bias)


def _dispatch_kernel(P, idx_ref, rank_ref, row_start_ref, pad_start_ref, pad_len_ref, nu_ref, h_ref, xs_hbm,
                     zero_ref, sem, pad_sem):
    i = pl.program_id(0)
    tc = idx_ref.shape[1]
    tile_rows = zero_ref.shape[0]

    def slab(ref, row):
        return ref.at[pl.ds(pl.multiple_of(row * P, math.gcd(P, 8)), P)]

    def token(n, carry):
        src = slab(h_ref, n)
        for j in range(TOP_K):
            dest = row_start_ref[idx_ref[j, n]] + rank_ref[j, n]
            pltpu.make_async_copy(src, slab(xs_hbm, dest), sem).start()
        return carry

    lax.fori_loop(0, tc, token, 0)

    @pl.when(i == 0)
    def _():
        zero_ref[...] = jnp.zeros_like(zero_ref)
        zero_slab = zero_ref.at[pl.ds(0, P)]

        def expert(e, carry):
            def fill(r, c):
                pltpu.make_async_copy(zero_slab, slab(xs_hbm, pad_start_ref[e] + r), pad_sem).start()
                return c

            def drain(r, c):
                pltpu.make_async_copy(zero_slab, slab(xs_hbm, 0), pad_sem).wait()
                return c

            lax.fori_loop(0, pad_len_ref[e], fill, 0)
            lax.fori_loop(0, pad_len_ref[e], drain, 0)
            return carry

        lax.fori_loop(0, N_EXPERTS, expert, 0)

        def tile_dst(t):
            return xs_hbm.at[pl.ds(pl.multiple_of(t * tile_rows, 8), tile_rows)]

        def fill_tile(t, c):
            pltpu.make_async_copy(zero_ref, tile_dst(t), pad_sem).start()
            return c

        def drain_tile(t, c):
            pltpu.make_async_copy(zero_ref, tile_dst(0), pad_sem).wait()
            return c

        n_tiles = xs_hbm.shape[0] // tile_rows
        lax.fori_loop(nu_ref[0], n_tiles, fill_tile, 0)
        lax.fori_loop(nu_ref[0], n_tiles, drain_tile, 0)

    for j in range(TOP_K):
        pltpu.make_async_copy(h_ref, xs_hbm.at[pl.ds(0, tc * P)], sem).wait()


def _dispatch(idx, rank, row_start, pad_start, pad_len, n_used, h_tok, n_tiles, tm_e, tc):
    T = idx.shape[1]
    P = h_tok.shape[0] // T
    smem = pl.BlockSpec(memory_space=pltpu.SMEM)
    slots = pl.BlockSpec((TOP_K, tc), lambda i: (0, i), memory_space=pltpu.SMEM)
    return pl.pallas_call(
        functools.partial(_dispatch_kernel, P),
        grid=(T // tc,),
        in_specs=[slots, slots, smem, smem, smem, smem,
                  pl.BlockSpec((tc * P, LANES), lambda i: (i, 0))],
        out_specs=pl.BlockSpec(memory_space=pl.ANY),
        out_shape=jax.ShapeDtypeStruct((n_tiles * tm_e * P, LANES), F32),
        scratch_shapes=[pltpu.VMEM((tm_e * P, LANES), F32), pltpu.SemaphoreType.DMA,
                        pltpu.SemaphoreType.DMA],
        compiler_params=_cparams(("arbitrary",)),
        name="dispatch",
    )(idx, rank, row_start, pad_start, pad_len, n_used, h_tok)


def _experts_kernel(te_ref, nu_ref, x_ref, wgu_ref, wd_ref, o_ref):
    S = wgu_ref.shape[2] // LANES
    ff = wd_ref.shape[2]
    ffp = wgu_ref.shape[3] // 2
    P = _pitch(S)
    tm = x_ref.shape[0] // P
    kc = 8 if S % 8 == 0 else S

    @pl.when(pl.program_id(0) >= nu_ref[0])
    def _():
        o_ref[...] = jnp.zeros_like(o_ref)

    @pl.when(pl.program_id(0) < nu_ref[0])
    def _():
        gu = None
        for c0 in range(0, S, kc):
            xk = jnp.concatenate([x_ref[pl.ds(s, tm, stride=P), :] for s in range(c0, c0 + kc)],
                                 axis=1).astype(BF16)
            ks = slice(c0 * LANES, (c0 + kc) * LANES)
            gk = jnp.dot(xk, wgu_ref[0, 0, ks, :], preferred_element_type=F32)
            gu = gk if gu is None else gu + gk
        act = (_silu(gu[:, :ff]) * gu[:, ffp:ffp + ff]).astype(BF16)
        for c0 in range(0, S, kc):
            ns = slice(c0 * LANES, (c0 + kc) * LANES)
            y = jnp.dot(act, wd_ref[0, 0, :, ns].astype(BF16), preferred_element_type=F32)
            for s in range(kc):
                o_ref[pl.ds(c0 + s, tm, stride=P), :] = y[:, s * LANES:(s + 1) * LANES]
        for s in range(S, P):
            o_ref[pl.ds(s, tm, stride=P), :] = jnp.zeros((tm, LANES), F32)


def _experts(tile_expert, n_used, x_tok, wgu, wd, layer, tm):
    _, E, D, ffp2 = wgu.shape
    ff = wd.shape[2]
    P = _pitch(D // LANES)
    n_tiles = x_tok.shape[0] // (tm * P)

    def row_map(i, te, nu):
        return (jnp.minimum(i, nu[0] - 1), 0)

    def w_map(i, te, nu):
        return (layer, te[i], 0, 0)

    grid_spec = pltpu.PrefetchScalarGridSpec(
        num_scalar_prefetch=2,
        grid=(n_tiles,),
        in_specs=[pl.BlockSpec((tm * P, LANES), row_map),
                  pl.BlockSpec((1, 1, D, ffp2), w_map),
                  pl.BlockSpec((1, 1, ff, D), w_map)],
        out_specs=pl.BlockSpec((tm * P, LANES), lambda i, te, nu: (i, 0)),
    )
    return pl.pallas_call(
        _experts_kernel,
        grid_spec=grid_spec,
        out_shape=jax.ShapeDtypeStruct(x_tok.shape, F32),
        compiler_params=_cparams(("arbitrary",)),
        name="experts",
    )(tile_expert, n_used, x_tok, wgu, wd)


def _combine_kernel(has_norm, idx_ref, rank_ref, row_start_ref, w_ref, ysh_ref, x_ref, gate2_ref, *rest):
    if has_norm:
        norm_gain_ref, y_hbm, o_ref, buf_ref, acc_ref, sem = rest
    else:
        y_hbm, o_ref, buf_ref, acc_ref, sem = rest
    tc, D = x_ref.shape
    S = D // LANES
    P = _pitch(S)
    align = math.gcd(P, 8)
    slot_rows = tc * P

    def token(n, carry):
        for j in range(TOP_K):
            dest = row_start_ref[idx_ref[j, n]] + rank_ref[j, n]
            src = y_hbm.at[pl.ds(pl.multiple_of(dest * P, align), S)]
            dst = buf_ref.at[pl.ds(pl.multiple_of(j * slot_rows + n * P, align), S)]
            pltpu.make_async_copy(src, dst, sem).start()
        return carry

    lax.fori_loop(0, tc, token, 0)
    n_rows = TOP_K * tc * S
    pltpu.make_async_copy(y_hbm.at[pl.ds(0, n_rows)], buf_ref.at[pl.ds(0, n_rows)], sem).wait()

    def accumulate(n, carry):
        row = pl.multiple_of(n * P, align)
        acc = ysh_ref[pl.ds(row, S), :]
        for j in range(TOP_K):
            slab = buf_ref[pl.ds(pl.multiple_of(j * slot_rows + n * P, align), S), :]
            acc = acc + w_ref[j, n] * slab
        acc_ref[pl.ds(row, S), :] = acc
        return carry

    lax.fori_loop(0, tc, accumulate, 0, unroll=2)
    for s in range(S):
        ls = slice(s * LANES, (s + 1) * LANES)
        o_ref[:, ls] = x_ref[:, ls] + gate2_ref[:, ls] * acc_ref[pl.ds(s, tc, stride=P), :]
    if has_norm:
        xo = o_ref[...]
        o_ref[...] = xo * lax.rsqrt(jnp.mean(xo * xo, axis=-1, keepdims=True) + NORM_EPS) * norm_gain_ref[...]


def _combine(idx, rank, row_start, wts, y_shared, x, gate2, y_sorted, norm_gain=None, tc=128):
    T, D = x.shape
    tc = min(tc, T)
    P = _pitch(D // LANES)
    slots = pl.BlockSpec((TOP_K, tc), lambda i: (0, i), memory_space=pltpu.SMEM)
    vec = pl.BlockSpec((1, D), lambda i: (0, 0))
    in_specs = [slots, slots, pl.BlockSpec(memory_space=pltpu.SMEM), slots,
                pl.BlockSpec((tc * P, LANES), lambda i: (i, 0)),
                pl.BlockSpec((tc, D), lambda i: (i, 0)), vec]
    args = [idx, rank, row_start, wts, y_shared, x, gate2]
    if norm_gain is not None:
        in_specs.append(vec)
        args.append(norm_gain)
    in_specs.append(pl.BlockSpec(memory_space=pl.ANY))
    args.append(y_sorted)
    return pl.pallas_call(
        functools.partial(_combine_kernel, norm_gain is not None),
        grid=(T // tc,),
        in_specs=in_specs,
        out_specs=pl.BlockSpec((tc, D), lambda i: (i, 0)),
        out_shape=jax.ShapeDtypeStruct((T, D), F32),
        scratch_shapes=[pltpu.VMEM((TOP_K * tc * P, LANES), F32), pltpu.VMEM((tc * P, LANES), F32),
                        pltpu.SemaphoreType.DMA],
        compiler_params=_cparams(("arbitrary",)),
        name="combine",
    )(*args)


def _pack_gate_up(w_gate, w_up):
    ff = w_gate.shape[-1]
    pad = [(0, 0)] * (w_gate.ndim - 1) + [(0, -ff % LANES)]
    return jnp.concatenate([jnp.pad(w_gate, pad), jnp.pad(w_up, pad)], axis=-1).astype(BF16)


def _moe(x, gain, shift, gate2, rw_t, bias, wgu, wd, swgu, swd, layer, norm_gain=None, tm_e=256, tc_d=256):
    T, D = x.shape
    tm_e = min(tm_e, T)
    tc_d = min(tc_d, T)
    h_tok, wts, idx, rank, cnt = _router(x, gain, shift, rw_t, bias)
    counts = cnt[:, 0].astype(jnp.int32)
    tiles = (counts + tm_e - 1) // tm_e
    tiles_cum = jnp.cumsum(tiles)
    row_start = (tiles_cum - tiles) * tm_e
    n_tiles = (T * TOP_K) // tm_e + N_EXPERTS
    tile_ids = jnp.arange(n_tiles, dtype=jnp.int32)
    tile_expert = jnp.minimum(jnp.sum((tiles_cum[None, :] <= tile_ids[:, None]).astype(jnp.int32), axis=1),
                              N_EXPERTS - 1)
    n_used = tiles_cum[-1:].astype(jnp.int32)
    x_sorted = _dispatch(idx, rank, row_start, row_start + counts, tiles * tm_e - counts, n_used, h_tok,
                         n_tiles, tm_e, tc_d)
    y_sorted = _experts(tile_expert, n_used, x_sorted, wgu, wd, layer, tm_e)
    n_sh = T // tm_e
    y_shared = _experts(jnp.zeros((n_sh,), jnp.int32), jnp.full((1,), n_sh, jnp.int32), h_tok,
                        swgu[:, None], swd[:, None], layer, tm_e)
    return _combine(idx, rank, row_start, wts, y_shared, x, gate2, y_sorted, norm_gain)


def _pad_cols(a, n):
    return jnp.pad(a, ((0, 0), (0, n - a.shape[1])))


def _pad_rows(a, n):
    return jnp.pad(a, ((0, n - a.shape[0]), (0, 0)))


def _pad_rwkv_cols(a, W, w_lora, a_lora):
    c0 = 3 * W
    return jnp.concatenate([a[:, :c0], _pad_cols(a[:, c0:c0 + w_lora], LORA_PAD),
                            _pad_cols(a[:, c0 + w_lora:c0 + w_lora + a_lora], LORA_PAD),
                            a[:, c0 + w_lora + a_lora:]], axis=1)


def kernel(x, c, w_mod, b_mod, norm_mix, norm_ffn, w_in, rwkv_mu, rwkv_w0, rwkv_w_up, rwkv_a0, rwkv_a_up, rwkv_g_up, rwkv_k_k, rwkv_k_a, rwkv_r_k, rwkv_gn_w, rwkv_gn_b, rwkv_v0, rwkv_v_down, rwkv_v_up, hgrn_lower_bounds, hgrn_norm, w_out, router_w, router_bias, expert_w_gate, expert_w_up, expert_w_down, shared_w_gate, shared_w_up, shared_w_down, final_norm):
    B, T, D = x.shape
    L = w_mod.shape[0]
    W = rwkv_w0.shape[1]
    w_lora, a_lora = rwkv_w_up.shape[1], rwkv_a_up.shape[1]
    rwkv_cols = rwkv_mu.shape[1]
    rwkv_cols_p = 3 * W + 2 * LORA_PAD + G_LORA

    lb_soft = jax.nn.softmax(hgrn_lower_bounds.astype(F32), axis=0)
    lbs = jnp.cumsum(lb_soft, axis=0) - lb_soft[0]
    mod = _modulation(c, w_mod, b_mod)

    expert_wgu = _pack_gate_up(expert_w_gate, expert_w_up)
    shared_wgu = _pack_gate_up(shared_w_gate, shared_w_up)
    xs = x.reshape(B * T, D)
    v_first = None
    for l in range(L):
        sh1, sc1, g1, sh2, sc2, g2 = [mod[l, n * D:(n + 1) * D].reshape(1, D) for n in range(6)]
        w_in_p = jnp.concatenate([_pad_rwkv_cols(w_in[l][:, :rwkv_cols], W, w_lora, a_lora),
                                  w_in[l][:, rwkv_cols:]], axis=1).astype(BF16)
        p = _norm_proj(xs, norm_mix[l].reshape(1, D) * (1.0 + sc1), sh1, w_in_p)
        mu_p = _pad_rwkv_cols(rwkv_mu[l].reshape(1, -1), W, w_lora, a_lora)
        vres = None
        if l > 0:
            vres = (rwkv_v0[l - 1].reshape(1, W), _pad_cols(rwkv_v_down[l - 1], V_LORA_PAD).astype(BF16),
                    _pad_rows(rwkv_v_up[l - 1], V_LORA_PAD).astype(BF16), v_first)
        r, lw, k, v, kk, kka, g = _rwkv_prep(
            p, mu_p, rwkv_w0[l].reshape(1, W), _pad_rows(rwkv_w_up[l], LORA_PAD).astype(BF16),
            rwkv_a0[l].reshape(1, W), _pad_rows(rwkv_a_up[l], LORA_PAD).astype(BF16),
            rwkv_g_up[l].astype(BF16), rwkv_k_k[l].reshape(1, W), rwkv_k_a[l].reshape(1, W), vres)
        if l == 0:
            v_first = v
        y_r = _rwkv_recurrence(r, lw, k, v, kk, kka, g, rwkv_r_k[l].reshape(1, W),
                               rwkv_gn_w[l].reshape(1, W), rwkv_gn_b[l].reshape(1, W))
        y_h = _hgrn(p, rwkv_cols_p, lbs[l].reshape(1, -1), hgrn_norm[l].reshape(1, -1))
        xs = _out_proj(y_r, y_h, w_out[l].astype(BF16), xs, g1)
        xs = _moe(xs, norm_ffn[l].reshape(1, D) * (1.0 + sc2), sh2, g2, router_w[l].T,
                  router_bias[l].reshape(-1, 1), expert_wgu, expert_w_down, shared_wgu, shared_w_down, l,
                  final_norm.reshape(1, D) if l == L - 1 else None)
    return xs.reshape(B, T, D)
```

```python
import functools
import math

import jax
import jax.numpy as jnp
from jax import lax
from jax.experimental import pallas as pl
from jax.experimental.pallas import tpu as pltpu

F32 = jnp.float32
BF16 = jnp.bfloat16

LANES = 128
VMEM_LIMIT = 56 * 1024 * 1024

RWKV_HEAD = 64
HGRN_HEAD = 128
CHUNK = 64
SUB = 16
LORA_PAD = 128
G_LORA = 256
V_LORA_PAD = 128
N_EXPERTS = 64
N_GROUPS = 8
TOPK_GROUPS = 4
TOP_K = 8
ROUTE_SCALE = 2.5
MASKED_SCORE = -1e4
MIN_FORGET = 1e-30
NORM_EPS = 1e-6
GN_EPS = 64e-5
DECAY_SCALE = 0.6065306597126334

NN = (((1,), (0,)), ((), ()))
NT = (((1,), (1,)), ((), ()))
TN = (((0,), (0,)), ((), ()))


def _mm(a, b, dims=NN):
    return lax.dot_general(a.astype(BF16), b.astype(BF16), dims, preferred_element_type=F32)


def _mmx(a, b, dims=NN):
    return lax.dot_general(a.astype(F32), b.astype(F32), dims, preferred_element_type=F32,
                           precision=lax.Precision.HIGHEST)


def _split3(x):
    hi = x.astype(BF16)
    r1 = x - hi.astype(F32)
    mid = r1.astype(BF16)
    lo = (r1 - mid.astype(F32)).astype(BF16)
    return hi, mid, lo


def _mm_left01(a01, b, dims=NN):
    a = a01.astype(BF16)
    d = functools.partial(lax.dot_general, dimension_numbers=dims, preferred_element_type=F32)
    hi, mid, lo = _split3(b)
    return d(a, hi) + d(a, mid) + d(a, lo)


def _mm_right01(a, b01, dims=NN):
    b = b01.astype(BF16)
    d = functools.partial(lax.dot_general, dimension_numbers=dims, preferred_element_type=F32)
    hi, mid, lo = _split3(a)
    return d(hi, b) + d(mid, b) + d(lo, b)


def _sigmoid(x):
    return 1.0 / (1.0 + jnp.exp(-x))


def _silu(x):
    return x * _sigmoid(x)


def _cparams(sem):
    return pltpu.CompilerParams(dimension_semantics=sem, vmem_limit_bytes=VMEM_LIMIT)


def _mod_kernel(c_ref, w_ref, b_ref, o_ref):
    cond = _silu(c_ref[...])
    tn = w_ref.shape[2]
    cols = [jnp.sum(w_ref[0, :, j * LANES:(j + 1) * LANES] * cond, axis=0, keepdims=True)
            for j in range(tn // LANES)]
    o_ref[0] = jnp.concatenate(cols, axis=1) + b_ref[0]


def _modulation(c, w_mod, b_mod):
    L, D, N = w_mod.shape
    tn = 512
    c_lanes = jnp.broadcast_to(c.reshape(D, 1), (D, LANES))
    out = pl.pallas_call(
        _mod_kernel,
        grid=(L, N // tn),
        in_specs=[pl.BlockSpec((D, LANES), lambda l, j: (0, 0)),
                  pl.BlockSpec((1, D, tn), lambda l, j: (l, 0, j)),
                  pl.BlockSpec((1, 1, tn), lambda l, j: (l, 0, j))],
        out_specs=pl.BlockSpec((1, 1, tn), lambda l, j: (l, 0, j)),
        out_shape=jax.ShapeDtypeStruct((L, 1, N), F32),
        compiler_params=_cparams(("arbitrary", "arbitrary")),
        name="modulation",
    )(c_lanes, w_mod, b_mod.reshape(L, 1, N))
    return out[:, 0, :]


def _normmod(x, gain, shift):
    y = x * lax.rsqrt(jnp.mean(x * x, axis=-1, keepdims=True) + NORM_EPS)
    return y * gain + shift


def _norm_proj_kernel(x_ref, gain_ref, shift_ref, w_ref, o_ref, h_ref):
    @pl.when(pl.program_id(1) == 0)
    def _():
        h_ref[...] = _normmod(x_ref[...], gain_ref[...], shift_ref[...]).astype(BF16)

    o_ref[...] = lax.dot_general(h_ref[...], w_ref[...], NT, preferred_element_type=F32)


def _norm_proj(x, gain, shift, w_t, tm=512, tn=512):
    T, D = x.shape
    N = w_t.shape[0]
    return pl.pallas_call(
        _norm_proj_kernel,
        grid=(T // tm, N // tn),
        in_specs=[pl.BlockSpec((tm, D), lambda i, j: (i, 0)),
                  pl.BlockSpec((1, D), lambda i, j: (0, 0)),
                  pl.BlockSpec((1, D), lambda i, j: (0, 0)),
                  pl.BlockSpec((tn, D), lambda i, j: (j, 0))],
        out_specs=pl.BlockSpec((tm, tn), lambda i, j: (i, j)),
        out_shape=jax.ShapeDtypeStruct((T, N), F32),
        scratch_shapes=[pltpu.VMEM((tm, D), BF16)],
        compiler_params=_cparams(("arbitrary", "arbitrary")),
        name="norm_proj",
    )(x, gain, shift, w_t)


def _pair_ones():
    r = lax.broadcasted_iota(jnp.int32, (LANES, LANES), 0) // RWKV_HEAD
    c = lax.broadcasted_iota(jnp.int32, (LANES, LANES), 1) // RWKV_HEAD
    return jnp.where(r == c, 1.0, 0.0).astype(F32)


def _rwkv_prep_kernel(has_vres, *refs):
    n_in = 20 if has_vres else 16
    (r_ref, k_ref, v_ref, lo_ref, rp_ref, kp_ref, vp_ref, lop_ref, mu_ref, w0_ref, wup_ref,
     a0_ref, aup_ref, gup_ref, kk_ref, ka_ref) = refs[:16]
    if has_vres:
        v0_ref, vdn_ref, vup_ref, vf_ref = refs[16:20]
    ro_ref, lwo_ref, ko_ref, vo_ref, kko_ref, kkao_ref, go_ref = refs[n_in:]
    first = pl.program_id(0) == 0
    W = r_ref.shape[1]

    def shift(cur_ref, prev_ref, mu):
        cur = cur_ref[...]
        prev_last = jnp.where(first, 0.0, prev_ref[7:8, :])
        rolled = pltpu.roll(cur, 1, axis=0)
        row = lax.broadcasted_iota(jnp.int32, cur.shape, 0)
        prev = jnp.where(row == 0, prev_last, rolled)
        return cur + (prev - cur) * mu

    r = shift(r_ref, rp_ref, mu_ref[:, 0:W])
    k = shift(k_ref, kp_ref, mu_ref[:, W:2 * W])
    v = shift(v_ref, vp_ref, mu_ref[:, 2 * W:3 * W])
    lo = shift(lo_ref, lop_ref, mu_ref[:, 3 * W:])
    w_lo = lo[:, 0:LORA_PAD]
    a_lo = lo[:, LORA_PAD:2 * LORA_PAD]
    g_lo = lo[:, 2 * LORA_PAD:]

    z = w0_ref[...] + _mm(jnp.tanh(w_lo), wup_ref[...])
    lw = -DECAY_SCALE * _sigmoid(z)
    a = _sigmoid(a0_ref[...] + _mm(a_lo, aup_ref[...]))
    g = _mm(_sigmoid(g_lo), gup_ref[...])
    if has_vres:
        mix = _sigmoid(v0_ref[...] + _mm(_mm(v, vdn_ref[...]), vup_ref[...]))
        v = v + (vf_ref[...] - v) * mix

    kkr = k * kk_ref[...]
    sq = kkr * kkr
    ones = _pair_ones()
    ss = jnp.concatenate(
        [_mm_right01(sq[:, j * LANES:(j + 1) * LANES], ones) for j in range(W // LANES)], axis=1)
    kk = kkr / jnp.maximum(jnp.sqrt(ss), 1e-12)
    k2 = k * (1.0 + (a - 1.0) * ka_ref[...])

    ro_ref[...] = r
    lwo_ref[...] = lw
    ko_ref[...] = k2
    vo_ref[...] = v
    kko_ref[...] = kk
    kkao_ref[...] = kk * a
    go_ref[...] = g


def _rwkv_prep(p, mu_p, w0, wup, a0, aup, gup, k_k, k_a, vres, tp=128):
    T = p.shape[0]
    W = w0.shape[1]
    lo_w = 2 * LORA_PAD + G_LORA
    nb = tp // 8
    lo_blk = 3 * W // lo_w

    def cur(cb, width):
        return pl.BlockSpec((tp, width), lambda i, cb=cb: (i, cb))

    def prev(cb, width):
        return pl.BlockSpec((8, width), lambda i, cb=cb: (jnp.maximum(i * nb - 1, 0), cb))

    def vec(n):
        return pl.BlockSpec((1, n), lambda i: (0, 0))

    def mat(a, b):
        return pl.BlockSpec((a, b), lambda i: (0, 0))

    in_specs = [cur(0, W), cur(1, W), cur(2, W), cur(lo_blk, lo_w),
                prev(0, W), prev(1, W), prev(2, W), prev(lo_blk, lo_w),
                vec(3 * W + lo_w), vec(W), mat(LORA_PAD, W), vec(W), mat(LORA_PAD, W),
                mat(G_LORA, W), vec(W), vec(W)]
    args = [p, p, p, p, p, p, p, p, mu_p, w0, wup, a0, aup, gup, k_k, k_a]
    if vres is not None:
        v0, vdn, vup, v_first = vres
        in_specs += [vec(W), mat(W, V_LORA_PAD), mat(V_LORA_PAD, W),
                     pl.BlockSpec((tp, W), lambda i: (i, 0))]
        args += [v0, vdn, vup, v_first]
    out_spec = pl.BlockSpec((tp, W), lambda i: (i, 0))
    sds = jax.ShapeDtypeStruct((T, W), F32)
    return pl.pallas_call(
        functools.partial(_rwkv_prep_kernel, vres is not None),
        grid=(T // tp,),
        in_specs=in_specs,
        out_specs=[out_spec] * 7,
        out_shape=[sds] * 7,
        compiler_params=_cparams(("arbitrary",)),
        name="rwkv_prep",
    )(*args)


def _tri_incl(n):
    row = lax.broadcasted_iota(jnp.int32, (n, n), 0)
    col = lax.broadcasted_iota(jnp.int32, (n, n), 1)
    return jnp.where(col <= row, 1.0, 0.0).astype(BF16)


def _rwkv_pair(r, c, lw, k, v, kk, kka, g, rk, gnw, gnb, st):
    C = r.shape[0]
    C2 = 2 * C
    cp = c - lw
    c_last = c[C - 1:C, :]
    e_c, e_cp, e_nc, e_dl = jnp.exp(c), jnp.exp(cp), jnp.exp(-c), jnp.exp(c_last - c)

    lane = lax.broadcasted_iota(jnp.int32, (C, LANES), 1)
    m0 = lane < RWKV_HEAD

    def stack(x):
        return jnp.concatenate([jnp.where(m0, x, 0.0), jnp.where(m0, 0.0, x)], axis=0)

    aq_rt = jnp.concatenate([stack(kk * e_cp), stack(r * e_c)], axis=0).astype(BF16)
    bt_kt = jnp.concatenate([stack(kka * e_nc), stack(k * e_nc)], axis=0).astype(BF16)
    kw_bw = jnp.concatenate([stack(k * e_dl), stack(kka * e_dl)], axis=0).astype(BF16)
    vs = stack(v)
    vs_b = vs.astype(BF16)

    row = lax.broadcasted_iota(jnp.int32, (C2, C2), 0)
    col = lax.broadcasted_iota(jnp.int32, (C2, C2), 1)
    t_i, s_j = row % C, col % C
    strict, incl, eye = s_j < t_i, s_j <= t_i, row == col

    yield
    p = _mm(aq_rt, bt_kt, NT)
    yield
    l_ab = jnp.where(strict, p[:C2, :C2], 0.0)
    l_ak = jnp.where(strict, p[:C2, C2:], 0.0)
    l_rb = jnp.where(incl, p[C2:, :C2], 0.0)
    l_rk = jnp.where(incl, p[C2:, C2:], 0.0)

    pw = -l_ab
    inv = jnp.where(eye, 1.0, 0.0) + pw
    n = 2
    while n < C:
        pw = _mm(pw, pw)
        yield
        inv = inv + _mm(inv, pw)
        n *= 2

    xy = _mm(aq_rt, st) + _mm(jnp.concatenate([l_ak, l_rk], axis=0), vs_b)
    yield
    u = _mm(inv, xy[:C2])
    yield
    y = xy[C2:] - _mm(l_rb, u)
    w_col = jnp.broadcast_to(jnp.exp(c_last), (C2, C2)).T
    st_new = st * w_col + _mm(kw_bw, jnp.concatenate([vs, -u], axis=0), TN)
    yield

    mh = (row // C) == (col // RWKV_HEAD)
    inv_n = 1.0 / RWKV_HEAD
    mean = jnp.sum(y, axis=1, keepdims=True) * inv_n
    yc = jnp.where(mh, y - mean, 0.0)
    var = jnp.sum(yc * yc, axis=1, keepdims=True) * inv_n
    yn = yc * lax.rsqrt(var + GN_EPS)
    bonus = jnp.sum(stack(r * k * rk), axis=1, keepdims=True) * vs
    yn_t = yn[:C] + yn[C:]
    bonus_t = bonus[:C] + bonus[C:]
    return (yn_t * gnw + gnb + bonus_t) * g, st_new


def _interleave(gens):
    results = [None] * len(gens)
    live = list(enumerate(gens))
    while live:
        still = []
        for j, gen in live:
            try:
                next(gen)
                still.append((j, gen))
            except StopIteration as stop:
                results[j] = stop.value
        live = still
    return results


def _rwkv_rec_kernel(r_ref, lw_ref, k_ref, v_ref, kk_ref, kka_ref, g_ref, rk_ref, gnw_ref, gnb_ref,
                     o_ref, st_ref):
    @pl.when(pl.program_id(1) == 0)
    def _():
        st_ref[...] = jnp.zeros_like(st_ref)

    rows, lanes = r_ref.shape
    tri = _tri_incl(CHUNK)

    def chunk(ci, carry):
        rs = pl.ds(pl.multiple_of(ci * CHUNK, CHUNK), CHUNK)
        lw_all = lw_ref[rs, :]
        c_all = _mm_left01(tri, lw_all)
        r, k, v, kk, kka, g = (ref[rs, :] for ref in (r_ref, k_ref, v_ref, kk_ref, kka_ref, g_ref))
        rk, gnw, gnb = rk_ref[...], gnw_ref[...], gnb_ref[...]
        gens = []
        for j in range(lanes // LANES):
            sl = slice(j * LANES, (j + 1) * LANES)
            gens.append(_rwkv_pair(r[:, sl], c_all[:, sl], lw_all[:, sl], k[:, sl], v[:, sl], kk[:, sl],
                                   kka[:, sl], g[:, sl], rk[:, sl], gnw[:, sl], gnb[:, sl], st_ref[j]))
        results = _interleave(gens)
        for j, (_, st_new) in enumerate(results):
            st_ref[j] = st_new
        o_ref[rs, :] = jnp.concatenate([out for out, _ in results], axis=1).astype(o_ref.dtype)
        return carry

    lax.fori_loop(0, rows // CHUNK, chunk, 0)


def _rwkv_recurrence(r, lw, k, v, kk, kka, g, rk, gnw, gnb, hb_lanes=1024, rows=512):
    T, W = r.shape
    rows = min(rows, T)
    seq = pl.BlockSpec((rows, hb_lanes), lambda h, t: (t, h))
    vec = pl.BlockSpec((1, hb_lanes), lambda h, t: (0, h))
    return pl.pallas_call(
        _rwkv_rec_kernel,
        grid=(W // hb_lanes, T // rows),
        in_specs=[seq] * 7 + [vec] * 3,
        out_specs=seq,
        out_shape=jax.ShapeDtypeStruct((T, W), BF16),
        scratch_shapes=[pltpu.VMEM((hb_lanes // LANES, LANES, LANES), F32)],
        compiler_params=_cparams(("arbitrary", "arbitrary")),
        name="rwkv_recurrence",
    )(r, lw, k, v, kk, kka, g, rk, gnw, gnb)


def _hgrn_head(q_raw, f_raw, i_c, g_raw, lb, gain, tri, s):
    C = q_raw.shape[0]
    q = _silu(q_raw)
    forget = lb + (1.0 - lb) * _sigmoid(f_raw)
    lf = jnp.log(jnp.maximum(forget, MIN_FORGET))
    kin = (1.0 - lb) * _sigmoid(-f_raw)
    b = _mm_left01(tri, lf)
    yield
    o = _mm(q * jnp.exp(b), s)

    lane_s = lax.broadcasted_iota(jnp.int32, (SUB, C), 1)
    row_s = lax.broadcasted_iota(jnp.int32, (SUB, C), 0)
    score_rows = []
    for i in range(C // SUB):
        lo, hi = i * SUB, (i + 1) * SUB
        b_i, q_i = b[lo:hi], q[lo:hi]
        blk = jnp.zeros((SUB, C), F32)
        for sidx in range(SUB):
            srow = lo + sidx
            d = jnp.exp(b_i - b[srow:srow + 1]) * q_i * kin[srow:srow + 1]
            blk = jnp.where(lane_s == srow, jnp.sum(d, axis=1, keepdims=True), blk)
        blk = jnp.where(lane_s <= row_s + lo, blk, 0.0)
        yield
        if i > 0:
            b_st = b[lo - 1:lo]
            qs = q_i * jnp.exp(b_i - b_st)
            ks = kin * jnp.exp(jnp.minimum(b_st - b, 0.0))
            blk = blk + jnp.where(lane_s < lo, _mm(qs, ks, NT), 0.0)
        score_rows.append(blk)
    scores = jnp.concatenate(score_rows, axis=0)
    o = o + _mm(scores, i_c)

    b_last = b[C - 1:C]
    f_col = jnp.broadcast_to(jnp.exp(b_last), (LANES, LANES)).T
    s_new = s * f_col + _mm(kin * jnp.exp(b_last - b), i_c, TN)

    o = o * lax.rsqrt(jnp.mean(o * o, axis=1, keepdims=True) + NORM_EPS)
    return o * gain * _sigmoid(g_raw), s_new


def _hgrn_kernel(q_ref, f_ref, i_ref, g_ref, lb_ref, gain_ref, o_ref, s_ref):
    @pl.when(pl.program_id(1) == 0)
    def _():
        s_ref[...] = jnp.zeros_like(s_ref)

    rows, lanes = q_ref.shape
    tri = _tri_incl(CHUNK)

    def chunk(ci, carry):
        rs = pl.ds(pl.multiple_of(ci * CHUNK, CHUNK), CHUNK)
        q, f, i_c, g = (ref[rs, :] for ref in (q_ref, f_ref, i_ref, g_ref))
        lb, gain = lb_ref[...], gain_ref[...]
        gens = []
        for j in range(lanes // LANES):
            sl = slice(j * LANES, (j + 1) * LANES)
            gens.append(_hgrn_head(q[:, sl], f[:, sl], i_c[:, sl], g[:, sl], lb[:, sl], gain[:, sl],
                                   tri, s_ref[j]))
        results = _interleave(gens)
        for j, (_, s_new) in enumerate(results):
            s_ref[j] = s_new
        o_ref[rs, :] = jnp.concatenate([out for out, _ in results], axis=1).astype(o_ref.dtype)
        return carry

    lax.fori_loop(0, rows // CHUNK, chunk, 0)


def _hgrn(p, col0, lb, gain, hb_lanes=512, rows=512):
    T = p.shape[0]
    W = lb.shape[1]
    rows = min(rows, T)
    nb = W // hb_lanes
    b0 = col0 // hb_lanes

    def sec(n):
        return pl.BlockSpec((rows, hb_lanes), lambda h, t, n=n: (t, b0 + n * nb + h))

    vec = pl.BlockSpec((1, hb_lanes), lambda h, t: (0, h))
    return pl.pallas_call(
        _hgrn_kernel,
        grid=(nb, T // rows),
        in_specs=[sec(0), sec(1), sec(2), sec(3), vec, vec],
        out_specs=pl.BlockSpec((rows, hb_lanes), lambda h, t: (t, h)),
        out_shape=jax.ShapeDtypeStruct((T, W), BF16),
        scratch_shapes=[pltpu.VMEM((hb_lanes // LANES, LANES, LANES), F32)],
        compiler_params=_cparams(("arbitrary", "arbitrary")),
        name="hgrn2",
    )(p, p, p, p, lb, gain)


def _out_proj_kernel(ya_ref, yb_ref, wa_ref, wb_ref, x_ref, gate_ref, o_ref):
    acc = jnp.dot(ya_ref[...], wa_ref[...], preferred_element_type=F32)
    acc = acc + jnp.dot(yb_ref[...], wb_ref[...], preferred_element_type=F32)
    o_ref[...] = x_ref[...] + gate_ref[...] * acc


def _out_proj(ya, yb, w, x, gate, tm=512, tn=1024):
    T, Ka = ya.shape
    Kb = yb.shape[1]
    D = w.shape[1]
    return pl.pallas_call(
        _out_proj_kernel,
        grid=(T // tm, D // tn),
        in_specs=[pl.BlockSpec((tm, Ka), lambda i, j: (i, 0)),
                  pl.BlockSpec((tm, Kb), lambda i, j: (i, 0)),
                  pl.BlockSpec((Ka, tn), lambda i, j: (0, j)),
                  pl.BlockSpec((Kb, tn), lambda i, j: (Ka // Kb, j)),
                  pl.BlockSpec((tm, tn), lambda i, j: (i, j)),
                  pl.BlockSpec((1, tn), lambda i, j: (0, j))],
        out_specs=pl.BlockSpec((tm, tn), lambda i, j: (i, j)),
        out_shape=jax.ShapeDtypeStruct((T, D), F32),
        compiler_params=_cparams(("arbitrary", "arbitrary")),
        name="out_proj",
    )(ya, yb, w, w, x, gate)


def _first_max_onehot(work, axis, size):
    m = jnp.max(work, axis=axis, keepdims=True)
    idx = lax.broadcasted_iota(jnp.int32, work.shape, axis)
    first = jnp.min(jnp.where(work == m, idx, size), axis=axis, keepdims=True)
    return idx == first, m


def _pitch(s_per):
    return s_per + 8 if s_per % 16 == 0 else s_per


def _store_token_major(ref, val):
    n, d = val.shape
    pitch = ref.shape[0] // n
    for s in range(d // LANES):
        ref[pl.ds(s, n, stride=pitch), :] = val[:, s * LANES:(s + 1) * LANES]
    for s in range(d // LANES, pitch):
        ref[pl.ds(s, n, stride=pitch), :] = jnp.zeros((n, LANES), F32)


def _router_kernel(x_ref, gain_ref, shift_ref, rwt_ref, bias_ref,
                   h_ref, wts_ref, idx_ref, rank_ref, cnt_ref, carry_ref):
    @pl.when(pl.program_id(0) == 0)
    def _():
        carry_ref[...] = jnp.zeros_like(carry_ref)

    h = _normmod(x_ref[...], gain_ref[...], shift_ref[...])
    _store_token_major(h_ref, h)
    tm = h.shape[0]
    per_group = N_EXPERTS // N_GROUPS
    logits = _mmx(rwt_ref[...], h, NT)
    scores = _sigmoid(logits)
    biased = scores + bias_ref[...]

    b3 = biased.reshape(N_GROUPS, per_group, tm)
    pick1, m1 = _first_max_onehot(b3, 1, per_group)
    m2 = jnp.max(jnp.where(pick1, -jnp.inf, b3), axis=1, keepdims=True)
    gscore = (m1 + m2).reshape(N_GROUPS, tm)

    work = gscore
    gsel = jnp.zeros_like(gscore)
    for _ in range(TOPK_GROUPS):
        pick, _m = _first_max_onehot(work, 0, N_GROUPS)
        gsel = jnp.where(pick, 1.0, gsel)
        work = jnp.where(pick, -jnp.inf, work)
    ok = jnp.broadcast_to(gsel.reshape(N_GROUPS, 1, tm), (N_GROUPS, per_group, tm)).reshape(N_EXPERTS, tm)
    work = jnp.where(ok > 0.5, biased, MASKED_SCORE)
    picks = []
    for _ in range(TOP_K):
        pick, _m = _first_max_onehot(work, 0, N_EXPERTS)
        picks.append(pick)
        work = jnp.where(pick, -jnp.inf, work)
    picked = functools.reduce(lambda a, b: a + b, [jnp.where(p, 1.0, 0.0) for p in picks])
    sel = picked * scores
    gates_t = sel / jnp.sum(sel, axis=0, keepdims=True) * ROUTE_SCALE

    ra = lax.broadcasted_iota(jnp.int32, (tm, tm), 0)
    rb = lax.broadcasted_iota(jnp.int32, (tm, tm), 1)
    before = jnp.where(ra < rb, 1.0, 0.0).astype(BF16)
    carry = carry_ref[:, 0:1]
    rank_full = _mm(picked, before) + carry
    e_iota = lax.broadcasted_iota(jnp.int32, (N_EXPERTS, tm), 0).astype(F32)

    def slot_rows(table):
        rows = [jnp.sum(jnp.where(p, table, 0.0), axis=0, keepdims=True) for p in picks]
        return jnp.concatenate(rows, axis=0)

    idx_ref[...] = slot_rows(e_iota).astype(jnp.int32)
    rank_ref[...] = slot_rows(rank_full).astype(jnp.int32)
    wts_ref[...] = slot_rows(gates_t)
    new_carry = carry + jnp.sum(picked, axis=1, keepdims=True)
    carry_ref[...] = jnp.broadcast_to(new_carry, carry_ref.shape)
    cnt_ref[...] = jnp.broadcast_to(new_carry, cnt_ref.shape)


def _router(x, gain, shift, rw_t, bias, tm=256):
    T, D = x.shape
    tm = min(tm, T)
    P = _pitch(D // LANES)
    return pl.pallas_call(
        _router_kernel,
        grid=(T // tm,),
        in_specs=[pl.BlockSpec((tm, D), lambda i: (i, 0)),
                  pl.BlockSpec((1, D), lambda i: (0, 0)),
                  pl.BlockSpec((1, D), lambda i: (0, 0)),
                  pl.BlockSpec((N_EXPERTS, D), lambda i: (0, 0)),
                  pl.BlockSpec((N_EXPERTS, 1), lambda i: (0, 0))],
        out_specs=[pl.BlockSpec((tm * P, LANES), lambda i: (i, 0)),
                   pl.BlockSpec((TOP_K, tm), lambda i: (0, i)),
                   pl.BlockSpec((TOP_K, tm), lambda i: (0, i)),
                   pl.BlockSpec((TOP_K, tm), lambda i: (0, i)),
                   pl.BlockSpec((N_EXPERTS, LANES), lambda i: (0, 0))],
        out_shape=[jax.ShapeDtypeStruct((T * P, LANES), F32),
                   jax.ShapeDtypeStruct((TOP_K, T), F32),
                   jax.ShapeDtypeStruct((TOP_K, T), jnp.int32),
                   jax.ShapeDtypeStruct((TOP_K, T), jnp.int32),
                   jax.ShapeDtypeStruct((N_EXPERTS, LANES), F32)],
        scratch_shapes=[pltpu.VMEM((N_EXPERTS, LANES), F32)],
        compiler_params=_cparams(("arbitrary",)),
        name="router",
    )(x, gain, shift, rw_t, bias)


def _dispatch_kernel(P, idx_ref, rank_ref, row_start_ref, pad_start_ref, pad_len_ref, nu_ref, h_ref, xs_hbm,
                     zero_ref, sem, pad_sem):
    i = pl.program_id(0)
    tc = idx_ref.shape[1]
    tile_rows = zero_ref.shape[0]

    def slab(ref, row):
        return ref.at[pl.ds(pl.multiple_of(row * P, math.gcd(P, 8)), P)]

    def token(n, carry):
        src = slab(h_ref, n)
        for j in range(TOP_K):
            dest = row_start_ref[idx_ref[j, n]] + rank_ref[j, n]
            pltpu.make_async_copy(src, slab(xs_hbm, dest), sem).start()
        return carry

    lax.fori_loop(0, tc, token, 0)

    @pl.when(i == 0)
    def _():
        zero_ref[...] = jnp.zeros_like(zero_ref)
        zero_slab = zero_ref.at[pl.ds(0, P)]

        def expert(e, carry):
            def fill(r, c):
                pltpu.make_async_copy(zero_slab, slab(xs_hbm, pad_start_ref[e] + r), pad_sem).start()
                return c

            def drain(r, c):
                pltpu.make_async_copy(zero_slab, slab(xs_hbm, 0), pad_sem).wait()
                return c

            lax.fori_loop(0, pad_len_ref[e], fill, 0)
            lax.fori_loop(0, pad_len_ref[e], drain, 0)
            return carry

        lax.fori_loop(0, N_EXPERTS, expert, 0)

        def tile_dst(t):
            return xs_hbm.at[pl.ds(pl.multiple_of(t * tile_rows, 8), tile_rows)]

        def fill_tile(t, c):
            pltpu.make_async_copy(zero_ref, tile_dst(t), pad_sem).start()
            return c

        def drain_tile(t, c):
            pltpu.make_async_copy(zero_ref, tile_dst(0), pad_sem).wait()
            return c

        n_tiles = xs_hbm.shape[0] // tile_rows
        lax.fori_loop(nu_ref[0], n_tiles, fill_tile, 0)
        lax.fori_loop(nu_ref[0], n_tiles, drain_tile, 0)

    for j in range(TOP_K):
        pltpu.make_async_copy(h_ref, xs_hbm.at[pl.ds(0, tc * P)], sem).wait()


def _dispatch(idx, rank, row_start, pad_start, pad_len, n_used, h_tok, n_tiles, tm_e, tc):
    T = idx.shape[1]
    P = h_tok.shape[0] // T
    smem = pl.BlockSpec(memory_space=pltpu.SMEM)
    slots = pl.BlockSpec((TOP_K, tc), lambda i: (0, i), memory_space=pltpu.SMEM)
    return pl.pallas_call(
        functools.partial(_dispatch_kernel, P),
        grid=(T // tc,),
        in_specs=[slots, slots, smem, smem, smem, smem,
                  pl.BlockSpec((tc * P, LANES), lambda i: (i, 0))],
        out_specs=pl.BlockSpec(memory_space=pl.ANY),
        out_shape=jax.ShapeDtypeStruct((n_tiles * tm_e * P, LANES), F32),
        scratch_shapes=[pltpu.VMEM((tm_e * P, LANES), F32), pltpu.SemaphoreType.DMA,
                        pltpu.SemaphoreType.DMA],
        compiler_params=_cparams(("arbitrary",)),
        name="dispatch",
    )(idx, rank, row_start, pad_start, pad_len, n_used, h_tok)


def _experts_kernel(te_ref, nu_ref, x_ref, wg_ref, wu_ref, wd_ref, o_ref):
    S = wg_ref.shape[3] // LANES
    P = _pitch(S)
    tm = x_ref.shape[0] // P
    kc = 8 if S % 8 == 0 else S

    @pl.when(pl.program_id(0) >= nu_ref[0])
    def _():
        o_ref[...] = jnp.zeros_like(o_ref)

    @pl.when(pl.program_id(0) < nu_ref[0])
    def _():
        g = u = None
        for c0 in range(0, S, kc):
            xk = jnp.concatenate([x_ref[pl.ds(s, tm, stride=P), :] for s in range(c0, c0 + kc)],
                                 axis=1).astype(BF16)
            ks = slice(c0 * LANES, (c0 + kc) * LANES)
            gk = lax.dot_general(xk, wg_ref[0, 0, :, ks].astype(BF16), NT, preferred_element_type=F32)
            uk = lax.dot_general(xk, wu_ref[0, 0, :, ks].astype(BF16), NT, preferred_element_type=F32)
            g = gk if g is None else g + gk
            u = uk if u is None else u + uk
        act = (_silu(g) * u).astype(BF16)
        for c0 in range(0, S, kc):
            ns = slice(c0 * LANES, (c0 + kc) * LANES)
            y = jnp.dot(act, wd_ref[0, 0, :, ns].astype(BF16), preferred_element_type=F32)
            for s in range(kc):
                o_ref[pl.ds(c0 + s, tm, stride=P), :] = y[:, s * LANES:(s + 1) * LANES]
        for s in range(S, P):
            o_ref[pl.ds(s, tm, stride=P), :] = jnp.zeros((tm, LANES), F32)


def _experts(tile_expert, n_used, x_tok, wg, wu, wd, layer, tm):
    _, E, ff, D = wg.shape
    P = _pitch(D // LANES)
    n_tiles = x_tok.shape[0] // (tm * P)

    def row_map(i, te, nu):
        return (jnp.minimum(i, nu[0] - 1), 0)

    def w_map(i, te, nu):
        return (layer, te[i], 0, 0)

    grid_spec = pltpu.PrefetchScalarGridSpec(
        num_scalar_prefetch=2,
        grid=(n_tiles,),
        in_specs=[pl.BlockSpec((tm * P, LANES), row_map),
                  pl.BlockSpec((1, 1, ff, D), w_map),
                  pl.BlockSpec((1, 1, ff, D), w_map),
                  pl.BlockSpec((1, 1, ff, D), w_map)],
        out_specs=pl.BlockSpec((tm * P, LANES), lambda i, te, nu: (i, 0)),
    )
    return pl.pallas_call(
        _experts_kernel,
        grid_spec=grid_spec,
        out_shape=jax.ShapeDtypeStruct(x_tok.shape, F32),
        compiler_params=_cparams(("arbitrary",)),
        name="experts",
    )(tile_expert, n_used, x_tok, wg, wu, wd)


def _combine_kernel(has_norm, idx_ref, rank_ref, row_start_ref, w_ref, ysh_ref, x_ref, gate2_ref, *rest):
    if has_norm:
        norm_gain_ref, y_hbm, o_ref, buf_ref, acc_ref, sem = rest
    else:
        y_hbm, o_ref, buf_ref, acc_ref, sem = rest
    tc, D = x_ref.shape
    S = D // LANES
    P = _pitch(S)
    align = math.gcd(P, 8)
    slot_rows = tc * P

    def token(n, carry):
        for j in range(TOP_K):
            dest = row_start_ref[idx_ref[j, n]] + rank_ref[j, n]
            src = y_hbm.at[pl.ds(pl.multiple_of(dest * P, align), S)]
            dst = buf_ref.at[pl.ds(pl.multiple_of(j * slot_rows + n * P, align), S)]
            pltpu.make_async_copy(src, dst, sem).start()
        return carry

    lax.fori_loop(0, tc, token, 0)
    n_rows = TOP_K * tc * S
    pltpu.make_async_copy(y_hbm.at[pl.ds(0, n_rows)], buf_ref.at[pl.ds(0, n_rows)], sem).wait()

    def accumulate(n, carry):
        row = pl.multiple_of(n * P, align)
        acc = ysh_ref[pl.ds(row, S), :]
        for j in range(TOP_K):
            slab = buf_ref[pl.ds(pl.multiple_of(j * slot_rows + n * P, align), S), :]
            acc = acc + w_ref[j, n] * slab
        acc_ref[pl.ds(row, S), :] = acc
        return carry

    lax.fori_loop(0, tc, accumulate, 0, unroll=2)
    for s in range(S):
        ls = slice(s * LANES, (s + 1) * LANES)
        o_ref[:, ls] = x_ref[:, ls] + gate2_ref[:, ls] * acc_ref[pl.ds(s, tc, stride=P), :]
    if has_norm:
        xo = o_ref[...]
        o_ref[...] = xo * lax.rsqrt(jnp.mean(xo * xo, axis=-1, keepdims=True) + NORM_EPS) * norm_gain_ref[...]


def _combine(idx, rank, row_start, wts, y_shared, x, gate2, y_sorted, norm_gain=None, tc=128):
    T, D = x.shape
    tc = min(tc, T)
    P = _pitch(D // LANES)
    slots = pl.BlockSpec((TOP_K, tc), lambda i: (0, i), memory_space=pltpu.SMEM)
    vec = pl.BlockSpec((1, D), lambda i: (0, 0))
    in_specs = [slots, slots, pl.BlockSpec(memory_space=pltpu.SMEM), slots,
                pl.BlockSpec((tc * P, LANES), lambda i: (i, 0)),
                pl.BlockSpec((tc, D), lambda i: (i, 0)), vec]
    args = [idx, rank, row_start, wts, y_shared, x, gate2]
    if norm_gain is not None:
        in_specs.append(vec)
        args.append(norm_gain)
    in_specs.append(pl.BlockSpec(memory_space=pl.ANY))
    args.append(y_sorted)
    return pl.pallas_call(
        functools.partial(_combine_kernel, norm_gain is not None),
        grid=(T // tc,),
        in_specs=in_specs,
        out_specs=pl.BlockSpec((tc, D), lambda i: (i, 0)),
        out_shape=jax.ShapeDtypeStruct((T, D), F32),
        scratch_shapes=[pltpu.VMEM((TOP_K * tc * P, LANES), F32), pltpu.VMEM((tc * P, LANES), F32),
                        pltpu.SemaphoreType.DMA],
        compiler_params=_cparams(("arbitrary",)),
        name="combine",
    )(*args)


def _moe(x, gain, shift, gate2, rw_t, bias, wg, wu, wd, swg, swu, swd, layer, norm_gain=None, tm_e=256,
         tc_d=256):
    T, D = x.shape
    tm_e = min(tm_e, T)
    tc_d = min(tc_d, T)
    h_tok, wts, idx, rank, cnt = _router(x, gain, shift, rw_t, bias)
    counts = cnt[:, 0].astype(jnp.int32)
    tiles = (counts + tm_e - 1) // tm_e
    tiles_cum = jnp.cumsum(tiles)
    row_start = (tiles_cum - tiles) * tm_e
    n_tiles = (T * TOP_K) // tm_e + N_EXPERTS
    tile_ids = jnp.arange(n_tiles, dtype=jnp.int32)
    tile_expert = jnp.minimum(jnp.sum((tiles_cum[None, :] <= tile_ids[:, None]).astype(jnp.int32), axis=1),
                              N_EXPERTS - 1)
    n_used = tiles_cum[-1:].astype(jnp.int32)
    x_sorted = _dispatch(idx, rank, row_start, row_start + counts, tiles * tm_e - counts, n_used, h_tok,
                         n_tiles, tm_e, tc_d)
    y_sorted = _experts(tile_expert, n_used, x_sorted, wg, wu, wd, layer, tm_e)
    n_sh = T // tm_e
    y_shared = _experts(jnp.zeros((n_sh,), jnp.int32), jnp.full((1,), n_sh, jnp.int32), h_tok,
                        swg[:, None], swu[:, None], swd[:, None], layer, tm_e)
    return _combine(idx, rank, row_start, wts, y_shared, x, gate2, y_sorted, norm_gain)


def _pad_cols(a, n):
    return jnp.pad(a, ((0, 0), (0, n - a.shape[1])))


def _pad_rows(a, n):
    return jnp.pad(a, ((0, n - a.shape[0]), (0, 0)))


def _pad_rwkv_cols(a, W, w_lora, a_lora):
    c0 = 3 * W
    return jnp.concatenate([a[:, :c0], _pad_cols(a[:, c0:c0 + w_lora], LORA_PAD),
                            _pad_cols(a[:, c0 + w_lora:c0 + w_lora + a_lora], LORA_PAD),
                            a[:, c0 + w_lora + a_lora:]], axis=1)


def kernel(x, c, w_mod, b_mod, norm_mix, norm_ffn, w_in, rwkv_mu, rwkv_w0, rwkv_w_up, rwkv_a0, rwkv_a_up, rwkv_g_up, rwkv_k_k, rwkv_k_a, rwkv_r_k, rwkv_gn_w, rwkv_gn_b, rwkv_v0, rwkv_v_down, rwkv_v_up, hgrn_lower_bounds, hgrn_norm, w_out, router_w, router_bias, expert_w_gate, expert_w_up, expert_w_down, shared_w_gate, shared_w_up, shared_w_down, final_norm):
    B, T, D = x.shape
    L = w_mod.shape[0]
    W = rwkv_w0.shape[1]
    w_lora, a_lora = rwkv_w_up.shape[1], rwkv_a_up.shape[1]
    rwkv_cols = rwkv_mu.shape[1]
    rwkv_cols_p = 3 * W + 2 * LORA_PAD + G_LORA

    lb_soft = jax.nn.softmax(hgrn_lower_bounds.astype(F32), axis=0)
    lbs = jnp.cumsum(lb_soft, axis=0) - lb_soft[0]
    mod = _modulation(c, w_mod, b_mod)

    xs = x.reshape(B * T, D)
    v_first = None
    for l in range(L):
        sh1, sc1, g1, sh2, sc2, g2 = [mod[l, n * D:(n + 1) * D].reshape(1, D) for n in range(6)]
        w_in_p = _pad_rwkv_cols(w_in[l], W, w_lora, a_lora).T.astype(BF16)
        p = _norm_proj(xs, norm_mix[l].reshape(1, D) * (1.0 + sc1), sh1, w_in_p)
        mu_p = _pad_rwkv_cols(rwkv_mu[l].reshape(1, -1), W, w_lora, a_lora)
        vres = None
        if l > 0:
            vres = (rwkv_v0[l - 1].reshape(1, W), _pad_cols(rwkv_v_down[l - 1], V_LORA_PAD).astype(BF16),
                    _pad_rows(rwkv_v_up[l - 1], V_LORA_PAD).astype(BF16), v_first)
        r, lw, k, v, kk, kka, g = _rwkv_prep(
            p, mu_p, rwkv_w0[l].reshape(1, W), _pad_rows(rwkv_w_up[l], LORA_PAD).astype(BF16),
            rwkv_a0[l].reshape(1, W), _pad_rows(rwkv_a_up[l], LORA_PAD).astype(BF16),
            rwkv_g_up[l].astype(BF16), rwkv_k_k[l].reshape(1, W), rwkv_k_a[l].reshape(1, W), vres)
        if l == 0:
            v_first = v
        y_r = _rwkv_recurrence(r, lw, k, v, kk, kka, g, rwkv_r_k[l].reshape(1, W),
                               rwkv_gn_w[l].reshape(1, W), rwkv_gn_b[l].reshape(1, W))
        y_h = _hgrn(p, rwkv_cols_p, lbs[l].reshape(1, -1), hgrn_norm[l].reshape(1, -1))
        xs = _out_proj(y_r, y_h, w_out[l].astype(BF16), xs, g1)
        xs = _moe(xs, norm_ffn[l].reshape(1, D) * (1.0 + sc2), sh2, g2, router_w[l].T,
                  router_bias[l].reshape(-1, 1), jnp.swapaxes(expert_w_gate, 2, 3),
                  jnp.swapaxes(expert_w_up, 2, 3), expert_w_down, jnp.swapaxes(shared_w_gate, 1, 2),
                  jnp.swapaxes(shared_w_up, 1, 2), shared_w_down, l,
                  final_norm.reshape(1, D) if l == L - 1 else None)
    return xs.reshape(B, T, D)
```

```python
import functools
import math

import jax
import jax.numpy as jnp
from jax import lax
from jax.experimental import pallas as pl
from jax.experimental.pallas import tpu as pltpu

F32 = jnp.float32
BF16 = jnp.bfloat16

LANES = 128
VMEM_LIMIT = 56 * 1024 * 1024

RWKV_HEAD = 64
HGRN_HEAD = 128
CHUNK = 64
SUB = 16
LORA_PAD = 128
G_LORA = 256
V_LORA_PAD = 128
N_EXPERTS = 64
N_GROUPS = 8
TOPK_GROUPS = 4
TOP_K = 8
ROUTE_SCALE = 2.5
MASKED_SCORE = -1e4
MIN_FORGET = 1e-30
NORM_EPS = 1e-6
GN_EPS = 64e-5
DECAY_SCALE = 0.6065306597126334

NN = (((1,), (0,)), ((), ()))
NT = (((1,), (1,)), ((), ()))
TN = (((0,), (0,)), ((), ()))


def _mm(a, b, dims=NN):
    return lax.dot_general(a.astype(BF16), b.astype(BF16), dims, preferred_element_type=F32)


def _mmx(a, b, dims=NN):
    return lax.dot_general(a.astype(F32), b.astype(F32), dims, preferred_element_type=F32,
                           precision=lax.Precision.HIGHEST)


def _split3(x):
    hi = x.astype(BF16)
    r1 = x - hi.astype(F32)
    mid = r1.astype(BF16)
    lo = (r1 - mid.astype(F32)).astype(BF16)
    return hi, mid, lo


def _mm_left01(a01, b, dims=NN):
    a = a01.astype(BF16)
    d = functools.partial(lax.dot_general, dimension_numbers=dims, preferred_element_type=F32)
    hi, mid, lo = _split3(b)
    return d(a, hi) + d(a, mid) + d(a, lo)


def _mm_right01(a, b01, dims=NN):
    b = b01.astype(BF16)
    d = functools.partial(lax.dot_general, dimension_numbers=dims, preferred_element_type=F32)
    hi, mid, lo = _split3(a)
    return d(hi, b) + d(mid, b) + d(lo, b)


def _sigmoid(x):
    return 1.0 / (1.0 + jnp.exp(-x))


def _silu(x):
    return x * _sigmoid(x)


def _cparams(sem):
    return pltpu.CompilerParams(dimension_semantics=sem, vmem_limit_bytes=VMEM_LIMIT)


def _mod_kernel(c_ref, w_ref, b_ref, o_ref):
    cond = _silu(c_ref[...])
    tn = w_ref.shape[2]
    cols = [jnp.sum(w_ref[0, :, j * LANES:(j + 1) * LANES] * cond, axis=0, keepdims=True)
            for j in range(tn // LANES)]
    o_ref[0] = jnp.concatenate(cols, axis=1) + b_ref[0]


def _modulation(c, w_mod, b_mod):
    L, D, N = w_mod.shape
    tn = 512
    c_lanes = jnp.broadcast_to(c.reshape(D, 1), (D, LANES))
    out = pl.pallas_call(
        _mod_kernel,
        grid=(L, N // tn),
        in_specs=[pl.BlockSpec((D, LANES), lambda l, j: (0, 0)),
                  pl.BlockSpec((1, D, tn), lambda l, j: (l, 0, j)),
                  pl.BlockSpec((1, 1, tn), lambda l, j: (l, 0, j))],
        out_specs=pl.BlockSpec((1, 1, tn), lambda l, j: (l, 0, j)),
        out_shape=jax.ShapeDtypeStruct((L, 1, N), F32),
        compiler_params=_cparams(("arbitrary", "arbitrary")),
        name="modulation",
    )(c_lanes, w_mod, b_mod.reshape(L, 1, N))
    return out[:, 0, :]


def _normmod(x, gain, shift):
    y = x * lax.rsqrt(jnp.mean(x * x, axis=-1, keepdims=True) + NORM_EPS)
    return y * gain + shift


def _norm_proj_kernel(x_ref, gain_ref, shift_ref, w_ref, o_ref, h_ref):
    @pl.when(pl.program_id(1) == 0)
    def _():
        h_ref[...] = _normmod(x_ref[...], gain_ref[...], shift_ref[...]).astype(BF16)

    o_ref[...] = lax.dot_general(h_ref[...], w_ref[...], NT, preferred_element_type=F32)


def _norm_proj(x, gain, shift, w_t, tm=1024, tn=512):
    T, D = x.shape
    N = w_t.shape[0]
    return pl.pallas_call(
        _norm_proj_kernel,
        grid=(T // tm, N // tn),
        in_specs=[pl.BlockSpec((tm, D), lambda i, j: (i, 0), pipeline_mode=pl.Buffered(1)),
                  pl.BlockSpec((1, D), lambda i, j: (0, 0)),
                  pl.BlockSpec((1, D), lambda i, j: (0, 0)),
                  pl.BlockSpec((tn, D), lambda i, j: (j, 0))],
        out_specs=pl.BlockSpec((tm, tn), lambda i, j: (i, j)),
        out_shape=jax.ShapeDtypeStruct((T, N), F32),
        scratch_shapes=[pltpu.VMEM((tm, D), BF16)],
        compiler_params=_cparams(("arbitrary", "arbitrary")),
        name="norm_proj",
    )(x, gain, shift, w_t)


def _pair_ones():
    r = lax.broadcasted_iota(jnp.int32, (LANES, LANES), 0) // RWKV_HEAD
    c = lax.broadcasted_iota(jnp.int32, (LANES, LANES), 1) // RWKV_HEAD
    return jnp.where(r == c, 1.0, 0.0).astype(F32)


def _rwkv_prep_kernel(has_vres, *refs):
    n_in = 20 if has_vres else 16
    (r_ref, k_ref, v_ref, lo_ref, rp_ref, kp_ref, vp_ref, lop_ref, mu_ref, w0_ref, wup_ref,
     a0_ref, aup_ref, gup_ref, kk_ref, ka_ref) = refs[:16]
    if has_vres:
        v0_ref, vdn_ref, vup_ref, vf_ref = refs[16:20]
    ro_ref, lwo_ref, ko_ref, vo_ref, kko_ref, kkao_ref, go_ref = refs[n_in:]
    first = pl.program_id(0) == 0
    W = r_ref.shape[1]

    def shift(cur_ref, prev_ref, mu):
        cur = cur_ref[...]
        prev_last = jnp.where(first, 0.0, prev_ref[7:8, :])
        rolled = pltpu.roll(cur, 1, axis=0)
        row = lax.broadcasted_iota(jnp.int32, cur.shape, 0)
        prev = jnp.where(row == 0, prev_last, rolled)
        return cur + (prev - cur) * mu

    r = shift(r_ref, rp_ref, mu_ref[:, 0:W])
    k = shift(k_ref, kp_ref, mu_ref[:, W:2 * W])
    v = shift(v_ref, vp_ref, mu_ref[:, 2 * W:3 * W])
    lo = shift(lo_ref, lop_ref, mu_ref[:, 3 * W:])
    w_lo = lo[:, 0:LORA_PAD]
    a_lo = lo[:, LORA_PAD:2 * LORA_PAD]
    g_lo = lo[:, 2 * LORA_PAD:]

    z = w0_ref[...] + _mm(jnp.tanh(w_lo), wup_ref[...])
    lw = -DECAY_SCALE * _sigmoid(z)
    a = _sigmoid(a0_ref[...] + _mm(a_lo, aup_ref[...]))
    g = _mm(_sigmoid(g_lo), gup_ref[...])
    if has_vres:
        mix = _sigmoid(v0_ref[...] + _mm(_mm(v, vdn_ref[...]), vup_ref[...]))
        v = v + (vf_ref[...] - v) * mix

    kkr = k * kk_ref[...]
    sq = kkr * kkr
    ones = _pair_ones()
    ss = jnp.concatenate(
        [_mm_right01(sq[:, j * LANES:(j + 1) * LANES], ones) for j in range(W // LANES)], axis=1)
    kk = kkr / jnp.maximum(jnp.sqrt(ss), 1e-12)
    k2 = k * (1.0 + (a - 1.0) * ka_ref[...])

    ro_ref[...] = r
    lwo_ref[...] = lw
    ko_ref[...] = k2
    vo_ref[...] = v
    kko_ref[...] = kk
    kkao_ref[...] = kk * a
    go_ref[...] = g


def _rwkv_prep(p, mu_p, w0, wup, a0, aup, gup, k_k, k_a, vres, tp=128):
    T = p.shape[0]
    W = w0.shape[1]
    lo_w = 2 * LORA_PAD + G_LORA
    nb = tp // 8
    lo_blk = 3 * W // lo_w

    def cur(cb, width):
        return pl.BlockSpec((tp, width), lambda i, cb=cb: (i, cb))

    def prev(cb, width):
        return pl.BlockSpec((8, width), lambda i, cb=cb: (jnp.maximum(i * nb - 1, 0), cb))

    def vec(n):
        return pl.BlockSpec((1, n), lambda i: (0, 0))

    def mat(a, b):
        return pl.BlockSpec((a, b), lambda i: (0, 0))

    in_specs = [cur(0, W), cur(1, W), cur(2, W), cur(lo_blk, lo_w),
                prev(0, W), prev(1, W), prev(2, W), prev(lo_blk, lo_w),
                vec(3 * W + lo_w), vec(W), mat(LORA_PAD, W), vec(W), mat(LORA_PAD, W),
                mat(G_LORA, W), vec(W), vec(W)]
    args = [p, p, p, p, p, p, p, p, mu_p, w0, wup, a0, aup, gup, k_k, k_a]
    if vres is not None:
        v0, vdn, vup, v_first = vres
        in_specs += [vec(W), mat(W, V_LORA_PAD), mat(V_LORA_PAD, W),
                     pl.BlockSpec((tp, W), lambda i: (i, 0))]
        args += [v0, vdn, vup, v_first]
    out_spec = pl.BlockSpec((tp, W), lambda i: (i, 0))
    sds = jax.ShapeDtypeStruct((T, W), F32)
    return pl.pallas_call(
        functools.partial(_rwkv_prep_kernel, vres is not None),
        grid=(T // tp,),
        in_specs=in_specs,
        out_specs=[out_spec] * 7,
        out_shape=[sds] * 7,
        compiler_params=_cparams(("arbitrary",)),
        name="rwkv_prep",
    )(*args)


def _tri_incl(n):
    row = lax.broadcasted_iota(jnp.int32, (n, n), 0)
    col = lax.broadcasted_iota(jnp.int32, (n, n), 1)
    return jnp.where(col <= row, 1.0, 0.0).astype(BF16)


def _rwkv_pair(r, c, lw, k, v, kk, kka, g, rk, gnw, gnb, st):
    C = r.shape[0]
    C2 = 2 * C
    cp = c - lw
    c_last = c[C - 1:C, :]
    e_c, e_cp, e_nc, e_dl = jnp.exp(c), jnp.exp(cp), jnp.exp(-c), jnp.exp(c_last - c)

    lane = lax.broadcasted_iota(jnp.int32, (C, LANES), 1)
    m0 = lane < RWKV_HEAD

    def stack(x):
        return jnp.concatenate([jnp.where(m0, x, 0.0), jnp.where(m0, 0.0, x)], axis=0)

    aq_rt = jnp.concatenate([stack(kk * e_cp), stack(r * e_c)], axis=0).astype(BF16)
    bt_kt = jnp.concatenate([stack(kka * e_nc), stack(k * e_nc)], axis=0).astype(BF16)
    kw_bw = jnp.concatenate([stack(k * e_dl), stack(kka * e_dl)], axis=0).astype(BF16)
    vs = stack(v)
    vs_b = vs.astype(BF16)

    row = lax.broadcasted_iota(jnp.int32, (C2, C2), 0)
    col = lax.broadcasted_iota(jnp.int32, (C2, C2), 1)
    t_i, s_j = row % C, col % C
    strict, incl, eye = s_j < t_i, s_j <= t_i, row == col

    yield
    p = _mm(aq_rt, bt_kt, NT)
    yield
    l_ab = jnp.where(strict, p[:C2, :C2], 0.0)
    l_ak = jnp.where(strict, p[:C2, C2:], 0.0)
    l_rb = jnp.where(incl, p[C2:, :C2], 0.0)
    l_rk = jnp.where(incl, p[C2:, C2:], 0.0)

    pw = -l_ab
    inv = jnp.where(eye, 1.0, 0.0) + pw
    n = 2
    while n < C:
        pw = _mm(pw, pw)
        yield
        inv = inv + _mm(inv, pw)
        n *= 2

    xy = _mm(aq_rt, st) + _mm(jnp.concatenate([l_ak, l_rk], axis=0), vs_b)
    yield
    u = _mm(inv, xy[:C2])
    yield
    y = xy[C2:] - _mm(l_rb, u)
    w_col = jnp.broadcast_to(jnp.exp(c_last), (C2, C2)).T
    st_new = st * w_col + _mm(kw_bw, jnp.concatenate([vs, -u], axis=0), TN)
    yield

    mh = (row // C) == (col // RWKV_HEAD)
    inv_n = 1.0 / RWKV_HEAD
    mean = jnp.sum(y, axis=1, keepdims=True) * inv_n
    yc = jnp.where(mh, y - mean, 0.0)
    var = jnp.sum(yc * yc, axis=1, keepdims=True) * inv_n
    yn = yc * lax.rsqrt(var + GN_EPS)
    bonus = jnp.sum(stack(r * k * rk), axis=1, keepdims=True) * vs
    yn_t = yn[:C] + yn[C:]
    bonus_t = bonus[:C] + bonus[C:]
    return (yn_t * gnw + gnb + bonus_t) * g, st_new


def _interleave(gens):
    results = [None] * len(gens)
    live = list(enumerate(gens))
    while live:
        still = []
        for j, gen in live:
            try:
                next(gen)
                still.append((j, gen))
            except StopIteration as stop:
                results[j] = stop.value
        live = still
    return results


def _rwkv_rec_kernel(r_ref, lw_ref, k_ref, v_ref, kk_ref, kka_ref, g_ref, rk_ref, gnw_ref, gnb_ref,
                     o_ref, st_ref):
    @pl.when(pl.program_id(1) == 0)
    def _():
        st_ref[...] = jnp.zeros_like(st_ref)

    rows, lanes = r_ref.shape
    tri = _tri_incl(CHUNK)

    def chunk(ci, carry):
        rs = pl.ds(pl.multiple_of(ci * CHUNK, CHUNK), CHUNK)
        lw_all = lw_ref[rs, :]
        c_all = _mm_left01(tri, lw_all)
        r, k, v, kk, kka, g = (ref[rs, :] for ref in (r_ref, k_ref, v_ref, kk_ref, kka_ref, g_ref))
        rk, gnw, gnb = rk_ref[...], gnw_ref[...], gnb_ref[...]
        gens = []
        for j in range(lanes // LANES):
            sl = slice(j * LANES, (j + 1) * LANES)
            gens.append(_rwkv_pair(r[:, sl], c_all[:, sl], lw_all[:, sl], k[:, sl], v[:, sl], kk[:, sl],
                                   kka[:, sl], g[:, sl], rk[:, sl], gnw[:, sl], gnb[:, sl], st_ref[j]))
        results = _interleave(gens)
        for j, (_, st_new) in enumerate(results):
            st_ref[j] = st_new
        o_ref[rs, :] = jnp.concatenate([out for out, _ in results], axis=1).astype(o_ref.dtype)
        return carry

    lax.fori_loop(0, rows // CHUNK, chunk, 0)


def _rwkv_recurrence(r, lw, k, v, kk, kka, g, rk, gnw, gnb, hb_lanes=1024, rows=512):
    T, W = r.shape
    rows = min(rows, T)
    seq = pl.BlockSpec((rows, hb_lanes), lambda h, t: (t, h))
    vec = pl.BlockSpec((1, hb_lanes), lambda h, t: (0, h))
    return pl.pallas_call(
        _rwkv_rec_kernel,
        grid=(W // hb_lanes, T // rows),
        in_specs=[seq] * 7 + [vec] * 3,
        out_specs=seq,
        out_shape=jax.ShapeDtypeStruct((T, W), BF16),
        scratch_shapes=[pltpu.VMEM((hb_lanes // LANES, LANES, LANES), F32)],
        compiler_params=_cparams(("arbitrary", "arbitrary")),
        name="rwkv_recurrence",
    )(r, lw, k, v, kk, kka, g, rk, gnw, gnb)


def _hgrn_head(q_raw, f_raw, i_c, g_raw, lb, gain, tri, s):
    C = q_raw.shape[0]
    q = _silu(q_raw)
    forget = lb + (1.0 - lb) * _sigmoid(f_raw)
    lf = jnp.log(jnp.maximum(forget, MIN_FORGET))
    kin = (1.0 - lb) * _sigmoid(-f_raw)
    b = _mm_left01(tri, lf)
    yield
    o = _mm(q * jnp.exp(b), s)

    lane_s = lax.broadcasted_iota(jnp.int32, (SUB, C), 1)
    row_s = lax.broadcasted_iota(jnp.int32, (SUB, C), 0)
    score_rows = []
    for i in range(C // SUB):
        lo, hi = i * SUB, (i + 1) * SUB
        b_i, q_i = b[lo:hi], q[lo:hi]
        blk = jnp.zeros((SUB, C), F32)
        for sidx in range(SUB):
            srow = lo + sidx
            d = jnp.exp(b_i - b[srow:srow + 1]) * q_i * kin[srow:srow + 1]
            blk = jnp.where(lane_s == srow, jnp.sum(d, axis=1, keepdims=True), blk)
        blk = jnp.where(lane_s <= row_s + lo, blk, 0.0)
        yield
        if i > 0:
            b_st = b[lo - 1:lo]
            qs = q_i * jnp.exp(b_i - b_st)
            ks = kin * jnp.exp(jnp.minimum(b_st - b, 0.0))
            blk = blk + jnp.where(lane_s < lo, _mm(qs, ks, NT), 0.0)
        score_rows.append(blk)
    scores = jnp.concatenate(score_rows, axis=0)
    o = o + _mm(scores, i_c)

    b_last = b[C - 1:C]
    f_col = jnp.broadcast_to(jnp.exp(b_last), (LANES, LANES)).T
    s_new = s * f_col + _mm(kin * jnp.exp(b_last - b), i_c, TN)

    o = o * lax.rsqrt(jnp.mean(o * o, axis=1, keepdims=True) + NORM_EPS)
    return o * gain * _sigmoid(g_raw), s_new


def _hgrn_kernel(q_ref, f_ref, i_ref, g_ref, lb_ref, gain_ref, o_ref, s_ref):
    @pl.when(pl.program_id(1) == 0)
    def _():
        s_ref[...] = jnp.zeros_like(s_ref)

    rows, lanes = q_ref.shape
    tri = _tri_incl(CHUNK)

    def chunk(ci, carry):
        rs = pl.ds(pl.multiple_of(ci * CHUNK, CHUNK), CHUNK)
        q, f, i_c, g = (ref[rs, :] for ref in (q_ref, f_ref, i_ref, g_ref))
        lb, gain = lb_ref[...], gain_ref[...]
        gens = []
        for j in range(lanes // LANES):
            sl = slice(j * LANES, (j + 1) * LANES)
            gens.append(_hgrn_head(q[:, sl], f[:, sl], i_c[:, sl], g[:, sl], lb[:, sl], gain[:, sl],
                                   tri, s_ref[j]))
        results = _interleave(gens)
        for j, (_, s_new) in enumerate(results):
            s_ref[j] = s_new
        o_ref[rs, :] = jnp.concatenate([out for out, _ in results], axis=1).astype(o_ref.dtype)
        return carry

    lax.fori_loop(0, rows // CHUNK, chunk, 0)


def _hgrn(p, col0, lb, gain, hb_lanes=512, rows=512):
    T = p.shape[0]
    W = lb.shape[1]
    rows = min(rows, T)
    nb = W // hb_lanes
    b0 = col0 // hb_lanes

    def sec(n):
        return pl.BlockSpec((rows, hb_lanes), lambda h, t, n=n: (t, b0 + n * nb + h))

    vec = pl.BlockSpec((1, hb_lanes), lambda h, t: (0, h))
    return pl.pallas_call(
        _hgrn_kernel,
        grid=(nb, T // rows),
        in_specs=[sec(0), sec(1), sec(2), sec(3), vec, vec],
        out_specs=pl.BlockSpec((rows, hb_lanes), lambda h, t: (t, h)),
        out_shape=jax.ShapeDtypeStruct((T, W), BF16),
        scratch_shapes=[pltpu.VMEM((hb_lanes // LANES, LANES, LANES), F32)],
        compiler_params=_cparams(("arbitrary", "arbitrary")),
        name="hgrn2",
    )(p, p, p, p, lb, gain)


def _out_proj_kernel(ya_ref, yb_ref, wa_ref, wb_ref, x_ref, gate_ref, o_ref):
    acc = jnp.dot(ya_ref[...], wa_ref[...], preferred_element_type=F32)
    acc = acc + jnp.dot(yb_ref[...], wb_ref[...], preferred_element_type=F32)
    o_ref[...] = x_ref[...] + gate_ref[...] * acc


def _out_proj(ya, yb, w, x, gate, tm=512, tn=1024):
    T, Ka = ya.shape
    Kb = yb.shape[1]
    D = w.shape[1]
    return pl.pallas_call(
        _out_proj_kernel,
        grid=(T // tm, D // tn),
        in_specs=[pl.BlockSpec((tm, Ka), lambda i, j: (i, 0)),
                  pl.BlockSpec((tm, Kb), lambda i, j: (i, 0)),
                  pl.BlockSpec((Ka, tn), lambda i, j: (0, j)),
                  pl.BlockSpec((Kb, tn), lambda i, j: (Ka // Kb, j)),
                  pl.BlockSpec((tm, tn), lambda i, j: (i, j)),
                  pl.BlockSpec((1, tn), lambda i, j: (0, j))],
        out_specs=pl.BlockSpec((tm, tn), lambda i, j: (i, j)),
        out_shape=jax.ShapeDtypeStruct((T, D), F32),
        compiler_params=_cparams(("arbitrary", "arbitrary")),
        name="out_proj",
    )(ya, yb, w, w, x, gate)


def _first_max_onehot(work, axis, size):
    m = jnp.max(work, axis=axis, keepdims=True)
    idx = lax.broadcasted_iota(jnp.int32, work.shape, axis)
    first = jnp.min(jnp.where(work == m, idx, size), axis=axis, keepdims=True)
    return idx == first, m


def _pitch(s_per):
    return s_per + 8 if s_per % 16 == 0 else s_per


def _pack_bf16_pairs(h):
    half = h.shape[1] // 2
    bits = lambda v: lax.bitcast_convert_type(v.astype(BF16).astype(F32), jnp.uint32)
    word = (bits(h[:, :half]) >> 16) | (bits(h[:, half:]) & jnp.uint32(0xFFFF0000))
    return lax.bitcast_convert_type(word, jnp.int32)


def _unpack_bf16_pairs(word):
    u = lax.bitcast_convert_type(word, jnp.uint32)
    lo = lax.bitcast_convert_type(u << 16, F32).astype(BF16)
    hi = lax.bitcast_convert_type(u & jnp.uint32(0xFFFF0000), F32).astype(BF16)
    return lo, hi


def _store_token_major(ref, val):
    n, d = val.shape
    pitch = ref.shape[0] // n
    for s in range(d // LANES):
        ref[pl.ds(s, n, stride=pitch), :] = val[:, s * LANES:(s + 1) * LANES]
    for s in range(d // LANES, pitch):
        ref[pl.ds(s, n, stride=pitch), :] = jnp.zeros((n, LANES), ref.dtype)


def _router_kernel(x_ref, gain_ref, shift_ref, rwt_ref, bias_ref,
                   h_ref, wts_ref, idx_ref, rank_ref, cnt_ref, carry_ref):
    @pl.when(pl.program_id(0) == 0)
    def _():
        carry_ref[...] = jnp.zeros_like(carry_ref)

    h = _normmod(x_ref[...], gain_ref[...], shift_ref[...])
    _store_token_major(h_ref, _pack_bf16_pairs(h))
    tm = h.shape[0]
    per_group = N_EXPERTS // N_GROUPS
    logits = _mmx(rwt_ref[...], h, NT)
    scores = _sigmoid(logits)
    biased = scores + bias_ref[...]

    b3 = biased.reshape(N_GROUPS, per_group, tm)
    pick1, m1 = _first_max_onehot(b3, 1, per_group)
    m2 = jnp.max(jnp.where(pick1, -jnp.inf, b3), axis=1, keepdims=True)
    gscore = (m1 + m2).reshape(N_GROUPS, tm)

    work = gscore
    gsel = jnp.zeros_like(gscore)
    for _ in range(TOPK_GROUPS):
        pick, _m = _first_max_onehot(work, 0, N_GROUPS)
        gsel = jnp.where(pick, 1.0, gsel)
        work = jnp.where(pick, -jnp.inf, work)
    ok = jnp.broadcast_to(gsel.reshape(N_GROUPS, 1, tm), (N_GROUPS, per_group, tm)).reshape(N_EXPERTS, tm)
    work = jnp.where(ok > 0.5, biased, MASKED_SCORE)
    picks = []
    for _ in range(TOP_K):
        pick, _m = _first_max_onehot(work, 0, N_EXPERTS)
        picks.append(pick)
        work = jnp.where(pick, -jnp.inf, work)
    picked = functools.reduce(lambda a, b: a + b, [jnp.where(p, 1.0, 0.0) for p in picks])
    sel = picked * scores
    gates_t = sel / jnp.sum(sel, axis=0, keepdims=True) * ROUTE_SCALE

    ra = lax.broadcasted_iota(jnp.int32, (tm, tm), 0)
    rb = lax.broadcasted_iota(jnp.int32, (tm, tm), 1)
    before = jnp.where(ra < rb, 1.0, 0.0).astype(BF16)
    carry = carry_ref[:, 0:1]
    rank_full = _mm(picked, before) + carry
    e_iota = lax.broadcasted_iota(jnp.int32, (N_EXPERTS, tm), 0).astype(F32)

    def slot_rows(table):
        rows = [jnp.sum(jnp.where(p, table, 0.0), axis=0, keepdims=True) for p in picks]
        return jnp.concatenate(rows, axis=0)

    idx_ref[...] = slot_rows(e_iota).astype(jnp.int32)
    rank_ref[...] = slot_rows(rank_full).astype(jnp.int32)
    wts_ref[...] = slot_rows(gates_t)
    new_carry = carry + jnp.sum(picked, axis=1, keepdims=True)
    carry_ref[...] = jnp.broadcast_to(new_carry, carry_ref.shape)
    cnt_ref[...] = jnp.broadcast_to(new_carry, cnt_ref.shape)


def _router(x, gain, shift, rw_t, bias, tm=256):
    T, D = x.shape
    tm = min(tm, T)
    P = _pitch(D // 2 // LANES)
    return pl.pallas_call(
        _router_kernel,
        grid=(T // tm,),
        in_specs=[pl.BlockSpec((tm, D), lambda i: (i, 0)),
                  pl.BlockSpec((1, D), lambda i: (0, 0)),
                  pl.BlockSpec((1, D), lambda i: (0, 0)),
                  pl.BlockSpec((N_EXPERTS, D), lambda i: (0, 0)),
                  pl.BlockSpec((N_EXPERTS, 1), lambda i: (0, 0))],
        out_specs=[pl.BlockSpec((tm * P, LANES), lambda i: (i, 0)),
                   pl.BlockSpec((TOP_K, tm), lambda i: (0, i)),
                   pl.BlockSpec((TOP_K, tm), lambda i: (0, i)),
                   pl.BlockSpec((TOP_K, tm), lambda i: (0, i)),
                   pl.BlockSpec((N_EXPERTS, LANES), lambda i: (0, 0))],
        out_shape=[jax.ShapeDtypeStruct((T * P, LANES), jnp.int32),
                   jax.ShapeDtypeStruct((TOP_K, T), F32),
                   jax.ShapeDtypeStruct((TOP_K, T), jnp.int32),
                   jax.ShapeDtypeStruct((TOP_K, T), jnp.int32),
                   jax.ShapeDtypeStruct((N_EXPERTS, LANES), F32)],
        scratch_shapes=[pltpu.VMEM((N_EXPERTS, LANES), F32)],
        compiler_params=_cparams(("arbitrary",)),
        name="router",
    )(x, gain, shift, rw_t, bias)


def _dispatch_kernel(P, idx_ref, rank_ref, row_start_ref, pad_start_ref, pad_len_ref, nu_ref, h_ref, xs_hbm,
                     zero_ref, sem, pad_sem):
    i = pl.program_id(0)
    tc = idx_ref.shape[1]
    tile_rows = zero_ref.shape[0]

    def slab(ref, row):
        return ref.at[pl.ds(pl.multiple_of(row * P, math.gcd(P, 8)), P)]

    def token(n, carry):
        src = slab(h_ref, n)
        for j in range(TOP_K):
            dest = row_start_ref[idx_ref[j, n]] + rank_ref[j, n]
            pltpu.make_async_copy(src, slab(xs_hbm, dest), sem).start()
        return carry

    lax.fori_loop(0, tc, token, 0)

    @pl.when(i == 0)
    def _():
        zero_ref[...] = jnp.zeros_like(zero_ref)
        zero_slab = zero_ref.at[pl.ds(0, P)]

        def expert(e, carry):
            def fill(r, c):
                pltpu.make_async_copy(zero_slab, slab(xs_hbm, pad_start_ref[e] + r), pad_sem).start()
                return c

            def drain(r, c):
                pltpu.make_async_copy(zero_slab, slab(xs_hbm, 0), pad_sem).wait()
                return c

            lax.fori_loop(0, pad_len_ref[e], fill, 0)
            lax.fori_loop(0, pad_len_ref[e], drain, 0)
            return carry

        lax.fori_loop(0, N_EXPERTS, expert, 0)

        def tile_dst(t):
            return xs_hbm.at[pl.ds(pl.multiple_of(t * tile_rows, 8), tile_rows)]

        def fill_tile(t, c):
            pltpu.make_async_copy(zero_ref, tile_dst(t), pad_sem).start()
            return c

        def drain_tile(t, c):
            pltpu.make_async_copy(zero_ref, tile_dst(0), pad_sem).wait()
            return c

        n_tiles = xs_hbm.shape[0] // tile_rows
        lax.fori_loop(nu_ref[0], n_tiles, fill_tile, 0)
        lax.fori_loop(nu_ref[0], n_tiles, drain_tile, 0)

    for j in range(TOP_K):
        pltpu.make_async_copy(h_ref, xs_hbm.at[pl.ds(0, tc * P)], sem).wait()


def _dispatch(idx, rank, row_start, pad_start, pad_len, n_used, h_tok, n_tiles, tm_e, tc):
    T = idx.shape[1]
    P = h_tok.shape[0] // T
    smem = pl.BlockSpec(memory_space=pltpu.SMEM)
    slots = pl.BlockSpec((TOP_K, tc), lambda i: (0, i), memory_space=pltpu.SMEM)
    return pl.pallas_call(
        functools.partial(_dispatch_kernel, P),
        grid=(T // tc,),
        in_specs=[slots, slots, smem, smem, smem, smem,
                  pl.BlockSpec((tc * P, LANES), lambda i: (i, 0))],
        out_specs=pl.BlockSpec(memory_space=pl.ANY),
        out_shape=jax.ShapeDtypeStruct((n_tiles * tm_e * P, LANES), h_tok.dtype),
        scratch_shapes=[pltpu.VMEM((tm_e * P, LANES), h_tok.dtype), pltpu.SemaphoreType.DMA,
                        pltpu.SemaphoreType.DMA],
        compiler_params=_cparams(("arbitrary",)),
        name="dispatch",
    )(idx, rank, row_start, pad_start, pad_len, n_used, h_tok)


def _experts_kernel(te_ref, nu_ref, x_ref, wg_ref, wu_ref, wd_ref, o_ref):
    S = wg_ref.shape[3] // LANES
    P = _pitch(S)
    sx = S // 2
    px = _pitch(sx)
    tm = x_ref.shape[0] // px
    kx = 8 if sx % 8 == 0 else sx
    ky = 8 if S % 8 == 0 else S

    @pl.when(pl.program_id(0) >= nu_ref[0])
    def _():
        o_ref[...] = jnp.zeros_like(o_ref)

    @pl.when(pl.program_id(0) < nu_ref[0])
    def _():
        g = u = None
        for c0 in range(0, sx, kx):
            words = jnp.concatenate([x_ref[pl.ds(s, tm, stride=px), :] for s in range(c0, c0 + kx)], axis=1)
            for xk, col0 in zip(_unpack_bf16_pairs(words), (c0, sx + c0)):
                ks = slice(col0 * LANES, (col0 + kx) * LANES)
                gk = lax.dot_general(xk, wg_ref[0, 0, :, ks].astype(BF16), NT, preferred_element_type=F32)
                uk = lax.dot_general(xk, wu_ref[0, 0, :, ks].astype(BF16), NT, preferred_element_type=F32)
                g = gk if g is None else g + gk
                u = uk if u is None else u + uk
        act = (_silu(g) * u).astype(BF16)
        for c0 in range(0, S, ky):
            ns = slice(c0 * LANES, (c0 + ky) * LANES)
            y = jnp.dot(act, wd_ref[0, 0, :, ns].astype(BF16), preferred_element_type=F32)
            for s in range(ky):
                o_ref[pl.ds(c0 + s, tm, stride=P), :] = y[:, s * LANES:(s + 1) * LANES]
        for s in range(S, P):
            o_ref[pl.ds(s, tm, stride=P), :] = jnp.zeros((tm, LANES), F32)


def _experts(tile_expert, n_used, x_tok, wg, wu, wd, layer, tm):
    _, E, ff, D = wg.shape
    P = _pitch(D // LANES)
    px = _pitch(D // 2 // LANES)
    n_tiles = x_tok.shape[0] // (tm * px)

    def row_map(i, te, nu):
        return (jnp.minimum(i, nu[0] - 1), 0)

    def w_map(i, te, nu):
        return (layer, te[i], 0, 0)

    grid_spec = pltpu.PrefetchScalarGridSpec(
        num_scalar_prefetch=2,
        grid=(n_tiles,),
        in_specs=[pl.BlockSpec((tm * px, LANES), row_map),
                  pl.BlockSpec((1, 1, ff, D), w_map),
                  pl.BlockSpec((1, 1, ff, D), w_map),
                  pl.BlockSpec((1, 1, ff, D), w_map)],
        out_specs=pl.BlockSpec((tm * P, LANES), lambda i, te, nu: (i, 0)),
    )
    return pl.pallas_call(
        _experts_kernel,
        grid_spec=grid_spec,
        out_shape=jax.ShapeDtypeStruct((n_tiles * tm * P, LANES), F32),
        compiler_params=_cparams(("arbitrary",)),
        name="experts",
    )(tile_expert, n_used, x_tok, wg, wu, wd)


def _combine_kernel(has_norm, idx_ref, rank_ref, row_start_ref, w_ref, ysh_ref, x_ref, gate2_ref, *rest):
    if has_norm:
        norm_gain_ref, y_hbm, o_ref, buf_ref, acc_ref, sem = rest
    else:
        y_hbm, o_ref, buf_ref, acc_ref, sem = rest
    tc, D = x_ref.shape
    S = D // LANES
    P = _pitch(S)
    align = math.gcd(P, 8)
    slot_rows = tc * P

    def token(n, carry):
        for j in range(TOP_K):
            dest = row_start_ref[idx_ref[j, n]] + rank_ref[j, n]
            src = y_hbm.at[pl.ds(pl.multiple_of(dest * P, align), S)]
            dst = buf_ref.at[pl.ds(pl.multiple_of(j * slot_rows + n * P, align), S)]
            pltpu.make_async_copy(src, dst, sem).start()
        return carry

    lax.fori_loop(0, tc, token, 0)
    n_rows = TOP_K * tc * S
    pltpu.make_async_copy(y_hbm.at[pl.ds(0, n_rows)], buf_ref.at[pl.ds(0, n_rows)], sem).wait()

    def accumulate(n, carry):
        row = pl.multiple_of(n * P, align)
        acc = ysh_ref[pl.ds(row, S), :]
        for j in range(TOP_K):
            slab = buf_ref[pl.ds(pl.multiple_of(j * slot_rows + n * P, align), S), :]
            acc = acc + w_ref[j, n] * slab
        acc_ref[pl.ds(row, S), :] = acc
        return carry

    lax.fori_loop(0, tc, accumulate, 0, unroll=2)
    for s in range(S):
        ls = slice(s * LANES, (s + 1) * LANES)
        o_ref[:, ls] = x_ref[:, ls] + gate2_ref[:, ls] * acc_ref[pl.ds(s, tc, stride=P), :]
    if has_norm:
        xo = o_ref[...]
        o_ref[...] = xo * lax.rsqrt(jnp.mean(xo * xo, axis=-1, keepdims=True) + NORM_EPS) * norm_gain_ref[...]


def _combine(idx, rank, row_start, wts, y_shared, x, gate2, y_sorted, norm_gain=None, tc=128):
    T, D = x.shape
    tc = min(tc, T)
    P = _pitch(D // LANES)
    slots = pl.BlockSpec((TOP_K, tc), lambda i: (0, i), memory_space=pltpu.SMEM)
    vec = pl.BlockSpec((1, D), lambda i: (0, 0))
    in_specs = [slots, slots, pl.BlockSpec(memory_space=pltpu.SMEM), slots,
                pl.BlockSpec((tc * P, LANES), lambda i: (i, 0)),
                pl.BlockSpec((tc, D), lambda i: (i, 0)), vec]
    args = [idx, rank, row_start, wts, y_shared, x, gate2]
    if norm_gain is not None:
        in_specs.append(vec)
        args.append(norm_gain)
    in_specs.append(pl.BlockSpec(memory_space=pl.ANY))
    args.append(y_sorted)
    return pl.pallas_call(
        functools.partial(_combine_kernel, norm_gain is not None),
        grid=(T // tc,),
        in_specs=in_specs,
        out_specs=pl.BlockSpec((tc, D), lambda i: (i, 0)),
        out_shape=jax.ShapeDtypeStruct((T, D), F32),
        scratch_shapes=[pltpu.VMEM((TOP_K * tc * P, LANES), F32), pltpu.VMEM((tc * P, LANES), F32),
                        pltpu.SemaphoreType.DMA],
        compiler_params=_cparams(("arbitrary",)),
        name="combine",
    )(*args)


def _moe(x, gain, shift, gate2, rw_t, bias, wg, wu, wd, swg, swu, swd, layer, norm_gain=None, tm_e=256,
         tc_d=256):
    T, D = x.shape
    tm_e = min(tm_e, T)
    tc_d = min(tc_d, T)
    h_tok, wts, idx, rank, cnt = _router(x, gain, shift, rw_t, bias)
    counts = cnt[:, 0].astype(jnp.int32)
    tiles = (counts + tm_e - 1) // tm_e
    tiles_cum = jnp.cumsum(tiles)
    row_start = (tiles_cum - tiles) * tm_e
    n_tiles = (T * TOP_K) // tm_e + N_EXPERTS
    tile_ids = jnp.arange(n_tiles, dtype=jnp.int32)
    tile_expert = jnp.minimum(jnp.sum((tiles_cum[None, :] <= tile_ids[:, None]).astype(jnp.int32), axis=1),
                              N_EXPERTS - 1)
    n_used = tiles_cum[-1:].astype(jnp.int32)
    x_sorted = _dispatch(idx, rank, row_start, row_start + counts, tiles * tm_e - counts, n_used, h_tok,
                         n_tiles, tm_e, tc_d)
    y_sorted = _experts(tile_expert, n_used, x_sorted, wg, wu, wd, layer, tm_e)
    n_sh = T // tm_e
    y_shared = _experts(jnp.zeros((n_sh,), jnp.int32), jnp.full((1,), n_sh, jnp.int32), h_tok,
                        swg[:, None], swu[:, None], swd[:, None], layer, tm_e)
    return _combine(idx, rank, row_start, wts, y_shared, x, gate2, y_sorted, norm_gain)


def _pad_cols(a, n):
    return jnp.pad(a, ((0, 0), (0, n - a.shape[1])))


def _pad_rows(a, n):
    return jnp.pad(a, ((0, n - a.shape[0]), (0, 0)))


def _pad_rwkv_cols(a, W, w_lora, a_lora):
    c0 = 3 * W
    return jnp.concatenate([a[:, :c0], _pad_cols(a[:, c0:c0 + w_lora], LORA_PAD),
                            _pad_cols(a[:, c0 + w_lora:c0 + w_lora + a_lora], LORA_PAD),
                            a[:, c0 + w_lora + a_lora:]], axis=1)


def kernel(x, c, w_mod, b_mod, norm_mix, norm_ffn, w_in, rwkv_mu, rwkv_w0, rwkv_w_up, rwkv_a0, rwkv_a_up, rwkv_g_up, rwkv_k_k, rwkv_k_a, rwkv_r_k, rwkv_gn_w, rwkv_gn_b, rwkv_v0, rwkv_v_down, rwkv_v_up, hgrn_lower_bounds, hgrn_norm, w_out, router_w, router_bias, expert_w_gate, expert_w_up, expert_w_down, shared_w_gate, shared_w_up, shared_w_down, final_norm):
    B, T, D = x.shape
    L = w_mod.shape[0]
    W = rwkv_w0.shape[1]
    w_lora, a_lora = rwkv_w_up.shape[1], rwkv_a_up.shape[1]
    rwkv_cols = rwkv_mu.shape[1]
    rwkv_cols_p = 3 * W + 2 * LORA_PAD + G_LORA

    lb_soft = jax.nn.softmax(hgrn_lower_bounds.astype(F32), axis=0)
    lbs = jnp.cumsum(lb_soft, axis=0) - lb_soft[0]
    mod = _modulation(c, w_mod, b_mod)

    xs = x.reshape(B * T, D)
    v_first = None
    for l in range(L):
        sh1, sc1, g1, sh2, sc2, g2 = [mod[l, n * D:(n + 1) * D].reshape(1, D) for n in range(6)]
        w_in_p = _pad_rwkv_cols(w_in[l], W, w_lora, a_lora).T.astype(BF16)
        p = _norm_proj(xs, norm_mix[l].reshape(1, D) * (1.0 + sc1), sh1, w_in_p)
        mu_p = _pad_rwkv_cols(rwkv_mu[l].reshape(1, -1), W, w_lora, a_lora)
        vres = None
        if l > 0:
            vres = (rwkv_v0[l - 1].reshape(1, W), _pad_cols(rwkv_v_down[l - 1], V_LORA_PAD).astype(BF16),
                    _pad_rows(rwkv_v_up[l - 1], V_LORA_PAD).astype(BF16), v_first)
        r, lw, k, v, kk, kka, g = _rwkv_prep(
            p, mu_p, rwkv_w0[l].reshape(1, W), _pad_rows(rwkv_w_up[l], LORA_PAD).astype(BF16),
            rwkv_a0[l].reshape(1, W), _pad_rows(rwkv_a_up[l], LORA_PAD).astype(BF16),
            rwkv_g_up[l].astype(BF16), rwkv_k_k[l].reshape(1, W), rwkv_k_a[l].reshape(1, W), vres)
        if l == 0:
            v_first = v
        y_r = _rwkv_recurrence(r, lw, k, v, kk, kka, g, rwkv_r_k[l].reshape(1, W),
                               rwkv_gn_w[l].reshape(1, W), rwkv_gn_b[l].reshape(1, W))
        y_h = _hgrn(p, rwkv_cols_p, lbs[l].reshape(1, -1), hgrn_norm[l].reshape(1, -1))
        xs = _out_proj(y_r, y_h, w_out[l].astype(BF16), xs, g1)
        xs = _moe(xs, norm_ffn[l].reshape(1, D) * (1.0 + sc2), sh2, g2, router_w[l].T,
                  router_bias[l].reshape(-1, 1), jnp.swapaxes(expert_w_gate, 2, 3),
                  jnp.swapaxes(expert_w_up, 2, 3), expert_w_down, jnp.swapaxes(shared_w_gate, 1, 2),
                  jnp.swapaxes(shared_w_up, 1, 2), shared_w_down, l,
                  final_norm.reshape(1, D) if l == L - 1 else None)
    return xs.reshape(B, T, D)
```

```python
import functools
import math

import jax
import jax.numpy as jnp
from jax import lax
from jax.experimental import pallas as pl
from jax.experimental.pallas import tpu as pltpu

F32 = jnp.float32
BF16 = jnp.bfloat16

LANES = 128
VMEM_LIMIT = 56 * 1024 * 1024

RWKV_HEAD = 64
HGRN_HEAD = 128
CHUNK = 64
SUB = 16
LORA_PAD = 128
G_LORA = 256
V_LORA_PAD = 128
N_EXPERTS = 64
N_GROUPS = 8
TOPK_GROUPS = 4
TOP_K = 8
ROUTE_SCALE = 2.5
MASKED_SCORE = -1e4
MIN_FORGET = 1e-30
NORM_EPS = 1e-6
GN_EPS = 64e-5
DECAY_SCALE = 0.6065306597126334

NN = (((1,), (0,)), ((), ()))
NT = (((1,), (1,)), ((), ()))
TN = (((0,), (0,)), ((), ()))


def _mm(a, b, dims=NN):
    return lax.dot_general(a.astype(BF16), b.astype(BF16), dims, preferred_element_type=F32)


def _mmx(a, b, dims=NN):
    return lax.dot_general(a.astype(F32), b.astype(F32), dims, preferred_element_type=F32,
                           precision=lax.Precision.HIGHEST)


def _split3(x):
    hi = x.astype(BF16)
    r1 = x - hi.astype(F32)
    mid = r1.astype(BF16)
    lo = (r1 - mid.astype(F32)).astype(BF16)
    return hi, mid, lo


def _mm_left01(a01, b, dims=NN):
    a = a01.astype(BF16)
    d = functools.partial(lax.dot_general, dimension_numbers=dims, preferred_element_type=F32)
    hi, mid, lo = _split3(b)
    return d(a, hi) + d(a, mid) + d(a, lo)


def _mm_right01(a, b01, dims=NN):
    b = b01.astype(BF16)
    d = functools.partial(lax.dot_general, dimension_numbers=dims, preferred_element_type=F32)
    hi, mid, lo = _split3(a)
    return d(hi, b) + d(mid, b) + d(lo, b)


def _sigmoid(x):
    return 1.0 / (1.0 + jnp.exp(-x))


def _silu(x):
    return x * _sigmoid(x)


def _cparams(sem):
    return pltpu.CompilerParams(dimension_semantics=sem, vmem_limit_bytes=VMEM_LIMIT)


def _mod_kernel(c_ref, w_ref, b_ref, o_ref):
    cond = _silu(c_ref[...])
    tn = w_ref.shape[2]
    cols = [jnp.sum(w_ref[0, :, j * LANES:(j + 1) * LANES] * cond, axis=0, keepdims=True)
            for j in range(tn // LANES)]
    o_ref[0] = jnp.concatenate(cols, axis=1) + b_ref[0]


def _modulation(c, w_mod, b_mod):
    L, D, N = w_mod.shape
    tn = 512
    c_lanes = jnp.broadcast_to(c.reshape(D, 1), (D, LANES))
    out = pl.pallas_call(
        _mod_kernel,
        grid=(L, N // tn),
        in_specs=[pl.BlockSpec((D, LANES), lambda l, j: (0, 0)),
                  pl.BlockSpec((1, D, tn), lambda l, j: (l, 0, j)),
                  pl.BlockSpec((1, 1, tn), lambda l, j: (l, 0, j))],
        out_specs=pl.BlockSpec((1, 1, tn), lambda l, j: (l, 0, j)),
        out_shape=jax.ShapeDtypeStruct((L, 1, N), F32),
        compiler_params=_cparams(("arbitrary", "arbitrary")),
        name="modulation",
    )(c_lanes, w_mod, b_mod.reshape(L, 1, N))
    return out[:, 0, :]


def _normmod(x, gain, shift):
    y = x * lax.rsqrt(jnp.mean(x * x, axis=-1, keepdims=True) + NORM_EPS)
    return y * gain + shift


def _norm_proj_kernel(x_ref, gain_ref, shift_ref, w_ref, o_ref, h_ref):
    @pl.when(pl.program_id(1) == 0)
    def _():
        h_ref[...] = _normmod(x_ref[...], gain_ref[...], shift_ref[...]).astype(BF16)

    o_ref[...] = lax.dot_general(h_ref[...], w_ref[...], NT, preferred_element_type=F32)


def _norm_proj(x, gain, shift, w_t, tm=1024, tn=512):
    T, D = x.shape
    N = w_t.shape[0]
    return pl.pallas_call(
        _norm_proj_kernel,
        grid=(T // tm, N // tn),
        in_specs=[pl.BlockSpec((tm, D), lambda i, j: (i, 0), pipeline_mode=pl.Buffered(1)),
                  pl.BlockSpec((1, D), lambda i, j: (0, 0)),
                  pl.BlockSpec((1, D), lambda i, j: (0, 0)),
                  pl.BlockSpec((tn, D), lambda i, j: (j, 0))],
        out_specs=pl.BlockSpec((tm, tn), lambda i, j: (i, j)),
        out_shape=jax.ShapeDtypeStruct((T, N), F32),
        scratch_shapes=[pltpu.VMEM((tm, D), BF16)],
        compiler_params=_cparams(("arbitrary", "arbitrary")),
        name="norm_proj",
    )(x, gain, shift, w_t)


def _pair_ones():
    r = lax.broadcasted_iota(jnp.int32, (LANES, LANES), 0) // RWKV_HEAD
    c = lax.broadcasted_iota(jnp.int32, (LANES, LANES), 1) // RWKV_HEAD
    return jnp.where(r == c, 1.0, 0.0).astype(F32)


def _rwkv_prep_kernel(has_vres, *refs):
    n_in = 20 if has_vres else 16
    (r_ref, k_ref, v_ref, lo_ref, rp_ref, kp_ref, vp_ref, lop_ref, mu_ref, w0_ref, wup_ref,
     a0_ref, aup_ref, gup_ref, kk_ref, ka_ref) = refs[:16]
    if has_vres:
        v0_ref, vdn_ref, vup_ref, vf_ref = refs[16:20]
    ro_ref, lwo_ref, ko_ref, vo_ref, kko_ref, kkao_ref, go_ref = refs[n_in:]
    first = pl.program_id(0) == 0
    W = r_ref.shape[1]

    def shift(cur_ref, prev_ref, mu):
        cur = cur_ref[...]
        prev_last = jnp.where(first, 0.0, prev_ref[7:8, :])
        rolled = pltpu.roll(cur, 1, axis=0)
        row = lax.broadcasted_iota(jnp.int32, cur.shape, 0)
        prev = jnp.where(row == 0, prev_last, rolled)
        return cur + (prev - cur) * mu

    r = shift(r_ref, rp_ref, mu_ref[:, 0:W])
    k = shift(k_ref, kp_ref, mu_ref[:, W:2 * W])
    v = shift(v_ref, vp_ref, mu_ref[:, 2 * W:3 * W])
    lo = shift(lo_ref, lop_ref, mu_ref[:, 3 * W:])
    w_lo = lo[:, 0:LORA_PAD]
    a_lo = lo[:, LORA_PAD:2 * LORA_PAD]
    g_lo = lo[:, 2 * LORA_PAD:]

    z = w0_ref[...] + _mm(jnp.tanh(w_lo), wup_ref[...])
    lw = -DECAY_SCALE * _sigmoid(z)
    a = _sigmoid(a0_ref[...] + _mm(a_lo, aup_ref[...]))
    g = _mm(_sigmoid(g_lo), gup_ref[...])
    if has_vres:
        mix = _sigmoid(v0_ref[...] + _mm(_mm(v, vdn_ref[...]), vup_ref[...]))
        v = v + (vf_ref[...] - v) * mix

    kkr = k * kk_ref[...]
    sq = kkr * kkr
    ones = _pair_ones()
    ss = jnp.concatenate(
        [_mm_right01(sq[:, j * LANES:(j + 1) * LANES], ones) for j in range(W // LANES)], axis=1)
    kk = kkr / jnp.maximum(jnp.sqrt(ss), 1e-12)
    k2 = k * (1.0 + (a - 1.0) * ka_ref[...])

    ro_ref[...] = r
    lwo_ref[...] = lw
    ko_ref[...] = k2
    vo_ref[...] = v
    kko_ref[...] = kk
    kkao_ref[...] = kk * a
    go_ref[...] = g


def _rwkv_prep(p, mu_p, w0, wup, a0, aup, gup, k_k, k_a, vres, tp=128):
    T = p.shape[0]
    W = w0.shape[1]
    lo_w = 2 * LORA_PAD + G_LORA
    nb = tp // 8
    lo_blk = 3 * W // lo_w

    def cur(cb, width):
        return pl.BlockSpec((tp, width), lambda i, cb=cb: (i, cb))

    def prev(cb, width):
        return pl.BlockSpec((8, width), lambda i, cb=cb: (jnp.maximum(i * nb - 1, 0), cb))

    def vec(n):
        return pl.BlockSpec((1, n), lambda i: (0, 0))

    def mat(a, b):
        return pl.BlockSpec((a, b), lambda i: (0, 0))

    in_specs = [cur(0, W), cur(1, W), cur(2, W), cur(lo_blk, lo_w),
                prev(0, W), prev(1, W), prev(2, W), prev(lo_blk, lo_w),
                vec(3 * W + lo_w), vec(W), mat(LORA_PAD, W), vec(W), mat(LORA_PAD, W),
                mat(G_LORA, W), vec(W), vec(W)]
    args = [p, p, p, p, p, p, p, p, mu_p, w0, wup, a0, aup, gup, k_k, k_a]
    if vres is not None:
        v0, vdn, vup, v_first = vres
        in_specs += [vec(W), mat(W, V_LORA_PAD), mat(V_LORA_PAD, W),
                     pl.BlockSpec((tp, W), lambda i: (i, 0))]
        args += [v0, vdn, vup, v_first]
    out_spec = pl.BlockSpec((tp, W), lambda i: (i, 0))
    sds = jax.ShapeDtypeStruct((T, W), F32)
    return pl.pallas_call(
        functools.partial(_rwkv_prep_kernel, vres is not None),
        grid=(T // tp,),
        in_specs=in_specs,
        out_specs=[out_spec] * 7,
        out_shape=[sds] * 7,
        compiler_params=_cparams(("arbitrary",)),
        name="rwkv_prep",
    )(*args)


def _tri_incl(n):
    row = lax.broadcasted_iota(jnp.int32, (n, n), 0)
    col = lax.broadcasted_iota(jnp.int32, (n, n), 1)
    return jnp.where(col <= row, 1.0, 0.0).astype(BF16)


def _rwkv_pair(r, c, lw, k, v, kk, kka, g, rk, gnw, gnb, st):
    C = r.shape[0]
    C2 = 2 * C
    cp = c - lw
    c_last = c[C - 1:C, :]
    e_c, e_cp, e_nc, e_dl = jnp.exp(c), jnp.exp(cp), jnp.exp(-c), jnp.exp(c_last - c)

    lane = lax.broadcasted_iota(jnp.int32, (C, LANES), 1)
    m0 = lane < RWKV_HEAD

    def stack(x):
        return jnp.concatenate([jnp.where(m0, x, 0.0), jnp.where(m0, 0.0, x)], axis=0)

    aq_rt = jnp.concatenate([stack(kk * e_cp), stack(r * e_c)], axis=0).astype(BF16)
    bt_kt = jnp.concatenate([stack(kka * e_nc), stack(k * e_nc)], axis=0).astype(BF16)
    kw_bw = jnp.concatenate([stack(k * e_dl), stack(kka * e_dl)], axis=0).astype(BF16)
    vs = stack(v)
    vs_b = vs.astype(BF16)

    row = lax.broadcasted_iota(jnp.int32, (C2, C2), 0)
    col = lax.broadcasted_iota(jnp.int32, (C2, C2), 1)
    t_i, s_j = row % C, col % C
    strict, incl, eye = s_j < t_i, s_j <= t_i, row == col

    yield
    p = _mm(aq_rt, bt_kt, NT)
    yield
    l_ab = jnp.where(strict, p[:C2, :C2], 0.0)
    l_ak = jnp.where(strict, p[:C2, C2:], 0.0)
    l_rb = jnp.where(incl, p[C2:, :C2], 0.0)
    l_rk = jnp.where(incl, p[C2:, C2:], 0.0)

    pw = -l_ab
    inv = jnp.where(eye, 1.0, 0.0) + pw
    n = 2
    while n < C:
        pw = _mm(pw, pw)
        yield
        inv = inv + _mm(inv, pw)
        n *= 2

    xy = _mm(aq_rt, st) + _mm(jnp.concatenate([l_ak, l_rk], axis=0), vs_b)
    yield
    u = _mm(inv, xy[:C2])
    yield
    y = xy[C2:] - _mm(l_rb, u)
    w_col = jnp.broadcast_to(jnp.exp(c_last), (C2, C2)).T
    st_new = st * w_col + _mm(kw_bw, jnp.concatenate([vs, -u], axis=0), TN)
    yield

    mh = (row // C) == (col // RWKV_HEAD)
    inv_n = 1.0 / RWKV_HEAD
    mean = jnp.sum(y, axis=1, keepdims=True) * inv_n
    yc = jnp.where(mh, y - mean, 0.0)
    var = jnp.sum(yc * yc, axis=1, keepdims=True) * inv_n
    yn = yc * lax.rsqrt(var + GN_EPS)
    bonus = jnp.sum(stack(r * k * rk), axis=1, keepdims=True) * vs
    yn_t = yn[:C] + yn[C:]
    bonus_t = bonus[:C] + bonus[C:]
    return (yn_t * gnw + gnb + bonus_t) * g, st_new


def _interleave(gens):
    results = [None] * len(gens)
    live = list(enumerate(gens))
    while live:
        still = []
        for j, gen in live:
            try:
                next(gen)
                still.append((j, gen))
            except StopIteration as stop:
                results[j] = stop.value
        live = still
    return results


def _rwkv_rec_kernel(r_ref, lw_ref, k_ref, v_ref, kk_ref, kka_ref, g_ref, rk_ref, gnw_ref, gnb_ref,
                     o_ref, st_ref):
    @pl.when(pl.program_id(1) == 0)
    def _():
        st_ref[...] = jnp.zeros_like(st_ref)

    rows, lanes = r_ref.shape
    tri = _tri_incl(CHUNK)

    def chunk(ci, carry):
        rs = pl.ds(pl.multiple_of(ci * CHUNK, CHUNK), CHUNK)
        lw_all = lw_ref[rs, :]
        c_all = _mm_left01(tri, lw_all)
        r, k, v, kk, kka, g = (ref[rs, :] for ref in (r_ref, k_ref, v_ref, kk_ref, kka_ref, g_ref))
        rk, gnw, gnb = rk_ref[...], gnw_ref[...], gnb_ref[...]
        gens = []
        for j in range(lanes // LANES):
            sl = slice(j * LANES, (j + 1) * LANES)
            gens.append(_rwkv_pair(r[:, sl], c_all[:, sl], lw_all[:, sl], k[:, sl], v[:, sl], kk[:, sl],
                                   kka[:, sl], g[:, sl], rk[:, sl], gnw[:, sl], gnb[:, sl], st_ref[j]))
        results = _interleave(gens)
        for j, (_, st_new) in enumerate(results):
            st_ref[j] = st_new
        o_ref[rs, :] = jnp.concatenate([out for out, _ in results], axis=1).astype(o_ref.dtype)
        return carry

    lax.fori_loop(0, rows // CHUNK, chunk, 0)


def _rwkv_recurrence(r, lw, k, v, kk, kka, g, rk, gnw, gnb, hb_lanes=1024, rows=512):
    T, W = r.shape
    rows = min(rows, T)
    seq = pl.BlockSpec((rows, hb_lanes), lambda h, t: (t, h))
    vec = pl.BlockSpec((1, hb_lanes), lambda h, t: (0, h))
    return pl.pallas_call(
        _rwkv_rec_kernel,
        grid=(W // hb_lanes, T // rows),
        in_specs=[seq] * 7 + [vec] * 3,
        out_specs=seq,
        out_shape=jax.ShapeDtypeStruct((T, W), BF16),
        scratch_shapes=[pltpu.VMEM((hb_lanes // LANES, LANES, LANES), F32)],
        compiler_params=_cparams(("arbitrary", "arbitrary")),
        name="rwkv_recurrence",
    )(r, lw, k, v, kk, kka, g, rk, gnw, gnb)


def _hgrn_head(q_raw, f_raw, i_c, g_raw, lb, gain, tri, s):
    C = q_raw.shape[0]
    q = _silu(q_raw)
    forget = lb + (1.0 - lb) * _sigmoid(f_raw)
    lf = jnp.log(jnp.maximum(forget, MIN_FORGET))
    kin = (1.0 - lb) * _sigmoid(-f_raw)
    b = _mm_left01(tri, lf)
    yield
    o = _mm(q * jnp.exp(b), s)

    lane_s = lax.broadcasted_iota(jnp.int32, (SUB, C), 1)
    row_s = lax.broadcasted_iota(jnp.int32, (SUB, C), 0)
    score_rows = []
    for i in range(C // SUB):
        lo, hi = i * SUB, (i + 1) * SUB
        b_i, q_i = b[lo:hi], q[lo:hi]
        blk = jnp.zeros((SUB, C), F32)
        for sidx in range(SUB):
            srow = lo + sidx
            d = jnp.exp(b_i - b[srow:srow + 1]) * q_i * kin[srow:srow + 1]
            blk = jnp.where(lane_s == srow, jnp.sum(d, axis=1, keepdims=True), blk)
        blk = jnp.where(lane_s <= row_s + lo, blk, 0.0)
        yield
        if i > 0:
            b_st = b[lo - 1:lo]
            qs = q_i * jnp.exp(b_i - b_st)
            ks = kin * jnp.exp(jnp.minimum(b_st - b, 0.0))
            blk = blk + jnp.where(lane_s < lo, _mm(qs, ks, NT), 0.0)
        score_rows.append(blk)
    scores = jnp.concatenate(score_rows, axis=0)
    o = o + _mm(scores, i_c)

    b_last = b[C - 1:C]
    f_col = jnp.broadcast_to(jnp.exp(b_last), (LANES, LANES)).T
    s_new = s * f_col + _mm(kin * jnp.exp(b_last - b), i_c, TN)

    o = o * lax.rsqrt(jnp.mean(o * o, axis=1, keepdims=True) + NORM_EPS)
    return o * gain * _sigmoid(g_raw), s_new


def _hgrn_kernel(q_ref, f_ref, i_ref, g_ref, lb_ref, gain_ref, o_ref, s_ref):
    @pl.when(pl.program_id(1) == 0)
    def _():
        s_ref[...] = jnp.zeros_like(s_ref)

    rows, lanes = q_ref.shape
    tri = _tri_incl(CHUNK)

    def chunk(ci, carry):
        rs = pl.ds(pl.multiple_of(ci * CHUNK, CHUNK), CHUNK)
        q, f, i_c, g = (ref[rs, :] for ref in (q_ref, f_ref, i_ref, g_ref))
        lb, gain = lb_ref[...], gain_ref[...]
        gens = []
        for j in range(lanes // LANES):
            sl = slice(j * LANES, (j + 1) * LANES)
            gens.append(_hgrn_head(q[:, sl], f[:, sl], i_c[:, sl], g[:, sl], lb[:, sl], gain[:, sl],
                                   tri, s_ref[j]))
        results = _interleave(gens)
        for j, (_, s_new) in enumerate(results):
            s_ref[j] = s_new
        o_ref[rs, :] = jnp.concatenate([out for out, _ in results], axis=1).astype(o_ref.dtype)
        return carry

    lax.fori_loop(0, rows // CHUNK, chunk, 0)


def _hgrn(p, col0, lb, gain, hb_lanes=512, rows=512):
    T = p.shape[0]
    W = lb.shape[1]
    rows = min(rows, T)
    nb = W // hb_lanes
    b0 = col0 // hb_lanes

    def sec(n):
        return pl.BlockSpec((rows, hb_lanes), lambda h, t, n=n: (t, b0 + n * nb + h))

    vec = pl.BlockSpec((1, hb_lanes), lambda h, t: (0, h))
    return pl.pallas_call(
        _hgrn_kernel,
        grid=(nb, T // rows),
        in_specs=[sec(0), sec(1), sec(2), sec(3), vec, vec],
        out_specs=pl.BlockSpec((rows, hb_lanes), lambda h, t: (t, h)),
        out_shape=jax.ShapeDtypeStruct((T, W), BF16),
        scratch_shapes=[pltpu.VMEM((hb_lanes // LANES, LANES, LANES), F32)],
        compiler_params=_cparams(("arbitrary", "arbitrary")),
        name="hgrn2",
    )(p, p, p, p, lb, gain)


def _out_proj_kernel(ya_ref, yb_ref, wa_ref, wb_ref, x_ref, gate_ref, o_ref):
    acc = jnp.dot(ya_ref[...], wa_ref[...], preferred_element_type=F32)
    acc = acc + jnp.dot(yb_ref[...], wb_ref[...], preferred_element_type=F32)
    o_ref[...] = x_ref[...] + gate_ref[...] * acc


def _out_proj(ya, yb, w, x, gate, tm=512, tn=1024):
    T, Ka = ya.shape
    Kb = yb.shape[1]
    D = w.shape[1]
    return pl.pallas_call(
        _out_proj_kernel,
        grid=(T // tm, D // tn),
        in_specs=[pl.BlockSpec((tm, Ka), lambda i, j: (i, 0)),
                  pl.BlockSpec((tm, Kb), lambda i, j: (i, 0)),
                  pl.BlockSpec((Ka, tn), lambda i, j: (0, j)),
                  pl.BlockSpec((Kb, tn), lambda i, j: (Ka // Kb, j)),
                  pl.BlockSpec((tm, tn), lambda i, j: (i, j)),
                  pl.BlockSpec((1, tn), lambda i, j: (0, j))],
        out_specs=pl.BlockSpec((tm, tn), lambda i, j: (i, j)),
        out_shape=jax.ShapeDtypeStruct((T, D), F32),
        compiler_params=_cparams(("arbitrary", "arbitrary")),
        name="out_proj",
    )(ya, yb, w, w, x, gate)


def _first_max_onehot(work, axis, size):
    m = jnp.max(work, axis=axis, keepdims=True)
    idx = lax.broadcasted_iota(jnp.int32, work.shape, axis)
    first = jnp.min(jnp.where(work == m, idx, size), axis=axis, keepdims=True)
    return idx == first, m


def _pitch(s_per):
    return s_per + 8 if s_per % 16 == 0 else s_per


def _pack_bf16_pairs(h):
    half = h.shape[1] // 2
    bits = lambda v: lax.bitcast_convert_type(v.astype(BF16).astype(F32), jnp.uint32)
    word = (bits(h[:, :half]) >> 16) | (bits(h[:, half:]) & jnp.uint32(0xFFFF0000))
    return lax.bitcast_convert_type(word, jnp.int32)


def _unpack_bf16_pairs(word, dtype=BF16):
    u = lax.bitcast_convert_type(word, jnp.uint32)
    lo = lax.bitcast_convert_type(u << 16, F32).astype(dtype)
    hi = lax.bitcast_convert_type(u & jnp.uint32(0xFFFF0000), F32).astype(dtype)
    return lo, hi


def _store_token_major(ref, val):
    n, d = val.shape
    pitch = ref.shape[0] // n
    for s in range(d // LANES):
        ref[pl.ds(s, n, stride=pitch), :] = val[:, s * LANES:(s + 1) * LANES]
    for s in range(d // LANES, pitch):
        ref[pl.ds(s, n, stride=pitch), :] = jnp.zeros((n, LANES), ref.dtype)


def _router_kernel(x_ref, gain_ref, shift_ref, rwt_ref, bias_ref,
                   h_ref, wts_ref, idx_ref, rank_ref, cnt_ref, carry_ref):
    @pl.when(pl.program_id(0) == 0)
    def _():
        carry_ref[...] = jnp.zeros_like(carry_ref)

    h = _normmod(x_ref[...], gain_ref[...], shift_ref[...])
    _store_token_major(h_ref, _pack_bf16_pairs(h))
    tm = h.shape[0]
    per_group = N_EXPERTS // N_GROUPS
    logits = _mmx(rwt_ref[...], h, NT)
    scores = _sigmoid(logits)
    biased = scores + bias_ref[...]

    b3 = biased.reshape(N_GROUPS, per_group, tm)
    pick1, m1 = _first_max_onehot(b3, 1, per_group)
    m2 = jnp.max(jnp.where(pick1, -jnp.inf, b3), axis=1, keepdims=True)
    gscore = (m1 + m2).reshape(N_GROUPS, tm)

    work = gscore
    gsel = jnp.zeros_like(gscore)
    for _ in range(TOPK_GROUPS):
        pick, _m = _first_max_onehot(work, 0, N_GROUPS)
        gsel = jnp.where(pick, 1.0, gsel)
        work = jnp.where(pick, -jnp.inf, work)
    ok = jnp.broadcast_to(gsel.reshape(N_GROUPS, 1, tm), (N_GROUPS, per_group, tm)).reshape(N_EXPERTS, tm)
    work = jnp.where(ok > 0.5, biased, MASKED_SCORE)
    picks = []
    for _ in range(TOP_K):
        pick, _m = _first_max_onehot(work, 0, N_EXPERTS)
        picks.append(pick)
        work = jnp.where(pick, -jnp.inf, work)
    picked = functools.reduce(lambda a, b: a + b, [jnp.where(p, 1.0, 0.0) for p in picks])
    sel = picked * scores
    gates_t = sel / jnp.sum(sel, axis=0, keepdims=True) * ROUTE_SCALE

    ra = lax.broadcasted_iota(jnp.int32, (tm, tm), 0)
    rb = lax.broadcasted_iota(jnp.int32, (tm, tm), 1)
    before = jnp.where(ra < rb, 1.0, 0.0).astype(BF16)
    carry = carry_ref[:, 0:1]
    rank_full = _mm(picked, before) + carry
    e_iota = lax.broadcasted_iota(jnp.int32, (N_EXPERTS, tm), 0).astype(F32)

    def slot_rows(table):
        rows = [jnp.sum(jnp.where(p, table, 0.0), axis=0, keepdims=True) for p in picks]
        return jnp.concatenate(rows, axis=0)

    idx_ref[...] = slot_rows(e_iota).astype(jnp.int32)
    rank_ref[...] = slot_rows(rank_full).astype(jnp.int32)
    wts_ref[...] = slot_rows(gates_t)
    new_carry = carry + jnp.sum(picked, axis=1, keepdims=True)
    carry_ref[...] = jnp.broadcast_to(new_carry, carry_ref.shape)
    cnt_ref[...] = jnp.broadcast_to(new_carry, cnt_ref.shape)


def _router(x, gain, shift, rw_t, bias, tm=256):
    T, D = x.shape
    tm = min(tm, T)
    P = _pitch(D // 2 // LANES)
    return pl.pallas_call(
        _router_kernel,
        grid=(T // tm,),
        in_specs=[pl.BlockSpec((tm, D), lambda i: (i, 0)),
                  pl.BlockSpec((1, D), lambda i: (0, 0)),
                  pl.BlockSpec((1, D), lambda i: (0, 0)),
                  pl.BlockSpec((N_EXPERTS, D), lambda i: (0, 0)),
                  pl.BlockSpec((N_EXPERTS, 1), lambda i: (0, 0))],
        out_specs=[pl.BlockSpec((tm * P, LANES), lambda i: (i, 0)),
                   pl.BlockSpec((TOP_K, tm), lambda i: (0, i)),
                   pl.BlockSpec((TOP_K, tm), lambda i: (0, i)),
                   pl.BlockSpec((TOP_K, tm), lambda i: (0, i)),
                   pl.BlockSpec((N_EXPERTS, LANES), lambda i: (0, 0))],
        out_shape=[jax.ShapeDtypeStruct((T * P, LANES), jnp.int32),
                   jax.ShapeDtypeStruct((TOP_K, T), F32),
                   jax.ShapeDtypeStruct((TOP_K, T), jnp.int32),
                   jax.ShapeDtypeStruct((TOP_K, T), jnp.int32),
                   jax.ShapeDtypeStruct((N_EXPERTS, LANES), F32)],
        scratch_shapes=[pltpu.VMEM((N_EXPERTS, LANES), F32)],
        compiler_params=_cparams(("arbitrary",)),
        name="router",
    )(x, gain, shift, rw_t, bias)


def _dispatch_kernel(P, idx_ref, rank_ref, row_start_ref, pad_start_ref, pad_len_ref, nu_ref, h_ref, xs_hbm,
                     zero_ref, sem, pad_sem):
    i = pl.program_id(0)
    tc = idx_ref.shape[1]
    tile_rows = zero_ref.shape[0]

    def slab(ref, row):
        return ref.at[pl.ds(pl.multiple_of(row * P, math.gcd(P, 8)), P)]

    def token(n, carry):
        src = slab(h_ref, n)
        for j in range(TOP_K):
            dest = row_start_ref[idx_ref[j, n]] + rank_ref[j, n]
            pltpu.make_async_copy(src, slab(xs_hbm, dest), sem).start()
        return carry

    lax.fori_loop(0, tc, token, 0)

    @pl.when(i == 0)
    def _():
        zero_ref[...] = jnp.zeros_like(zero_ref)
        zero_slab = zero_ref.at[pl.ds(0, P)]

        def expert(e, carry):
            def fill(r, c):
                pltpu.make_async_copy(zero_slab, slab(xs_hbm, pad_start_ref[e] + r), pad_sem).start()
                return c

            def drain(r, c):
                pltpu.make_async_copy(zero_slab, slab(xs_hbm, 0), pad_sem).wait()
                return c

            lax.fori_loop(0, pad_len_ref[e], fill, 0)
            lax.fori_loop(0, pad_len_ref[e], drain, 0)
            return carry

        lax.fori_loop(0, N_EXPERTS, expert, 0)

        def tile_dst(t):
            return xs_hbm.at[pl.ds(pl.multiple_of(t * tile_rows, 8), tile_rows)]

        def fill_tile(t, c):
            pltpu.make_async_copy(zero_ref, tile_dst(t), pad_sem).start()
            return c

        def drain_tile(t, c):
            pltpu.make_async_copy(zero_ref, tile_dst(0), pad_sem).wait()
            return c

        n_tiles = xs_hbm.shape[0] // tile_rows
        lax.fori_loop(nu_ref[0], n_tiles, fill_tile, 0)
        lax.fori_loop(nu_ref[0], n_tiles, drain_tile, 0)

    for j in range(TOP_K):
        pltpu.make_async_copy(h_ref, xs_hbm.at[pl.ds(0, tc * P)], sem).wait()


def _dispatch(idx, rank, row_start, pad_start, pad_len, n_used, h_tok, n_tiles, tm_e, tc):
    T = idx.shape[1]
    P = h_tok.shape[0] // T
    smem = pl.BlockSpec(memory_space=pltpu.SMEM)
    slots = pl.BlockSpec((TOP_K, tc), lambda i: (0, i), memory_space=pltpu.SMEM)
    return pl.pallas_call(
        functools.partial(_dispatch_kernel, P),
        grid=(T // tc,),
        in_specs=[slots, slots, smem, smem, smem, smem,
                  pl.BlockSpec((tc * P, LANES), lambda i: (i, 0))],
        out_specs=pl.BlockSpec(memory_space=pl.ANY),
        out_shape=jax.ShapeDtypeStruct((n_tiles * tm_e * P, LANES), h_tok.dtype),
        scratch_shapes=[pltpu.VMEM((tm_e * P, LANES), h_tok.dtype), pltpu.SemaphoreType.DMA,
                        pltpu.SemaphoreType.DMA],
        compiler_params=_cparams(("arbitrary",)),
        name="dispatch",
    )(idx, rank, row_start, pad_start, pad_len, n_used, h_tok)


def _experts_kernel(te_ref, nu_ref, x_ref, wg_ref, wu_ref, wd_ref, o_ref):
    sx = wg_ref.shape[3] // LANES // 2
    px = _pitch(sx)
    tm = x_ref.shape[0] // px
    kx = 8 if sx % 8 == 0 else sx

    @pl.when(pl.program_id(0) >= nu_ref[0])
    def _():
        o_ref[...] = jnp.zeros_like(o_ref)

    @pl.when(pl.program_id(0) < nu_ref[0])
    def _():
        g = u = None
        for c0 in range(0, sx, kx):
            words = jnp.concatenate([x_ref[pl.ds(s, tm, stride=px), :] for s in range(c0, c0 + kx)], axis=1)
            for xk, col0 in zip(_unpack_bf16_pairs(words), (c0, sx + c0)):
                ks = slice(col0 * LANES, (col0 + kx) * LANES)
                gk = lax.dot_general(xk, wg_ref[0, 0, :, ks].astype(BF16), NT, preferred_element_type=F32)
                uk = lax.dot_general(xk, wu_ref[0, 0, :, ks].astype(BF16), NT, preferred_element_type=F32)
                g = gk if g is None else g + gk
                u = uk if u is None else u + uk
        act = (_silu(g) * u).astype(BF16)
        for c0 in range(0, sx, kx):
            halves = []
            for col0 in (c0, sx + c0):
                ns = slice(col0 * LANES, (col0 + kx) * LANES)
                halves.append(jnp.dot(act, wd_ref[0, 0, :, ns].astype(BF16), preferred_element_type=F32))
            words = _pack_bf16_pairs(jnp.concatenate(halves, axis=1))
            for s in range(kx):
                o_ref[pl.ds(c0 + s, tm, stride=px), :] = words[:, s * LANES:(s + 1) * LANES]
        for s in range(sx, px):
            o_ref[pl.ds(s, tm, stride=px), :] = jnp.zeros((tm, LANES), o_ref.dtype)


def _experts(tile_expert, n_used, x_tok, wg, wu, wd, layer, tm):
    _, E, ff, D = wg.shape
    px = _pitch(D // 2 // LANES)
    n_tiles = x_tok.shape[0] // (tm * px)

    def row_map(i, te, nu):
        return (jnp.minimum(i, nu[0] - 1), 0)

    def w_map(i, te, nu):
        return (layer, te[i], 0, 0)

    grid_spec = pltpu.PrefetchScalarGridSpec(
        num_scalar_prefetch=2,
        grid=(n_tiles,),
        in_specs=[pl.BlockSpec((tm * px, LANES), row_map),
                  pl.BlockSpec((1, 1, ff, D), w_map),
                  pl.BlockSpec((1, 1, ff, D), w_map),
                  pl.BlockSpec((1, 1, ff, D), w_map)],
        out_specs=pl.BlockSpec((tm * px, LANES), lambda i, te, nu: (i, 0)),
    )
    return pl.pallas_call(
        _experts_kernel,
        grid_spec=grid_spec,
        out_shape=jax.ShapeDtypeStruct(x_tok.shape, jnp.int32),
        compiler_params=_cparams(("arbitrary",)),
        name="experts",
    )(tile_expert, n_used, x_tok, wg, wu, wd)


def _combine_kernel(has_norm, idx_ref, rank_ref, row_start_ref, w_ref, ysh_ref, x_ref, gate2_ref, *rest):
    if has_norm:
        norm_gain_ref, y_hbm, o_ref, buf_ref, acc_ref, sem = rest
    else:
        y_hbm, o_ref, buf_ref, acc_ref, sem = rest
    tc, D = x_ref.shape
    S = D // LANES
    P = _pitch(S)
    sy = S // 2
    py = _pitch(sy)
    align = math.gcd(py, 8)
    slot_rows = tc * py

    def token(n, carry):
        for j in range(TOP_K):
            dest = row_start_ref[idx_ref[j, n]] + rank_ref[j, n]
            src = y_hbm.at[pl.ds(pl.multiple_of(dest * py, align), sy)]
            dst = buf_ref.at[pl.ds(pl.multiple_of(j * slot_rows + n * py, align), sy)]
            pltpu.make_async_copy(src, dst, sem).start()
        return carry

    lax.fori_loop(0, tc, token, 0)
    n_rows = TOP_K * tc * sy
    pltpu.make_async_copy(y_hbm.at[pl.ds(0, n_rows)], buf_ref.at[pl.ds(0, n_rows)], sem).wait()

    def accumulate(n, carry):
        lo, hi = _unpack_bf16_pairs(ysh_ref[pl.ds(pl.multiple_of(n * py, align), sy), :], F32)
        for j in range(TOP_K):
            words = buf_ref[pl.ds(pl.multiple_of(j * slot_rows + n * py, align), sy), :]
            slab_lo, slab_hi = _unpack_bf16_pairs(words, F32)
            lo = lo + w_ref[j, n] * slab_lo
            hi = hi + w_ref[j, n] * slab_hi
        row = pl.multiple_of(n * P, math.gcd(P, 8))
        acc_ref[pl.ds(row, sy), :] = lo
        acc_ref[pl.ds(row + sy, sy), :] = hi
        return carry

    lax.fori_loop(0, tc, accumulate, 0, unroll=2)
    for s in range(S):
        ls = slice(s * LANES, (s + 1) * LANES)
        o_ref[:, ls] = x_ref[:, ls] + gate2_ref[:, ls] * acc_ref[pl.ds(s, tc, stride=P), :]
    if has_norm:
        xo = o_ref[...]
        o_ref[...] = xo * lax.rsqrt(jnp.mean(xo * xo, axis=-1, keepdims=True) + NORM_EPS) * norm_gain_ref[...]


def _combine(idx, rank, row_start, wts, y_shared, x, gate2, y_sorted, norm_gain=None, tc=128):
    T, D = x.shape
    tc = min(tc, T)
    P = _pitch(D // LANES)
    py = _pitch(D // 2 // LANES)
    slots = pl.BlockSpec((TOP_K, tc), lambda i: (0, i), memory_space=pltpu.SMEM)
    vec = pl.BlockSpec((1, D), lambda i: (0, 0))
    in_specs = [slots, slots, pl.BlockSpec(memory_space=pltpu.SMEM), slots,
                pl.BlockSpec((tc * py, LANES), lambda i: (i, 0)),
                pl.BlockSpec((tc, D), lambda i: (i, 0)), vec]
    args = [idx, rank, row_start, wts, y_shared, x, gate2]
    if norm_gain is not None:
        in_specs.append(vec)
        args.append(norm_gain)
    in_specs.append(pl.BlockSpec(memory_space=pl.ANY))
    args.append(y_sorted)
    return pl.pallas_call(
        functools.partial(_combine_kernel, norm_gain is not None),
        grid=(T // tc,),
        in_specs=in_specs,
        out_specs=pl.BlockSpec((tc, D), lambda i: (i, 0)),
        out_shape=jax.ShapeDtypeStruct((T, D), F32),
        scratch_shapes=[pltpu.VMEM((TOP_K * tc * py, LANES), jnp.int32), pltpu.VMEM((tc * P, LANES), F32),
                        pltpu.SemaphoreType.DMA],
        compiler_params=_cparams(("arbitrary",)),
        name="combine",
    )(*args)


def _moe(x, gain, shift, gate2, rw_t, bias, wg, wu, wd, swg, swu, swd, layer, norm_gain=None, tm_e=256,
         tc_d=256):
    T, D = x.shape
    tm_e = min(tm_e, T)
    tc_d = min(tc_d, T)
    h_tok, wts, idx, rank, cnt = _router(x, gain, shift, rw_t, bias)
    counts = cnt[:, 0].astype(jnp.int32)
    tiles = (counts + tm_e - 1) // tm_e
    tiles_cum = jnp.cumsum(tiles)
    row_start = (tiles_cum - tiles) * tm_e
    n_tiles = (T * TOP_K) // tm_e + N_EXPERTS
    tile_ids = jnp.arange(n_tiles, dtype=jnp.int32)
    tile_expert = jnp.minimum(jnp.sum((tiles_cum[None, :] <= tile_ids[:, None]).astype(jnp.int32), axis=1),
                              N_EXPERTS - 1)
    n_used = tiles_cum[-1:].astype(jnp.int32)
    x_sorted = _dispatch(idx, rank, row_start, row_start + counts, tiles * tm_e - counts, n_used, h_tok,
                         n_tiles, tm_e, tc_d)
    y_sorted = _experts(tile_expert, n_used, x_sorted, wg, wu, wd, layer, tm_e)
    n_sh = T // tm_e
    y_shared = _experts(jnp.zeros((n_sh,), jnp.int32), jnp.full((1,), n_sh, jnp.int32), h_tok,
                        swg[:, None], swu[:, None], swd[:, None], layer, tm_e)
    return _combine(idx, rank, row_start, wts, y_shared, x, gate2, y_sorted, norm_gain)


def _pad_cols(a, n):
    return jnp.pad(a, ((0, 0), (0, n - a.shape[1])))


def _pad_rows(a, n):
    return jnp.pad(a, ((0, n - a.shape[0]), (0, 0)))


def _pad_rwkv_cols(a, W, w_lora, a_lora):
    c0 = 3 * W
    return jnp.concatenate([a[:, :c0], _pad_cols(a[:, c0:c0 + w_lora], LORA_PAD),
                            _pad_cols(a[:, c0 + w_lora:c0 + w_lora + a_lora], LORA_PAD),
                            a[:, c0 + w_lora + a_lora:]], axis=1)


def kernel(x, c, w_mod, b_mod, norm_mix, norm_ffn, w_in, rwkv_mu, rwkv_w0, rwkv_w_up, rwkv_a0, rwkv_a_up, rwkv_g_up, rwkv_k_k, rwkv_k_a, rwkv_r_k, rwkv_gn_w, rwkv_gn_b, rwkv_v0, rwkv_v_down, rwkv_v_up, hgrn_lower_bounds, hgrn_norm, w_out, router_w, router_bias, expert_w_gate, expert_w_up, expert_w_down, shared_w_gate, shared_w_up, shared_w_down, final_norm):
    B, T, D = x.shape
    L = w_mod.shape[0]
    W = rwkv_w0.shape[1]
    w_lora, a_lora = rwkv_w_up.shape[1], rwkv_a_up.shape[1]
    rwkv_cols = rwkv_mu.shape[1]
    rwkv_cols_p = 3 * W + 2 * LORA_PAD + G_LORA

    lb_soft = jax.nn.softmax(hgrn_lower_bounds.astype(F32), axis=0)
    lbs = jnp.cumsum(lb_soft, axis=0) - lb_soft[0]
    mod = _modulation(c, w_mod, b_mod)

    xs = x.reshape(B * T, D)
    v_first = None
    for l in range(L):
        sh1, sc1, g1, sh2, sc2, g2 = [mod[l, n * D:(n + 1) * D].reshape(1, D) for n in range(6)]
        w_in_p = _pad_rwkv_cols(w_in[l], W, w_lora, a_lora).T.astype(BF16)
        p = _norm_proj(xs, norm_mix[l].reshape(1, D) * (1.0 + sc1), sh1, w_in_p)
        mu_p = _pad_rwkv_cols(rwkv_mu[l].reshape(1, -1), W, w_lora, a_lora)
        vres = None
        if l > 0:
            vres = (rwkv_v0[l - 1].reshape(1, W), _pad_cols(rwkv_v_down[l - 1], V_LORA_PAD).astype(BF16),
                    _pad_rows(rwkv_v_up[l - 1], V_LORA_PAD).astype(BF16), v_first)
        r, lw, k, v, kk, kka, g = _rwkv_prep(
            p, mu_p, rwkv_w0[l].reshape(1, W), _pad_rows(rwkv_w_up[l], LORA_PAD).astype(BF16),
            rwkv_a0[l].reshape(1, W), _pad_rows(rwkv_a_up[l], LORA_PAD).astype(BF16),
            rwkv_g_up[l].astype(BF16), rwkv_k_k[l].reshape(1, W), rwkv_k_a[l].reshape(1, W), vres)
        if l == 0:
            v_first = v
        y_r = _rwkv_recurrence(r, lw, k, v, kk, kka, g, rwkv_r_k[l].reshape(1, W),
                               rwkv_gn_w[l].reshape(1, W), rwkv_gn_b[l].reshape(1, W))
        y_h = _hgrn(p, rwkv_cols_p, lbs[l].reshape(1, -1), hgrn_norm[l].reshape(1, -1))
        xs = _out_proj(y_r, y_h, w_out[l].astype(BF16), xs, g1)
        xs = _moe(xs, norm_ffn[l].reshape(1, D) * (1.0 + sc2), sh2, g2, router_w[l].T,
                  router_bias[l].reshape(-1, 1), jnp.swapaxes(expert_w_gate, 2, 3),
                  jnp.swapaxes(expert_w_up, 2, 3), expert_w_down, jnp.swapaxes(shared_w_gate, 1, 2),
                  jnp.swapaxes(shared_w_up, 1, 2), shared_w_down, l,
                  final_norm.reshape(1, D) if l == L - 1 else None)
    return xs.reshape(B, T, D)
```

```python
import functools
import math

import jax
import jax.numpy as jnp
from jax import lax
from jax.experimental import pallas as pl
from jax.experimental.pallas import tpu as pltpu

F32 = jnp.float32
BF16 = jnp.bfloat16

LANES = 128
VMEM_LIMIT = 56 * 1024 * 1024

RWKV_HEAD = 64
HGRN_HEAD = 128
CHUNK = 64
SUB = 16
LORA_PAD = 128
G_LORA = 256
V_LORA_PAD = 128
N_EXPERTS = 64
N_GROUPS = 8
TOPK_GROUPS = 4
TOP_K = 8
ROUTE_SCALE = 2.5
MASKED_SCORE = -1e4
MIN_FORGET = 1e-30
NORM_EPS = 1e-6
GN_EPS = 64e-5
DECAY_SCALE = 0.6065306597126334

NN = (((1,), (0,)), ((), ()))
NT = (((1,), (1,)), ((), ()))
TN = (((0,), (0,)), ((), ()))


def _mm(a, b, dims=NN):
    return lax.dot_general(a.astype(BF16), b.astype(BF16), dims, preferred_element_type=F32)


def _mmx(a, b, dims=NN):
    return lax.dot_general(a.astype(F32), b.astype(F32), dims, preferred_element_type=F32,
                           precision=lax.Precision.HIGHEST)


def _split3(x):
    hi = x.astype(BF16)
    r1 = x - hi.astype(F32)
    mid = r1.astype(BF16)
    lo = (r1 - mid.astype(F32)).astype(BF16)
    return hi, mid, lo


def _mm_left01(a01, b, dims=NN):
    a = a01.astype(BF16)
    d = functools.partial(lax.dot_general, dimension_numbers=dims, preferred_element_type=F32)
    hi, mid, lo = _split3(b)
    return d(a, hi) + d(a, mid) + d(a, lo)


def _mm_right01(a, b01, dims=NN):
    b = b01.astype(BF16)
    d = functools.partial(lax.dot_general, dimension_numbers=dims, preferred_element_type=F32)
    hi, mid, lo = _split3(a)
    return d(hi, b) + d(mid, b) + d(lo, b)


def _sigmoid(x):
    return 1.0 / (1.0 + jnp.exp(-x))


def _silu(x):
    return x * _sigmoid(x)


def _cparams(sem):
    return pltpu.CompilerParams(dimension_semantics=sem, vmem_limit_bytes=VMEM_LIMIT)


def _mod_kernel(c_ref, w_ref, b_ref, o_ref):
    cond = _silu(c_ref[...])
    tn = w_ref.shape[2]
    cols = [jnp.sum(w_ref[0, :, j * LANES:(j + 1) * LANES] * cond, axis=0, keepdims=True)
            for j in range(tn // LANES)]
    o_ref[0] = jnp.concatenate(cols, axis=1) + b_ref[0]


def _modulation(c, w_mod, b_mod):
    L, D, N = w_mod.shape
    tn = 512
    c_lanes = jnp.broadcast_to(c.reshape(D, 1), (D, LANES))
    out = pl.pallas_call(
        _mod_kernel,
        grid=(L, N // tn),
        in_specs=[pl.BlockSpec((D, LANES), lambda l, j: (0, 0)),
                  pl.BlockSpec((1, D, tn), lambda l, j: (l, 0, j)),
                  pl.BlockSpec((1, 1, tn), lambda l, j: (l, 0, j))],
        out_specs=pl.BlockSpec((1, 1, tn), lambda l, j: (l, 0, j)),
        out_shape=jax.ShapeDtypeStruct((L, 1, N), F32),
        compiler_params=_cparams(("arbitrary", "arbitrary")),
        name="modulation",
    )(c_lanes, w_mod, b_mod.reshape(L, 1, N))
    return out[:, 0, :]


def _normmod(x, gain, shift):
    y = x * lax.rsqrt(jnp.mean(x * x, axis=-1, keepdims=True) + NORM_EPS)
    return y * gain + shift


def _norm_proj_kernel(x_ref, gain_ref, shift_ref, w_ref, o_ref, h_ref):
    @pl.when(pl.program_id(1) == 0)
    def _():
        h_ref[...] = _normmod(x_ref[...], gain_ref[...], shift_ref[...]).astype(BF16)

    o_ref[...] = lax.dot_general(h_ref[...], w_ref[...], NT, preferred_element_type=F32)


def _norm_proj(x, gain, shift, w_t, tm=1024, tn=512):
    T, D = x.shape
    N = w_t.shape[0]
    return pl.pallas_call(
        _norm_proj_kernel,
        grid=(T // tm, N // tn),
        in_specs=[pl.BlockSpec((tm, D), lambda i, j: (i, 0), pipeline_mode=pl.Buffered(1)),
                  pl.BlockSpec((1, D), lambda i, j: (0, 0)),
                  pl.BlockSpec((1, D), lambda i, j: (0, 0)),
                  pl.BlockSpec((tn, D), lambda i, j: (j, 0))],
        out_specs=pl.BlockSpec((tm, tn), lambda i, j: (i, j)),
        out_shape=jax.ShapeDtypeStruct((T, N), F32),
        scratch_shapes=[pltpu.VMEM((tm, D), BF16)],
        compiler_params=_cparams(("arbitrary", "arbitrary")),
        name="norm_proj",
    )(x, gain, shift, w_t)


def _pair_ones():
    r = lax.broadcasted_iota(jnp.int32, (LANES, LANES), 0) // RWKV_HEAD
    c = lax.broadcasted_iota(jnp.int32, (LANES, LANES), 1) // RWKV_HEAD
    return jnp.where(r == c, 1.0, 0.0).astype(F32)


def _rwkv_prep_kernel(has_vres, *refs):
    n_in = 20 if has_vres else 16
    (r_ref, k_ref, v_ref, lo_ref, rp_ref, kp_ref, vp_ref, lop_ref, mu_ref, w0_ref, wup_ref,
     a0_ref, aup_ref, gup_ref, kk_ref, ka_ref) = refs[:16]
    if has_vres:
        v0_ref, vdn_ref, vup_ref, vf_ref = refs[16:20]
    ro_ref, lwo_ref, ko_ref, vo_ref, kko_ref, kkao_ref, go_ref = refs[n_in:]
    first = pl.program_id(0) == 0
    W = r_ref.shape[1]

    def shift(cur_ref, prev_ref, mu):
        cur = cur_ref[...]
        prev_last = jnp.where(first, 0.0, prev_ref[7:8, :])
        rolled = pltpu.roll(cur, 1, axis=0)
        row = lax.broadcasted_iota(jnp.int32, cur.shape, 0)
        prev = jnp.where(row == 0, prev_last, rolled)
        return cur + (prev - cur) * mu

    r = shift(r_ref, rp_ref, mu_ref[:, 0:W])
    k = shift(k_ref, kp_ref, mu_ref[:, W:2 * W])
    v = shift(v_ref, vp_ref, mu_ref[:, 2 * W:3 * W])
    lo = shift(lo_ref, lop_ref, mu_ref[:, 3 * W:])
    w_lo = lo[:, 0:LORA_PAD]
    a_lo = lo[:, LORA_PAD:2 * LORA_PAD]
    g_lo = lo[:, 2 * LORA_PAD:]

    z = w0_ref[...] + _mm(jnp.tanh(w_lo), wup_ref[...])
    lw = -DECAY_SCALE * _sigmoid(z)
    a = _sigmoid(a0_ref[...] + _mm(a_lo, aup_ref[...]))
    g = _mm(_sigmoid(g_lo), gup_ref[...])
    if has_vres:
        mix = _sigmoid(v0_ref[...] + _mm(_mm(v, vdn_ref[...]), vup_ref[...]))
        v = v + (vf_ref[...] - v) * mix

    kkr = k * kk_ref[...]
    sq = kkr * kkr
    ones = _pair_ones()
    ss = jnp.concatenate(
        [_mm_right01(sq[:, j * LANES:(j + 1) * LANES], ones) for j in range(W // LANES)], axis=1)
    kk = kkr / jnp.maximum(jnp.sqrt(ss), 1e-12)
    k2 = k * (1.0 + (a - 1.0) * ka_ref[...])

    ro_ref[...] = r
    lwo_ref[...] = lw
    ko_ref[...] = k2
    vo_ref[...] = v
    kko_ref[...] = kk
    kkao_ref[...] = kk * a
    go_ref[...] = g


def _rwkv_prep(p, mu_p, w0, wup, a0, aup, gup, k_k, k_a, vres, tp=128):
    T = p.shape[0]
    W = w0.shape[1]
    lo_w = 2 * LORA_PAD + G_LORA
    nb = tp // 8
    lo_blk = 3 * W // lo_w

    def cur(cb, width):
        return pl.BlockSpec((tp, width), lambda i, cb=cb: (i, cb))

    def prev(cb, width):
        return pl.BlockSpec((8, width), lambda i, cb=cb: (jnp.maximum(i * nb - 1, 0), cb))

    def vec(n):
        return pl.BlockSpec((1, n), lambda i: (0, 0))

    def mat(a, b):
        return pl.BlockSpec((a, b), lambda i: (0, 0))

    in_specs = [cur(0, W), cur(1, W), cur(2, W), cur(lo_blk, lo_w),
                prev(0, W), prev(1, W), prev(2, W), prev(lo_blk, lo_w),
                vec(3 * W + lo_w), vec(W), mat(LORA_PAD, W), vec(W), mat(LORA_PAD, W),
                mat(G_LORA, W), vec(W), vec(W)]
    args = [p, p, p, p, p, p, p, p, mu_p, w0, wup, a0, aup, gup, k_k, k_a]
    if vres is not None:
        v0, vdn, vup, v_first = vres
        in_specs += [vec(W), mat(W, V_LORA_PAD), mat(V_LORA_PAD, W),
                     pl.BlockSpec((tp, W), lambda i: (i, 0))]
        args += [v0, vdn, vup, v_first]
    out_spec = pl.BlockSpec((tp, W), lambda i: (i, 0))
    sds = jax.ShapeDtypeStruct((T, W), F32)
    return pl.pallas_call(
        functools.partial(_rwkv_prep_kernel, vres is not None),
        grid=(T // tp,),
        in_specs=in_specs,
        out_specs=[out_spec] * 7,
        out_shape=[sds] * 7,
        compiler_params=_cparams(("arbitrary",)),
        name="rwkv_prep",
    )(*args)


def _tri_incl(n):
    row = lax.broadcasted_iota(jnp.int32, (n, n), 0)
    col = lax.broadcasted_iota(jnp.int32, (n, n), 1)
    return jnp.where(col <= row, 1.0, 0.0).astype(BF16)


def _rwkv_pair(r, c, lw, k, v, kk, kka, g, rk, gnw, gnb, st):
    C = r.shape[0]
    C2 = 2 * C
    cp = c - lw
    c_last = c[C - 1:C, :]
    e_c, e_cp, e_nc, e_dl = jnp.exp(c), jnp.exp(cp), jnp.exp(-c), jnp.exp(c_last - c)

    lane = lax.broadcasted_iota(jnp.int32, (C, LANES), 1)
    m0 = lane < RWKV_HEAD

    def stack(x):
        return jnp.concatenate([jnp.where(m0, x, 0.0), jnp.where(m0, 0.0, x)], axis=0)

    aq_rt = jnp.concatenate([stack(kk * e_cp), stack(r * e_c)], axis=0).astype(BF16)
    bt_kt = jnp.concatenate([stack(kka * e_nc), stack(k * e_nc)], axis=0).astype(BF16)
    kw_bw = jnp.concatenate([stack(k * e_dl), stack(kka * e_dl)], axis=0).astype(BF16)
    vs = stack(v)
    vs_b = vs.astype(BF16)

    row = lax.broadcasted_iota(jnp.int32, (C2, C2), 0)
    col = lax.broadcasted_iota(jnp.int32, (C2, C2), 1)
    t_i, s_j = row % C, col % C
    strict, incl, eye = s_j < t_i, s_j <= t_i, row == col

    yield
    p = _mm(aq_rt, bt_kt, NT)
    yield
    l_ab = jnp.where(strict, p[:C2, :C2], 0.0)
    l_ak = jnp.where(strict, p[:C2, C2:], 0.0)
    l_rb = jnp.where(incl, p[C2:, :C2], 0.0)
    l_rk = jnp.where(incl, p[C2:, C2:], 0.0)

    pw = -l_ab
    inv = jnp.where(eye, 1.0, 0.0) + pw
    n = 2
    while n < C:
        pw = _mm(pw, pw)
        yield
        inv = inv + _mm(inv, pw)
        n *= 2

    xy = _mm(aq_rt, st) + _mm(jnp.concatenate([l_ak, l_rk], axis=0), vs_b)
    yield
    u = _mm(inv, xy[:C2])
    yield
    y = xy[C2:] - _mm(l_rb, u)
    w_col = jnp.broadcast_to(jnp.exp(c_last), (C2, C2)).T
    st_new = st * w_col + _mm(kw_bw, jnp.concatenate([vs, -u], axis=0), TN)
    yield

    mh = (row // C) == (col // RWKV_HEAD)
    inv_n = 1.0 / RWKV_HEAD
    mean = jnp.sum(y, axis=1, keepdims=True) * inv_n
    yc = jnp.where(mh, y - mean, 0.0)
    var = jnp.sum(yc * yc, axis=1, keepdims=True) * inv_n
    yn = yc * lax.rsqrt(var + GN_EPS)
    bonus = jnp.sum(stack(r * k * rk), axis=1, keepdims=True) * vs
    yn_t = yn[:C] + yn[C:]
    bonus_t = bonus[:C] + bonus[C:]
    return (yn_t * gnw + gnb + bonus_t) * g, st_new


def _interleave(gens):
    results = [None] * len(gens)
    live = list(enumerate(gens))
    while live:
        still = []
        for j, gen in live:
            try:
                next(gen)
                still.append((j, gen))
            except StopIteration as stop:
                results[j] = stop.value
        live = still
    return results


def _rwkv_rec_kernel(r_ref, lw_ref, k_ref, v_ref, kk_ref, kka_ref, g_ref, rk_ref, gnw_ref, gnb_ref,
                     o_ref, st_ref):
    @pl.when(pl.program_id(1) == 0)
    def _():
        st_ref[...] = jnp.zeros_like(st_ref)

    rows, lanes = r_ref.shape
    tri = _tri_incl(CHUNK)

    def chunk(ci, carry):
        rs = pl.ds(pl.multiple_of(ci * CHUNK, CHUNK), CHUNK)
        lw_all = lw_ref[rs, :]
        c_all = _mm_left01(tri, lw_all)
        r, k, v, kk, kka, g = (ref[rs, :] for ref in (r_ref, k_ref, v_ref, kk_ref, kka_ref, g_ref))
        rk, gnw, gnb = rk_ref[...], gnw_ref[...], gnb_ref[...]
        gens = []
        for j in range(lanes // LANES):
            sl = slice(j * LANES, (j + 1) * LANES)
            gens.append(_rwkv_pair(r[:, sl], c_all[:, sl], lw_all[:, sl], k[:, sl], v[:, sl], kk[:, sl],
                                   kka[:, sl], g[:, sl], rk[:, sl], gnw[:, sl], gnb[:, sl], st_ref[j]))
        results = _interleave(gens)
        for j, (_, st_new) in enumerate(results):
            st_ref[j] = st_new
        o_ref[rs, :] = jnp.concatenate([out for out, _ in results], axis=1).astype(o_ref.dtype)
        return carry

    lax.fori_loop(0, rows // CHUNK, chunk, 0)


def _rwkv_recurrence(r, lw, k, v, kk, kka, g, rk, gnw, gnb, hb_lanes=1024, rows=512):
    T, W = r.shape
    rows = min(rows, T)
    seq = pl.BlockSpec((rows, hb_lanes), lambda h, t: (t, h))
    vec = pl.BlockSpec((1, hb_lanes), lambda h, t: (0, h))
    return pl.pallas_call(
        _rwkv_rec_kernel,
        grid=(W // hb_lanes, T // rows),
        in_specs=[seq] * 7 + [vec] * 3,
        out_specs=seq,
        out_shape=jax.ShapeDtypeStruct((T, W), BF16),
        scratch_shapes=[pltpu.VMEM((hb_lanes // LANES, LANES, LANES), F32)],
        compiler_params=_cparams(("arbitrary", "arbitrary")),
        name="rwkv_recurrence",
    )(r, lw, k, v, kk, kka, g, rk, gnw, gnb)


def _hgrn_head(q_raw, f_raw, i_c, g_raw, lb, gain, tri, s):
    C = q_raw.shape[0]
    q = _silu(q_raw)
    forget = lb + (1.0 - lb) * _sigmoid(f_raw)
    lf = jnp.log(jnp.maximum(forget, MIN_FORGET))
    kin = (1.0 - lb) * _sigmoid(-f_raw)
    b = _mm_left01(tri, lf)
    yield
    o = _mm(q * jnp.exp(b), s)

    lane_s = lax.broadcasted_iota(jnp.int32, (SUB, C), 1)
    row_s = lax.broadcasted_iota(jnp.int32, (SUB, C), 0)
    score_rows = []
    for i in range(C // SUB):
        lo, hi = i * SUB, (i + 1) * SUB
        b_i, q_i = b[lo:hi], q[lo:hi]
        blk = jnp.zeros((SUB, C), F32)
        for sidx in range(SUB):
            srow = lo + sidx
            d = jnp.exp(b_i - b[srow:srow + 1]) * q_i * kin[srow:srow + 1]
            blk = jnp.where(lane_s == srow, jnp.sum(d, axis=1, keepdims=True), blk)
        blk = jnp.where(lane_s <= row_s + lo, blk, 0.0)
        yield
        if i > 0:
            b_st = b[lo - 1:lo]
            qs = q_i * jnp.exp(b_i - b_st)
            ks = kin * jnp.exp(jnp.minimum(b_st - b, 0.0))
            blk = blk + jnp.where(lane_s < lo, _mm(qs, ks, NT), 0.0)
        score_rows.append(blk)
    scores = jnp.concatenate(score_rows, axis=0)
    o = o + _mm(scores, i_c)

    b_last = b[C - 1:C]
    f_col = jnp.broadcast_to(jnp.exp(b_last), (LANES, LANES)).T
    s_new = s * f_col + _mm(kin * jnp.exp(b_last - b), i_c, TN)

    o = o * lax.rsqrt(jnp.mean(o * o, axis=1, keepdims=True) + NORM_EPS)
    return o * gain * _sigmoid(g_raw), s_new


def _hgrn_kernel(q_ref, f_ref, i_ref, g_ref, lb_ref, gain_ref, o_ref, s_ref):
    @pl.when(pl.program_id(1) == 0)
    def _():
        s_ref[...] = jnp.zeros_like(s_ref)

    rows, lanes = q_ref.shape
    tri = _tri_incl(CHUNK)

    def chunk(ci, carry):
        rs = pl.ds(pl.multiple_of(ci * CHUNK, CHUNK), CHUNK)
        q, f, i_c, g = (ref[rs, :] for ref in (q_ref, f_ref, i_ref, g_ref))
        lb, gain = lb_ref[...], gain_ref[...]
        gens = []
        for j in range(lanes // LANES):
            sl = slice(j * LANES, (j + 1) * LANES)
            gens.append(_hgrn_head(q[:, sl], f[:, sl], i_c[:, sl], g[:, sl], lb[:, sl], gain[:, sl],
                                   tri, s_ref[j]))
        results = _interleave(gens)
        for j, (_, s_new) in enumerate(results):
            s_ref[j] = s_new
        o_ref[rs, :] = jnp.concatenate([out for out, _ in results], axis=1).astype(o_ref.dtype)
        return carry

    lax.fori_loop(0, rows // CHUNK, chunk, 0)


def _hgrn(p, col0, lb, gain, hb_lanes=512, rows=512):
    T = p.shape[0]
    W = lb.shape[1]
    rows = min(rows, T)
    nb = W // hb_lanes
    b0 = col0 // hb_lanes

    def sec(n):
        return pl.BlockSpec((rows, hb_lanes), lambda h, t, n=n: (t, b0 + n * nb + h))

    vec = pl.BlockSpec((1, hb_lanes), lambda h, t: (0, h))
    return pl.pallas_call(
        _hgrn_kernel,
        grid=(nb, T // rows),
        in_specs=[sec(0), sec(1), sec(2), sec(3), vec, vec],
        out_specs=pl.BlockSpec((rows, hb_lanes), lambda h, t: (t, h)),
        out_shape=jax.ShapeDtypeStruct((T, W), BF16),
        scratch_shapes=[pltpu.VMEM((hb_lanes // LANES, LANES, LANES), F32)],
        compiler_params=_cparams(("arbitrary", "arbitrary")),
        name="hgrn2",
    )(p, p, p, p, lb, gain)


def _out_proj_kernel(ya_ref, yb_ref, wa_ref, wb_ref, x_ref, gate_ref, o_ref):
    acc = jnp.dot(ya_ref[...], wa_ref[...], preferred_element_type=F32)
    acc = acc + jnp.dot(yb_ref[...], wb_ref[...], preferred_element_type=F32)
    o_ref[...] = x_ref[...] + gate_ref[...] * acc


def _out_proj(ya, yb, w, x, gate, tm=512, tn=1024):
    T, Ka = ya.shape
    Kb = yb.shape[1]
    D = w.shape[1]
    return pl.pallas_call(
        _out_proj_kernel,
        grid=(T // tm, D // tn),
        in_specs=[pl.BlockSpec((tm, Ka), lambda i, j: (i, 0)),
                  pl.BlockSpec((tm, Kb), lambda i, j: (i, 0)),
                  pl.BlockSpec((Ka, tn), lambda i, j: (0, j)),
                  pl.BlockSpec((Kb, tn), lambda i, j: (Ka // Kb, j)),
                  pl.BlockSpec((tm, tn), lambda i, j: (i, j)),
                  pl.BlockSpec((1, tn), lambda i, j: (0, j))],
        out_specs=pl.BlockSpec((tm, tn), lambda i, j: (i, j)),
        out_shape=jax.ShapeDtypeStruct((T, D), F32),
        compiler_params=_cparams(("arbitrary", "arbitrary")),
        name="out_proj",
    )(ya, yb, w, w, x, gate)


def _first_max_onehot(work, axis, size):
    m = jnp.max(work, axis=axis, keepdims=True)
    idx = lax.broadcasted_iota(jnp.int32, work.shape, axis)
    first = jnp.min(jnp.where(work == m, idx, size), axis=axis, keepdims=True)
    return idx == first, m


def _pitch(s_per):
    return s_per + 8 if s_per % 16 == 0 else s_per


def _pack_bf16_pairs(h):
    half = h.shape[1] // 2
    bits = lambda v: lax.bitcast_convert_type(v.astype(BF16).astype(F32), jnp.uint32)
    word = (bits(h[:, :half]) >> 16) | (bits(h[:, half:]) & jnp.uint32(0xFFFF0000))
    return lax.bitcast_convert_type(word, jnp.int32)


def _unpack_bf16_pairs(word, dtype=BF16):
    u = lax.bitcast_convert_type(word, jnp.uint32)
    lo = lax.bitcast_convert_type(u << 16, F32).astype(dtype)
    hi = lax.bitcast_convert_type(u & jnp.uint32(0xFFFF0000), F32).astype(dtype)
    return lo, hi


def _store_token_major(ref, val):
    n, d = val.shape
    pitch = ref.shape[0] // n
    for s in range(d // LANES):
        ref[pl.ds(s, n, stride=pitch), :] = val[:, s * LANES:(s + 1) * LANES]
    for s in range(d // LANES, pitch):
        ref[pl.ds(s, n, stride=pitch), :] = jnp.zeros((n, LANES), ref.dtype)


def _router_kernel(x_ref, gain_ref, shift_ref, rwt_ref, bias_ref,
                   h_ref, wts_ref, idx_ref, rank_ref, cnt_ref, carry_ref):
    @pl.when(pl.program_id(0) == 0)
    def _():
        carry_ref[...] = jnp.zeros_like(carry_ref)

    h = _normmod(x_ref[...], gain_ref[...], shift_ref[...])
    _store_token_major(h_ref, _pack_bf16_pairs(h))
    tm = h.shape[0]
    per_group = N_EXPERTS // N_GROUPS
    logits = _mmx(rwt_ref[...], h, NT)
    scores = _sigmoid(logits)
    biased = scores + bias_ref[...]

    b3 = biased.reshape(N_GROUPS, per_group, tm)
    pick1, m1 = _first_max_onehot(b3, 1, per_group)
    m2 = jnp.max(jnp.where(pick1, -jnp.inf, b3), axis=1, keepdims=True)
    gscore = (m1 + m2).reshape(N_GROUPS, tm)

    work = gscore
    gsel = jnp.zeros_like(gscore)
    for _ in range(TOPK_GROUPS):
        pick, _m = _first_max_onehot(work, 0, N_GROUPS)
        gsel = jnp.where(pick, 1.0, gsel)
        work = jnp.where(pick, -jnp.inf, work)
    ok = jnp.broadcast_to(gsel.reshape(N_GROUPS, 1, tm), (N_GROUPS, per_group, tm)).reshape(N_EXPERTS, tm)
    work = jnp.where(ok > 0.5, biased, MASKED_SCORE)
    picks = []
    for _ in range(TOP_K):
        pick, _m = _first_max_onehot(work, 0, N_EXPERTS)
        picks.append(pick)
        work = jnp.where(pick, -jnp.inf, work)
    picked = functools.reduce(lambda a, b: a + b, [jnp.where(p, 1.0, 0.0) for p in picks])
    sel = picked * scores
    gates_t = sel / jnp.sum(sel, axis=0, keepdims=True) * ROUTE_SCALE

    ra = lax.broadcasted_iota(jnp.int32, (tm, tm), 0)
    rb = lax.broadcasted_iota(jnp.int32, (tm, tm), 1)
    before = jnp.where(ra < rb, 1.0, 0.0).astype(BF16)
    carry = carry_ref[:, 0:1]
    rank_full = _mm(picked, before) + carry
    e_iota = lax.broadcasted_iota(jnp.int32, (N_EXPERTS, tm), 0).astype(F32)

    def slot_rows(table):
        rows = [jnp.sum(jnp.where(p, table, 0.0), axis=0, keepdims=True) for p in picks]
        return jnp.concatenate(rows, axis=0)

    idx_ref[...] = slot_rows(e_iota).astype(jnp.int32)
    rank_ref[...] = slot_rows(rank_full).astype(jnp.int32)
    wts_ref[...] = slot_rows(gates_t)
    new_carry = carry + jnp.sum(picked, axis=1, keepdims=True)
    carry_ref[...] = jnp.broadcast_to(new_carry, carry_ref.shape)
    cnt_ref[...] = jnp.broadcast_to(new_carry, cnt_ref.shape)


def _router(x, gain, shift, rw_t, bias, tm=256):
    T, D = x.shape
    tm = min(tm, T)
    P = _pitch(D // 2 // LANES)
    return pl.pallas_call(
        _router_kernel,
        grid=(T // tm,),
        in_specs=[pl.BlockSpec((tm, D), lambda i: (i, 0)),
                  pl.BlockSpec((1, D), lambda i: (0, 0)),
                  pl.BlockSpec((1, D), lambda i: (0, 0)),
                  pl.BlockSpec((N_EXPERTS, D), lambda i: (0, 0)),
                  pl.BlockSpec((N_EXPERTS, 1), lambda i: (0, 0))],
        out_specs=[pl.BlockSpec((tm * P, LANES), lambda i: (i, 0)),
                   pl.BlockSpec((TOP_K, tm), lambda i: (0, i)),
                   pl.BlockSpec((TOP_K, tm), lambda i: (0, i)),
                   pl.BlockSpec((TOP_K, tm), lambda i: (0, i)),
                   pl.BlockSpec((N_EXPERTS, LANES), lambda i: (0, 0))],
        out_shape=[jax.ShapeDtypeStruct((T * P, LANES), jnp.int32),
                   jax.ShapeDtypeStruct((TOP_K, T), F32),
                   jax.ShapeDtypeStruct((TOP_K, T), jnp.int32),
                   jax.ShapeDtypeStruct((TOP_K, T), jnp.int32),
                   jax.ShapeDtypeStruct((N_EXPERTS, LANES), F32)],
        scratch_shapes=[pltpu.VMEM((N_EXPERTS, LANES), F32)],
        compiler_params=_cparams(("arbitrary",)),
        name="router",
    )(x, gain, shift, rw_t, bias)


def _dispatch_kernel(P, idx_ref, rank_ref, row_start_ref, pad_start_ref, pad_len_ref, nu_ref, h_ref, xs_hbm,
                     zero_ref, sem, pad_sem):
    i = pl.program_id(0)
    tc = idx_ref.shape[1]
    tile_rows = zero_ref.shape[0]

    def slab(ref, row):
        return ref.at[pl.ds(pl.multiple_of(row * P, math.gcd(P, 8)), P)]

    def token(n, carry):
        src = slab(h_ref, n)
        for j in range(TOP_K):
            dest = row_start_ref[idx_ref[j, n]] + rank_ref[j, n]
            pltpu.make_async_copy(src, slab(xs_hbm, dest), sem).start()
        return carry

    lax.fori_loop(0, tc, token, 0, unroll=4)

    @pl.when(i == 0)
    def _():
        zero_ref[...] = jnp.zeros_like(zero_ref)
        zero_slab = zero_ref.at[pl.ds(0, P)]

        def expert(e, carry):
            def fill(r, c):
                pltpu.make_async_copy(zero_slab, slab(xs_hbm, pad_start_ref[e] + r), pad_sem).start()
                return c

            def drain(r, c):
                pltpu.make_async_copy(zero_slab, slab(xs_hbm, 0), pad_sem).wait()
                return c

            lax.fori_loop(0, pad_len_ref[e], fill, 0)
            lax.fori_loop(0, pad_len_ref[e], drain, 0)
            return carry

        lax.fori_loop(0, N_EXPERTS, expert, 0)

        def tile_dst(t):
            return xs_hbm.at[pl.ds(pl.multiple_of(t * tile_rows, 8), tile_rows)]

        def fill_tile(t, c):
            pltpu.make_async_copy(zero_ref, tile_dst(t), pad_sem).start()
            return c

        def drain_tile(t, c):
            pltpu.make_async_copy(zero_ref, tile_dst(0), pad_sem).wait()
            return c

        n_tiles = xs_hbm.shape[0] // tile_rows
        lax.fori_loop(nu_ref[0], n_tiles, fill_tile, 0)
        lax.fori_loop(nu_ref[0], n_tiles, drain_tile, 0)

    for j in range(TOP_K):
        pltpu.make_async_copy(h_ref, xs_hbm.at[pl.ds(0, tc * P)], sem).wait()


def _dispatch(idx, rank, row_start, pad_start, pad_len, n_used, h_tok, n_tiles, tm_e, tc):
    T = idx.shape[1]
    P = h_tok.shape[0] // T
    smem = pl.BlockSpec(memory_space=pltpu.SMEM)
    slots = pl.BlockSpec((TOP_K, tc), lambda i: (0, i), memory_space=pltpu.SMEM)
    return pl.pallas_call(
        functools.partial(_dispatch_kernel, P),
        grid=(T // tc,),
        in_specs=[slots, slots, smem, smem, smem, smem,
                  pl.BlockSpec((tc * P, LANES), lambda i: (i, 0))],
        out_specs=pl.BlockSpec(memory_space=pl.ANY),
        out_shape=jax.ShapeDtypeStruct((n_tiles * tm_e * P, LANES), h_tok.dtype),
        scratch_shapes=[pltpu.VMEM((tm_e * P, LANES), h_tok.dtype), pltpu.SemaphoreType.DMA,
                        pltpu.SemaphoreType.DMA],
        compiler_params=_cparams(("arbitrary",)),
        name="dispatch",
    )(idx, rank, row_start, pad_start, pad_len, n_used, h_tok)


def _experts_kernel(te_ref, nu_ref, x_ref, wg_ref, wu_ref, wd_ref, o_ref, wgc_ref, wuc_ref, wdc_ref):
    sx = wg_ref.shape[3] // LANES // 2
    px = _pitch(sx)
    tm = x_ref.shape[0] // px
    kx = 8 if sx % 8 == 0 else sx

    @pl.when(pl.program_id(0) >= nu_ref[0])
    def _():
        o_ref[...] = jnp.zeros_like(o_ref)

    @pl.when(pl.program_id(0) < nu_ref[0])
    def _():
        i = pl.program_id(0)

        @pl.when(jnp.logical_or(i == 0, te_ref[i] != te_ref[jnp.maximum(i - 1, 0)]))
        def _():
            wgc_ref[...] = wg_ref[0, 0].astype(BF16)
            wuc_ref[...] = wu_ref[0, 0].astype(BF16)
            wdc_ref[...] = wd_ref[0, 0].astype(BF16)

        g = u = None
        for c0 in range(0, sx, kx):
            words = jnp.concatenate([x_ref[pl.ds(s, tm, stride=px), :] for s in range(c0, c0 + kx)], axis=1)
            for xk, col0 in zip(_unpack_bf16_pairs(words), (c0, sx + c0)):
                ks = slice(col0 * LANES, (col0 + kx) * LANES)
                gk = lax.dot_general(xk, wgc_ref[:, ks], NT, preferred_element_type=F32)
                uk = lax.dot_general(xk, wuc_ref[:, ks], NT, preferred_element_type=F32)
                g = gk if g is None else g + gk
                u = uk if u is None else u + uk
        act = (_silu(g) * u).astype(BF16)
        for c0 in range(0, sx, kx):
            halves = []
            for col0 in (c0, sx + c0):
                ns = slice(col0 * LANES, (col0 + kx) * LANES)
                halves.append(jnp.dot(act, wdc_ref[:, ns], preferred_element_type=F32))
            words = _pack_bf16_pairs(jnp.concatenate(halves, axis=1))
            for s in range(kx):
                o_ref[pl.ds(c0 + s, tm, stride=px), :] = words[:, s * LANES:(s + 1) * LANES]
        for s in range(sx, px):
            o_ref[pl.ds(s, tm, stride=px), :] = jnp.zeros((tm, LANES), o_ref.dtype)


def _experts(tile_expert, n_used, x_tok, wg, wu, wd, layer, tm):
    _, E, ff, D = wg.shape
    px = _pitch(D // 2 // LANES)
    n_tiles = x_tok.shape[0] // (tm * px)

    def row_map(i, te, nu):
        return (jnp.minimum(i, nu[0] - 1), 0)

    def w_map(i, te, nu):
        return (layer, te[i], 0, 0)

    grid_spec = pltpu.PrefetchScalarGridSpec(
        num_scalar_prefetch=2,
        grid=(n_tiles,),
        in_specs=[pl.BlockSpec((tm * px, LANES), row_map),
                  pl.BlockSpec((1, 1, ff, D), w_map),
                  pl.BlockSpec((1, 1, ff, D), w_map),
                  pl.BlockSpec((1, 1, ff, D), w_map)],
        out_specs=pl.BlockSpec((tm * px, LANES), lambda i, te, nu: (i, 0)),
        scratch_shapes=[pltpu.VMEM((ff, D), BF16)] * 3,
    )
    return pl.pallas_call(
        _experts_kernel,
        grid_spec=grid_spec,
        out_shape=jax.ShapeDtypeStruct(x_tok.shape, jnp.int32),
        compiler_params=_cparams(("arbitrary",)),
        name="experts",
    )(tile_expert, n_used, x_tok, wg, wu, wd)


def _combine_kernel(has_norm, idx_ref, rank_ref, row_start_ref, w_ref, ysh_ref, x_ref, gate2_ref, *rest):
    if has_norm:
        norm_gain_ref, y_hbm, o_ref, buf_ref, acc_ref, sem = rest
    else:
        y_hbm, o_ref, buf_ref, acc_ref, sem = rest
    tc, D = x_ref.shape
    S = D // LANES
    P = _pitch(S)
    sy = S // 2
    py = _pitch(sy)
    align = math.gcd(py, 8)
    slot_rows = tc * py

    def token(n, carry):
        for j in range(TOP_K):
            dest = row_start_ref[idx_ref[j, n]] + rank_ref[j, n]
            src = y_hbm.at[pl.ds(pl.multiple_of(dest * py, align), sy)]
            dst = buf_ref.at[pl.ds(pl.multiple_of(j * slot_rows + n * py, align), sy)]
            pltpu.make_async_copy(src, dst, sem).start()
        return carry

    lax.fori_loop(0, tc, token, 0, unroll=4)
    n_rows = TOP_K * tc * sy
    pltpu.make_async_copy(y_hbm.at[pl.ds(0, n_rows)], buf_ref.at[pl.ds(0, n_rows)], sem).wait()

    def accumulate(n, carry):
        lo, hi = _unpack_bf16_pairs(ysh_ref[pl.ds(pl.multiple_of(n * py, align), sy), :], F32)
        for j in range(TOP_K):
            words = buf_ref[pl.ds(pl.multiple_of(j * slot_rows + n * py, align), sy), :]
            slab_lo, slab_hi = _unpack_bf16_pairs(words, F32)
            lo = lo + w_ref[j, n] * slab_lo
            hi = hi + w_ref[j, n] * slab_hi
        row = pl.multiple_of(n * P, math.gcd(P, 8))
        acc_ref[pl.ds(row, sy), :] = lo
        acc_ref[pl.ds(row + sy, sy), :] = hi
        return carry

    lax.fori_loop(0, tc, accumulate, 0, unroll=2)
    for s in range(S):
        ls = slice(s * LANES, (s + 1) * LANES)
        o_ref[:, ls] = x_ref[:, ls] + gate2_ref[:, ls] * acc_ref[pl.ds(s, tc, stride=P), :]
    if has_norm:
        xo = o_ref[...]
        o_ref[...] = xo * lax.rsqrt(jnp.mean(xo * xo, axis=-1, keepdims=True) + NORM_EPS) * norm_gain_ref[...]


def _combine(idx, rank, row_start, wts, y_shared, x, gate2, y_sorted, norm_gain=None, tc=128):
    T, D = x.shape
    tc = min(tc, T)
    P = _pitch(D // LANES)
    py = _pitch(D // 2 // LANES)
    slots = pl.BlockSpec((TOP_K, tc), lambda i: (0, i), memory_space=pltpu.SMEM)
    vec = pl.BlockSpec((1, D), lambda i: (0, 0))
    in_specs = [slots, slots, pl.BlockSpec(memory_space=pltpu.SMEM), slots,
                pl.BlockSpec((tc * py, LANES), lambda i: (i, 0)),
                pl.BlockSpec((tc, D), lambda i: (i, 0)), vec]
    args = [idx, rank, row_start, wts, y_shared, x, gate2]
    if norm_gain is not None:
        in_specs.append(vec)
        args.append(norm_gain)
    in_specs.append(pl.BlockSpec(memory_space=pl.ANY))
    args.append(y_sorted)
    return pl.pallas_call(
        functools.partial(_combine_kernel, norm_gain is not None),
        grid=(T // tc,),
        in_specs=in_specs,
        out_specs=pl.BlockSpec((tc, D), lambda i: (i, 0)),
        out_shape=jax.ShapeDtypeStruct((T, D), F32),
        scratch_shapes=[pltpu.VMEM((TOP_K * tc * py, LANES), jnp.int32), pltpu.VMEM((tc * P, LANES), F32),
                        pltpu.SemaphoreType.DMA],
        compiler_params=_cparams(("arbitrary",)),
        name="combine",
    )(*args)


def _moe(x, gain, shift, gate2, rw_t, bias, wg, wu, wd, swg, swu, swd, layer, norm_gain=None, tm_e=256,
         tc_d=256):
    T, D = x.shape
    tm_e = min(tm_e, T)
    tc_d = min(tc_d, T)
    h_tok, wts, idx, rank, cnt = _router(x, gain, shift, rw_t, bias)
    counts = cnt[:, 0].astype(jnp.int32)
    tiles = (counts + tm_e - 1) // tm_e
    tiles_cum = jnp.cumsum(tiles)
    row_start = (tiles_cum - tiles) * tm_e
    n_tiles = (T * TOP_K) // tm_e + N_EXPERTS
    tile_ids = jnp.arange(n_tiles, dtype=jnp.int32)
    tile_expert = jnp.minimum(jnp.sum((tiles_cum[None, :] <= tile_ids[:, None]).astype(jnp.int32), axis=1),
                              N_EXPERTS - 1)
    n_used = tiles_cum[-1:].astype(jnp.int32)
    x_sorted = _dispatch(idx, rank, row_start, row_start + counts, tiles * tm_e - counts, n_used, h_tok,
                         n_tiles, tm_e, tc_d)
    y_sorted = _experts(tile_expert, n_used, x_sorted, wg, wu, wd, layer, tm_e)
    n_sh = T // tm_e
    y_shared = _experts(jnp.zeros((n_sh,), jnp.int32), jnp.full((1,), n_sh, jnp.int32), h_tok,
                        swg[:, None], swu[:, None], swd[:, None], layer, tm_e)
    return _combine(idx, rank, row_start, wts, y_shared, x, gate2, y_sorted, norm_gain)


def _pad_cols(a, n):
    return jnp.pad(a, ((0, 0), (0, n - a.shape[1])))


def _pad_rows(a, n):
    return jnp.pad(a, ((0, n - a.shape[0]), (0, 0)))


def _pad_rwkv_cols(a, W, w_lora, a_lora):
    c0 = 3 * W
    return jnp.concatenate([a[:, :c0], _pad_cols(a[:, c0:c0 + w_lora], LORA_PAD),
                            _pad_cols(a[:, c0 + w_lora:c0 + w_lora + a_lora], LORA_PAD),
                            a[:, c0 + w_lora + a_lora:]], axis=1)


def kernel(x, c, w_mod, b_mod, norm_mix, norm_ffn, w_in, rwkv_mu, rwkv_w0, rwkv_w_up, rwkv_a0, rwkv_a_up, rwkv_g_up, rwkv_k_k, rwkv_k_a, rwkv_r_k, rwkv_gn_w, rwkv_gn_b, rwkv_v0, rwkv_v_down, rwkv_v_up, hgrn_lower_bounds, hgrn_norm, w_out, router_w, router_bias, expert_w_gate, expert_w_up, expert_w_down, shared_w_gate, shared_w_up, shared_w_down, final_norm):
    B, T, D = x.shape
    L = w_mod.shape[0]
    W = rwkv_w0.shape[1]
    w_lora, a_lora = rwkv_w_up.shape[1], rwkv_a_up.shape[1]
    rwkv_cols = rwkv_mu.shape[1]
    rwkv_cols_p = 3 * W + 2 * LORA_PAD + G_LORA

    lb_soft = jax.nn.softmax(hgrn_lower_bounds.astype(F32), axis=0)
    lbs = jnp.cumsum(lb_soft, axis=0) - lb_soft[0]
    mod = _modulation(c, w_mod, b_mod)

    xs = x.reshape(B * T, D)
    v_first = None
    for l in range(L):
        sh1, sc1, g1, sh2, sc2, g2 = [mod[l, n * D:(n + 1) * D].reshape(1, D) for n in range(6)]
        w_in_p = _pad_rwkv_cols(w_in[l], W, w_lora, a_lora).T.astype(BF16)
        p = _norm_proj(xs, norm_mix[l].reshape(1, D) * (1.0 + sc1), sh1, w_in_p)
        mu_p = _pad_rwkv_cols(rwkv_mu[l].reshape(1, -1), W, w_lora, a_lora)
        vres = None
        if l > 0:
            vres = (rwkv_v0[l - 1].reshape(1, W), _pad_cols(rwkv_v_down[l - 1], V_LORA_PAD).astype(BF16),
                    _pad_rows(rwkv_v_up[l - 1], V_LORA_PAD).astype(BF16), v_first)
        r, lw, k, v, kk, kka, g = _rwkv_prep(
            p, mu_p, rwkv_w0[l].reshape(1, W), _pad_rows(rwkv_w_up[l], LORA_PAD).astype(BF16),
            rwkv_a0[l].reshape(1, W), _pad_rows(rwkv_a_up[l], LORA_PAD).astype(BF16),
            rwkv_g_up[l].astype(BF16), rwkv_k_k[l].reshape(1, W), rwkv_k_a[l].reshape(1, W), vres)
        if l == 0:
            v_first = v
        y_r = _rwkv_recurrence(r, lw, k, v, kk, kka, g, rwkv_r_k[l].reshape(1, W),
                               rwkv_gn_w[l].reshape(1, W), rwkv_gn_b[l].reshape(1, W))
        y_h = _hgrn(p, rwkv_cols_p, lbs[l].reshape(1, -1), hgrn_norm[l].reshape(1, -1))
        xs = _out_proj(y_r, y_h, w_out[l].astype(BF16), xs, g1)
        xs = _moe(xs, norm_ffn[l].reshape(1, D) * (1.0 + sc2), sh2, g2, router_w[l].T,
                  router_bias[l].reshape(-1, 1), jnp.swapaxes(expert_w_gate, 2, 3),
                  jnp.swapaxes(expert_w_up, 2, 3), expert_w_down, jnp.swapaxes(shared_w_gate, 1, 2),
                  jnp.swapaxes(shared_w_up, 1, 2), shared_w_down, l,
                  final_norm.reshape(1, D) if l == L - 1 else None)
    return xs.reshape(B, T, D)
```

```python
import functools
import math

import jax
import jax.numpy as jnp
from jax import lax
from jax.experimental import pallas as pl
from jax.experimental.pallas import tpu as pltpu

F32 = jnp.float32
BF16 = jnp.bfloat16

LANES = 128
VMEM_LIMIT = 56 * 1024 * 1024

RWKV_HEAD = 64
HGRN_HEAD = 128
CHUNK = 64
SUB = 8
LORA_PAD = 128
G_LORA = 256
V_LORA_PAD = 128
N_EXPERTS = 64
N_GROUPS = 8
TOPK_GROUPS = 4
TOP_K = 8
ROUTE_SCALE = 2.5
MASKED_SCORE = -1e4
MIN_FORGET = 1e-30
NORM_EPS = 1e-6
GN_EPS = 64e-5
DECAY_SCALE = 0.6065306597126334

NN = (((1,), (0,)), ((), ()))
NT = (((1,), (1,)), ((), ()))
TN = (((0,), (0,)), ((), ()))


def _mm(a, b, dims=NN):
    return lax.dot_general(a.astype(BF16), b.astype(BF16), dims, preferred_element_type=F32)


def _mmx(a, b, dims=NN):
    return lax.dot_general(a.astype(F32), b.astype(F32), dims, preferred_element_type=F32,
                           precision=lax.Precision.HIGHEST)


def _split3(x):
    hi = x.astype(BF16)
    r1 = x - hi.astype(F32)
    mid = r1.astype(BF16)
    lo = (r1 - mid.astype(F32)).astype(BF16)
    return hi, mid, lo


def _mm_left01(a01, b, dims=NN):
    a = a01.astype(BF16)
    d = functools.partial(lax.dot_general, dimension_numbers=dims, preferred_element_type=F32)
    hi, mid, lo = _split3(b)
    return d(a, hi) + d(a, mid) + d(a, lo)


def _mm_right01(a, b01, dims=NN):
    b = b01.astype(BF16)
    d = functools.partial(lax.dot_general, dimension_numbers=dims, preferred_element_type=F32)
    hi, mid, lo = _split3(a)
    return d(hi, b) + d(mid, b) + d(lo, b)


def _sigmoid(x):
    return 1.0 / (1.0 + jnp.exp(-x))


def _silu(x):
    return x * _sigmoid(x)


def _cparams(sem):
    return pltpu.CompilerParams(dimension_semantics=sem, vmem_limit_bytes=VMEM_LIMIT)


def _mod_kernel(c_ref, w_ref, b_ref, o_ref):
    cond = _silu(c_ref[...])
    tn = w_ref.shape[2]
    cols = [jnp.sum(w_ref[0, :, j * LANES:(j + 1) * LANES] * cond, axis=0, keepdims=True)
            for j in range(tn // LANES)]
    o_ref[0] = jnp.concatenate(cols, axis=1) + b_ref[0]


def _modulation(c, w_mod, b_mod):
    L, D, N = w_mod.shape
    tn = 512
    c_lanes = jnp.broadcast_to(c.reshape(D, 1), (D, LANES))
    out = pl.pallas_call(
        _mod_kernel,
        grid=(L, N // tn),
        in_specs=[pl.BlockSpec((D, LANES), lambda l, j: (0, 0)),
                  pl.BlockSpec((1, D, tn), lambda l, j: (l, 0, j)),
                  pl.BlockSpec((1, 1, tn), lambda l, j: (l, 0, j))],
        out_specs=pl.BlockSpec((1, 1, tn), lambda l, j: (l, 0, j)),
        out_shape=jax.ShapeDtypeStruct((L, 1, N), F32),
        compiler_params=_cparams(("arbitrary", "arbitrary")),
        name="modulation",
    )(c_lanes, w_mod, b_mod.reshape(L, 1, N))
    return out[:, 0, :]


def _normmod(x, gain, shift):
    y = x * lax.rsqrt(jnp.mean(x * x, axis=-1, keepdims=True) + NORM_EPS)
    return y * gain + shift


def _norm_proj_kernel(x_ref, gain_ref, shift_ref, w_ref, o_ref, h_ref):
    @pl.when(pl.program_id(1) == 0)
    def _():
        h_ref[...] = _normmod(x_ref[...], gain_ref[...], shift_ref[...]).astype(BF16)

    o_ref[...] = lax.dot_general(h_ref[...], w_ref[...], NT, preferred_element_type=F32)


def _norm_proj(x, gain, shift, w_t, tm=1024, tn=512):
    T, D = x.shape
    N = w_t.shape[0]
    return pl.pallas_call(
        _norm_proj_kernel,
        grid=(T // tm, N // tn),
        in_specs=[pl.BlockSpec((tm, D), lambda i, j: (i, 0), pipeline_mode=pl.Buffered(1)),
                  pl.BlockSpec((1, D), lambda i, j: (0, 0)),
                  pl.BlockSpec((1, D), lambda i, j: (0, 0)),
                  pl.BlockSpec((tn, D), lambda i, j: (j, 0))],
        out_specs=pl.BlockSpec((tm, tn), lambda i, j: (i, j)),
        out_shape=jax.ShapeDtypeStruct((T, N), F32),
        scratch_shapes=[pltpu.VMEM((tm, D), BF16)],
        compiler_params=_cparams(("arbitrary", "arbitrary")),
        name="norm_proj",
    )(x, gain, shift, w_t)


def _pair_ones():
    r = lax.broadcasted_iota(jnp.int32, (LANES, LANES), 0) // RWKV_HEAD
    c = lax.broadcasted_iota(jnp.int32, (LANES, LANES), 1) // RWKV_HEAD
    return jnp.where(r == c, 1.0, 0.0).astype(F32)


def _rwkv_prep_kernel(has_vres, *refs):
    n_in = 20 if has_vres else 16
    (r_ref, k_ref, v_ref, lo_ref, rp_ref, kp_ref, vp_ref, lop_ref, mu_ref, w0_ref, wup_ref,
     a0_ref, aup_ref, gup_ref, kk_ref, ka_ref) = refs[:16]
    if has_vres:
        v0_ref, vdn_ref, vup_ref, vf_ref = refs[16:20]
    ro_ref, lwo_ref, ko_ref, vo_ref, kko_ref, kkao_ref, go_ref = refs[n_in:]
    first = pl.program_id(0) == 0
    W = r_ref.shape[1]

    def shift(cur_ref, prev_ref, mu):
        cur = cur_ref[...]
        prev_last = jnp.where(first, 0.0, prev_ref[7:8, :])
        rolled = pltpu.roll(cur, 1, axis=0)
        row = lax.broadcasted_iota(jnp.int32, cur.shape, 0)
        prev = jnp.where(row == 0, prev_last, rolled)
        return cur + (prev - cur) * mu

    r = shift(r_ref, rp_ref, mu_ref[:, 0:W])
    k = shift(k_ref, kp_ref, mu_ref[:, W:2 * W])
    v = shift(v_ref, vp_ref, mu_ref[:, 2 * W:3 * W])
    lo = shift(lo_ref, lop_ref, mu_ref[:, 3 * W:])
    w_lo = lo[:, 0:LORA_PAD]
    a_lo = lo[:, LORA_PAD:2 * LORA_PAD]
    g_lo = lo[:, 2 * LORA_PAD:]

    z = w0_ref[...] + _mm(jnp.tanh(w_lo), wup_ref[...])
    lw = -DECAY_SCALE * _sigmoid(z)
    a = _sigmoid(a0_ref[...] + _mm(a_lo, aup_ref[...]))
    g = _mm(_sigmoid(g_lo), gup_ref[...])
    if has_vres:
        mix = _sigmoid(v0_ref[...] + _mm(_mm(v, vdn_ref[...]), vup_ref[...]))
        v = v + (vf_ref[...] - v) * mix

    kkr = k * kk_ref[...]
    sq = kkr * kkr
    ones = _pair_ones()
    ss = jnp.concatenate(
        [_mm_right01(sq[:, j * LANES:(j + 1) * LANES], ones) for j in range(W // LANES)], axis=1)
    kk = kkr / jnp.maximum(jnp.sqrt(ss), 1e-12)
    k2 = k * (1.0 + (a - 1.0) * ka_ref[...])

    ro_ref[...] = r
    lwo_ref[...] = lw
    ko_ref[...] = k2
    vo_ref[...] = v
    kko_ref[...] = kk
    kkao_ref[...] = kk * a
    go_ref[...] = g


def _rwkv_prep(p, mu_p, w0, wup, a0, aup, gup, k_k, k_a, vres, tp=128):
    T = p.shape[0]
    W = w0.shape[1]
    lo_w = 2 * LORA_PAD + G_LORA
    nb = tp // 8
    lo_blk = 3 * W // lo_w

    def cur(cb, width):
        return pl.BlockSpec((tp, width), lambda i, cb=cb: (i, cb))

    def prev(cb, width):
        return pl.BlockSpec((8, width), lambda i, cb=cb: (jnp.maximum(i * nb - 1, 0), cb))

    def vec(n):
        return pl.BlockSpec((1, n), lambda i: (0, 0))

    def mat(a, b):
        return pl.BlockSpec((a, b), lambda i: (0, 0))

    in_specs = [cur(0, W), cur(1, W), cur(2, W), cur(lo_blk, lo_w),
                prev(0, W), prev(1, W), prev(2, W), prev(lo_blk, lo_w),
                vec(3 * W + lo_w), vec(W), mat(LORA_PAD, W), vec(W), mat(LORA_PAD, W),
                mat(G_LORA, W), vec(W), vec(W)]
    args = [p, p, p, p, p, p, p, p, mu_p, w0, wup, a0, aup, gup, k_k, k_a]
    if vres is not None:
        v0, vdn, vup, v_first = vres
        in_specs += [vec(W), mat(W, V_LORA_PAD), mat(V_LORA_PAD, W),
                     pl.BlockSpec((tp, W), lambda i: (i, 0))]
        args += [v0, vdn, vup, v_first]
    out_spec = pl.BlockSpec((tp, W), lambda i: (i, 0))
    sds = jax.ShapeDtypeStruct((T, W), F32)
    return pl.pallas_call(
        functools.partial(_rwkv_prep_kernel, vres is not None),
        grid=(T // tp,),
        in_specs=in_specs,
        out_specs=[out_spec] * 7,
        out_shape=[sds] * 7,
        compiler_params=_cparams(("arbitrary",)),
        name="rwkv_prep",
    )(*args)


def _tri_incl(n):
    row = lax.broadcasted_iota(jnp.int32, (n, n), 0)
    col = lax.broadcasted_iota(jnp.int32, (n, n), 1)
    return jnp.where(col <= row, 1.0, 0.0).astype(BF16)


def _rwkv_pair(r, c, lw, k, v, kk, kka, g, rk, gnw, gnb, st):
    C = r.shape[0]
    C2 = 2 * C
    cp = c - lw
    c_last = c[C - 1:C, :]
    e_c, e_cp, e_nc, e_dl = jnp.exp(c), jnp.exp(cp), jnp.exp(-c), jnp.exp(c_last - c)

    lane = lax.broadcasted_iota(jnp.int32, (C, LANES), 1)
    m0 = lane < RWKV_HEAD

    def stack(x):
        return jnp.concatenate([jnp.where(m0, x, 0.0), jnp.where(m0, 0.0, x)], axis=0)

    aq_rt = jnp.concatenate([stack(kk * e_cp), stack(r * e_c)], axis=0).astype(BF16)
    bt_kt = jnp.concatenate([stack(kka * e_nc), stack(k * e_nc)], axis=0).astype(BF16)
    kw_bw = jnp.concatenate([stack(k * e_dl), stack(kka * e_dl)], axis=0).astype(BF16)
    vs = stack(v)
    vs_b = vs.astype(BF16)

    row = lax.broadcasted_iota(jnp.int32, (C2, C2), 0)
    col = lax.broadcasted_iota(jnp.int32, (C2, C2), 1)
    t_i, s_j = row % C, col % C
    strict, incl, eye = s_j < t_i, s_j <= t_i, row == col

    yield
    p = _mm(aq_rt, bt_kt, NT)
    yield
    l_ab = jnp.where(strict, p[:C2, :C2], 0.0)
    l_ak = jnp.where(strict, p[:C2, C2:], 0.0)
    l_rb = jnp.where(incl, p[C2:, :C2], 0.0)
    l_rk = jnp.where(incl, p[C2:, C2:], 0.0)

    pw = -l_ab
    inv = jnp.where(eye, 1.0, 0.0) + pw
    n = 2
    while n < C:
        pw = _mm(pw, pw)
        yield
        inv = inv + _mm(inv, pw)
        n *= 2

    xy = _mm(aq_rt, st) + _mm(jnp.concatenate([l_ak, l_rk], axis=0), vs_b)
    yield
    u = _mm(inv, xy[:C2])
    yield
    y = xy[C2:] - _mm(l_rb, u)
    w_col = jnp.broadcast_to(jnp.exp(c_last), (C2, C2)).T
    st_new = st * w_col + _mm(kw_bw, jnp.concatenate([vs, -u], axis=0), TN)
    yield

    mh = (row // C) == (col // RWKV_HEAD)
    inv_n = 1.0 / RWKV_HEAD
    mean = jnp.sum(y, axis=1, keepdims=True) * inv_n
    yc = jnp.where(mh, y - mean, 0.0)
    var = jnp.sum(yc * yc, axis=1, keepdims=True) * inv_n
    yn = yc * lax.rsqrt(var + GN_EPS)
    bonus = jnp.sum(stack(r * k * rk), axis=1, keepdims=True) * vs
    yn_t = yn[:C] + yn[C:]
    bonus_t = bonus[:C] + bonus[C:]
    return (yn_t * gnw + gnb + bonus_t) * g, st_new


def _interleave(gens):
    results = [None] * len(gens)
    live = list(enumerate(gens))
    while live:
        still = []
        for j, gen in live:
            try:
                next(gen)
                still.append((j, gen))
            except StopIteration as stop:
                results[j] = stop.value
        live = still
    return results


def _rwkv_rec_kernel(r_ref, lw_ref, k_ref, v_ref, kk_ref, kka_ref, g_ref, rk_ref, gnw_ref, gnb_ref,
                     o_ref, st_ref):
    @pl.when(pl.program_id(1) == 0)
    def _():
        st_ref[...] = jnp.zeros_like(st_ref)

    rows, lanes = r_ref.shape
    tri = _tri_incl(CHUNK)

    def chunk(ci, carry):
        rs = pl.ds(pl.multiple_of(ci * CHUNK, CHUNK), CHUNK)
        lw_all = lw_ref[rs, :]
        c_all = _mm_left01(tri, lw_all)
        r, k, v, kk, kka, g = (ref[rs, :] for ref in (r_ref, k_ref, v_ref, kk_ref, kka_ref, g_ref))
        rk, gnw, gnb = rk_ref[...], gnw_ref[...], gnb_ref[...]
        gens = []
        for j in range(lanes // LANES):
            sl = slice(j * LANES, (j + 1) * LANES)
            gens.append(_rwkv_pair(r[:, sl], c_all[:, sl], lw_all[:, sl], k[:, sl], v[:, sl], kk[:, sl],
                                   kka[:, sl], g[:, sl], rk[:, sl], gnw[:, sl], gnb[:, sl], st_ref[j]))
        results = _interleave(gens)
        for j, (_, st_new) in enumerate(results):
            st_ref[j] = st_new
        o_ref[rs, :] = jnp.concatenate([out for out, _ in results], axis=1).astype(o_ref.dtype)
        return carry

    lax.fori_loop(0, rows // CHUNK, chunk, 0)


def _rwkv_recurrence(r, lw, k, v, kk, kka, g, rk, gnw, gnb, hb_lanes=1024, rows=512):
    T, W = r.shape
    rows = min(rows, T)
    seq = pl.BlockSpec((rows, hb_lanes), lambda h, t: (t, h))
    vec = pl.BlockSpec((1, hb_lanes), lambda h, t: (0, h))
    return pl.pallas_call(
        _rwkv_rec_kernel,
        grid=(W // hb_lanes, T // rows),
        in_specs=[seq] * 7 + [vec] * 3,
        out_specs=seq,
        out_shape=jax.ShapeDtypeStruct((T, W), BF16),
        scratch_shapes=[pltpu.VMEM((hb_lanes // LANES, LANES, LANES), F32)],
        compiler_params=_cparams(("arbitrary", "arbitrary")),
        name="rwkv_recurrence",
    )(r, lw, k, v, kk, kka, g, rk, gnw, gnb)


def _hgrn_head(q_raw, f_raw, i_c, g_raw, lb, gain, tri, s):
    C = q_raw.shape[0]
    q = _silu(q_raw)
    forget = lb + (1.0 - lb) * _sigmoid(f_raw)
    lf = jnp.log(jnp.maximum(forget, MIN_FORGET))
    kin = (1.0 - lb) * _sigmoid(-f_raw)
    b = _mm_left01(tri, lf)
    yield
    o = _mm(q * jnp.exp(b), s)

    lane_s = lax.broadcasted_iota(jnp.int32, (SUB, C), 1)
    row_s = lax.broadcasted_iota(jnp.int32, (SUB, C), 0)
    score_rows = []
    for i in range(C // SUB):
        lo, hi = i * SUB, (i + 1) * SUB
        b_i, q_i = b[lo:hi], q[lo:hi]
        blk = jnp.zeros((SUB, C), F32)
        for sidx in range(SUB):
            srow = lo + sidx
            d = jnp.exp(b_i - b[srow:srow + 1]) * q_i * kin[srow:srow + 1]
            blk = jnp.where(lane_s == srow, jnp.sum(d, axis=1, keepdims=True), blk)
        blk = jnp.where(lane_s <= row_s + lo, blk, 0.0)
        yield
        if i > 0:
            b_st = b[lo - 1:lo]
            qs = q_i * jnp.exp(b_i - b_st)
            ks = kin * jnp.exp(jnp.minimum(b_st - b, 0.0))
            blk = blk + jnp.where(lane_s < lo, _mm(qs, ks, NT), 0.0)
        score_rows.append(blk)
    scores = jnp.concatenate(score_rows, axis=0)
    o = o + _mm(scores, i_c)

    b_last = b[C - 1:C]
    f_col = jnp.broadcast_to(jnp.exp(b_last), (LANES, LANES)).T
    s_new = s * f_col + _mm(kin * jnp.exp(b_last - b), i_c, TN)

    o = o * lax.rsqrt(jnp.mean(o * o, axis=1, keepdims=True) + NORM_EPS)
    return o * gain * _sigmoid(g_raw), s_new


def _hgrn_kernel(q_ref, f_ref, i_ref, g_ref, lb_ref, gain_ref, o_ref, s_ref):
    @pl.when(pl.program_id(1) == 0)
    def _():
        s_ref[...] = jnp.zeros_like(s_ref)

    rows, lanes = q_ref.shape
    tri = _tri_incl(CHUNK)

    def chunk(ci, carry):
        rs = pl.ds(pl.multiple_of(ci * CHUNK, CHUNK), CHUNK)
        q, f, i_c, g = (ref[rs, :] for ref in (q_ref, f_ref, i_ref, g_ref))
        lb, gain = lb_ref[...], gain_ref[...]
        gens = []
        for j in range(lanes // LANES):
            sl = slice(j * LANES, (j + 1) * LANES)
            gens.append(_hgrn_head(q[:, sl], f[:, sl], i_c[:, sl], g[:, sl], lb[:, sl], gain[:, sl],
                                   tri, s_ref[j]))
        results = _interleave(gens)
        for j, (_, s_new) in enumerate(results):
            s_ref[j] = s_new
        o_ref[rs, :] = jnp.concatenate([out for out, _ in results], axis=1).astype(o_ref.dtype)
        return carry

    lax.fori_loop(0, rows // CHUNK, chunk, 0)


def _hgrn(p, col0, lb, gain, hb_lanes=512, rows=512):
    T = p.shape[0]
    W = lb.shape[1]
    rows = min(rows, T)
    nb = W // hb_lanes
    b0 = col0 // hb_lanes

    def sec(n):
        return pl.BlockSpec((rows, hb_lanes), lambda h, t, n=n: (t, b0 + n * nb + h))

    vec = pl.BlockSpec((1, hb_lanes), lambda h, t: (0, h))
    return pl.pallas_call(
        _hgrn_kernel,
        grid=(nb, T // rows),
        in_specs=[sec(0), sec(1), sec(2), sec(3), vec, vec],
        out_specs=pl.BlockSpec((rows, hb_lanes), lambda h, t: (t, h)),
        out_shape=jax.ShapeDtypeStruct((T, W), BF16),
        scratch_shapes=[pltpu.VMEM((hb_lanes // LANES, LANES, LANES), F32)],
        compiler_params=_cparams(("arbitrary", "arbitrary")),
        name="hgrn2",
    )(p, p, p, p, lb, gain)


def _out_proj_kernel(ya_ref, yb_ref, wa_ref, wb_ref, x_ref, gate_ref, o_ref):
    acc = jnp.dot(ya_ref[...], wa_ref[...], preferred_element_type=F32)
    acc = acc + jnp.dot(yb_ref[...], wb_ref[...], preferred_element_type=F32)
    o_ref[...] = x_ref[...] + gate_ref[...] * acc


def _out_proj(ya, yb, w, x, gate, tm=512, tn=1024):
    T, Ka = ya.shape
    Kb = yb.shape[1]
    D = w.shape[1]
    return pl.pallas_call(
        _out_proj_kernel,
        grid=(T // tm, D // tn),
        in_specs=[pl.BlockSpec((tm, Ka), lambda i, j: (i, 0)),
                  pl.BlockSpec((tm, Kb), lambda i, j: (i, 0)),
                  pl.BlockSpec((Ka, tn), lambda i, j: (0, j)),
                  pl.BlockSpec((Kb, tn), lambda i, j: (Ka // Kb, j)),
                  pl.BlockSpec((tm, tn), lambda i, j: (i, j)),
                  pl.BlockSpec((1, tn), lambda i, j: (0, j))],
        out_specs=pl.BlockSpec((tm, tn), lambda i, j: (i, j)),
        out_shape=jax.ShapeDtypeStruct((T, D), F32),
        compiler_params=_cparams(("arbitrary", "arbitrary")),
        name="out_proj",
    )(ya, yb, w, w, x, gate)


def _first_max_onehot(work, axis, size):
    m = jnp.max(work, axis=axis, keepdims=True)
    idx = lax.broadcasted_iota(jnp.int32, work.shape, axis)
    first = jnp.min(jnp.where(work == m, idx, size), axis=axis, keepdims=True)
    return idx == first, m


def _pitch(s_per):
    return s_per + 8 if s_per % 16 == 0 else s_per


def _pack_bf16_pairs(h):
    half = h.shape[1] // 2
    bits = lambda v: lax.bitcast_convert_type(v.astype(BF16).astype(F32), jnp.uint32)
    word = (bits(h[:, :half]) >> 16) | (bits(h[:, half:]) & jnp.uint32(0xFFFF0000))
    return lax.bitcast_convert_type(word, jnp.int32)


def _unpack_bf16_pairs(word, dtype=BF16):
    u = lax.bitcast_convert_type(word, jnp.uint32)
    lo = lax.bitcast_convert_type(u << 16, F32).astype(dtype)
    hi = lax.bitcast_convert_type(u & jnp.uint32(0xFFFF0000), F32).astype(dtype)
    return lo, hi


def _store_token_major(ref, val):
    n, d = val.shape
    pitch = ref.shape[0] // n
    for s in range(d // LANES):
        ref[pl.ds(s, n, stride=pitch), :] = val[:, s * LANES:(s + 1) * LANES]
    for s in range(d // LANES, pitch):
        ref[pl.ds(s, n, stride=pitch), :] = jnp.zeros((n, LANES), ref.dtype)


def _router_kernel(x_ref, gain_ref, shift_ref, rwt_ref, bias_ref,
                   h_ref, wts_ref, idx_ref, rank_ref, cnt_ref, carry_ref):
    @pl.when(pl.program_id(0) == 0)
    def _():
        carry_ref[...] = jnp.zeros_like(carry_ref)

    h = _normmod(x_ref[...], gain_ref[...], shift_ref[...])
    _store_token_major(h_ref, _pack_bf16_pairs(h))
    tm = h.shape[0]
    per_group = N_EXPERTS // N_GROUPS
    logits = _mmx(rwt_ref[...], h, NT)
    scores = _sigmoid(logits)
    biased = scores + bias_ref[...]

    b3 = biased.reshape(N_GROUPS, per_group, tm)
    pick1, m1 = _first_max_onehot(b3, 1, per_group)
    m2 = jnp.max(jnp.where(pick1, -jnp.inf, b3), axis=1, keepdims=True)
    gscore = (m1 + m2).reshape(N_GROUPS, tm)

    work = gscore
    gsel = jnp.zeros_like(gscore)
    for _ in range(TOPK_GROUPS):
        pick, _m = _first_max_onehot(work, 0, N_GROUPS)
        gsel = jnp.where(pick, 1.0, gsel)
        work = jnp.where(pick, -jnp.inf, work)
    ok = jnp.broadcast_to(gsel.reshape(N_GROUPS, 1, tm), (N_GROUPS, per_group, tm)).reshape(N_EXPERTS, tm)
    work = jnp.where(ok > 0.5, biased, MASKED_SCORE)
    picks = []
    for _ in range(TOP_K):
        pick, _m = _first_max_onehot(work, 0, N_EXPERTS)
        picks.append(pick)
        work = jnp.where(pick, -jnp.inf, work)
    picked = functools.reduce(lambda a, b: a + b, [jnp.where(p, 1.0, 0.0) for p in picks])
    sel = picked * scores
    gates_t = sel / jnp.sum(sel, axis=0, keepdims=True) * ROUTE_SCALE

    ra = lax.broadcasted_iota(jnp.int32, (tm, tm), 0)
    rb = lax.broadcasted_iota(jnp.int32, (tm, tm), 1)
    before = jnp.where(ra < rb, 1.0, 0.0).astype(BF16)
    carry = carry_ref[:, 0:1]
    rank_full = _mm(picked, before) + carry
    e_iota = lax.broadcasted_iota(jnp.int32, (N_EXPERTS, tm), 0).astype(F32)

    def slot_rows(table):
        rows = [jnp.sum(jnp.where(p, table, 0.0), axis=0, keepdims=True) for p in picks]
        return jnp.concatenate(rows, axis=0)

    idx_ref[...] = slot_rows(e_iota).astype(jnp.int32)
    rank_ref[...] = slot_rows(rank_full).astype(jnp.int32)
    wts_ref[...] = slot_rows(gates_t)
    new_carry = carry + jnp.sum(picked, axis=1, keepdims=True)
    carry_ref[...] = jnp.broadcast_to(new_carry, carry_ref.shape)
    cnt_ref[...] = jnp.broadcast_to(new_carry, cnt_ref.shape)


def _router(x, gain, shift, rw_t, bias, tm=256):
    T, D = x.shape
    tm = min(tm, T)
    P = _pitch(D // 2 // LANES)
    return pl.pallas_call(
        _router_kernel,
        grid=(T // tm,),
        in_specs=[pl.BlockSpec((tm, D), lambda i: (i, 0)),
                  pl.BlockSpec((1, D), lambda i: (0, 0)),
                  pl.BlockSpec((1, D), lambda i: (0, 0)),
                  pl.BlockSpec((N_EXPERTS, D), lambda i: (0, 0)),
                  pl.BlockSpec((N_EXPERTS, 1), lambda i: (0, 0))],
        out_specs=[pl.BlockSpec((tm * P, LANES), lambda i: (i, 0)),
                   pl.BlockSpec((TOP_K, tm), lambda i: (0, i)),
                   pl.BlockSpec((TOP_K, tm), lambda i: (0, i)),
                   pl.BlockSpec((TOP_K, tm), lambda i: (0, i)),
                   pl.BlockSpec((N_EXPERTS, LANES), lambda i: (0, 0))],
        out_shape=[jax.ShapeDtypeStruct((T * P, LANES), jnp.int32),
                   jax.ShapeDtypeStruct((TOP_K, T), F32),
                   jax.ShapeDtypeStruct((TOP_K, T), jnp.int32),
                   jax.ShapeDtypeStruct((TOP_K, T), jnp.int32),
                   jax.ShapeDtypeStruct((N_EXPERTS, LANES), F32)],
        scratch_shapes=[pltpu.VMEM((N_EXPERTS, LANES), F32)],
        compiler_params=_cparams(("arbitrary",)),
        name="router",
    )(x, gain, shift, rw_t, bias)


def _dispatch_kernel(P, idx_ref, rank_ref, row_start_ref, pad_start_ref, pad_len_ref, nu_ref, h_ref, xs_hbm,
                     zero_ref, sem, pad_sem):
    i = pl.program_id(0)
    tc = idx_ref.shape[1]
    tile_rows = zero_ref.shape[0]

    def slab(ref, row):
        return ref.at[pl.ds(pl.multiple_of(row * P, math.gcd(P, 8)), P)]

    def token(n, carry):
        src = slab(h_ref, n)
        for j in range(TOP_K):
            dest = row_start_ref[idx_ref[j, n]] + rank_ref[j, n]
            pltpu.make_async_copy(src, slab(xs_hbm, dest), sem).start()
        return carry

    lax.fori_loop(0, tc, token, 0)

    @pl.when(i == 0)
    def _():
        zero_ref[...] = jnp.zeros_like(zero_ref)
        zero_slab = zero_ref.at[pl.ds(0, P)]

        def expert(e, carry):
            def fill(r, c):
                pltpu.make_async_copy(zero_slab, slab(xs_hbm, pad_start_ref[e] + r), pad_sem).start()
                return c

            def drain(r, c):
                pltpu.make_async_copy(zero_slab, slab(xs_hbm, 0), pad_sem).wait()
                return c

            lax.fori_loop(0, pad_len_ref[e], fill, 0)
            lax.fori_loop(0, pad_len_ref[e], drain, 0)
            return carry

        lax.fori_loop(0, N_EXPERTS, expert, 0)

        def tile_dst(t):
            return xs_hbm.at[pl.ds(pl.multiple_of(t * tile_rows, 8), tile_rows)]

        def fill_tile(t, c):
            pltpu.make_async_copy(zero_ref, tile_dst(t), pad_sem).start()
            return c

        def drain_tile(t, c):
            pltpu.make_async_copy(zero_ref, tile_dst(0), pad_sem).wait()
            return c

        n_tiles = xs_hbm.shape[0] // tile_rows
        lax.fori_loop(nu_ref[0], n_tiles, fill_tile, 0)
        lax.fori_loop(nu_ref[0], n_tiles, drain_tile, 0)

    for j in range(TOP_K):
        pltpu.make_async_copy(h_ref, xs_hbm.at[pl.ds(0, tc * P)], sem).wait()


def _dispatch(idx, rank, row_start, pad_start, pad_len, n_used, h_tok, n_tiles, tm_e, tc):
    T = idx.shape[1]
    P = h_tok.shape[0] // T
    smem = pl.BlockSpec(memory_space=pltpu.SMEM)
    slots = pl.BlockSpec((TOP_K, tc), lambda i: (0, i), memory_space=pltpu.SMEM)
    return pl.pallas_call(
        functools.partial(_dispatch_kernel, P),
        grid=(T // tc,),
        in_specs=[slots, slots, smem, smem, smem, smem,
                  pl.BlockSpec((tc * P, LANES), lambda i: (i, 0))],
        out_specs=pl.BlockSpec(memory_space=pl.ANY),
        out_shape=jax.ShapeDtypeStruct((n_tiles * tm_e * P, LANES), h_tok.dtype),
        scratch_shapes=[pltpu.VMEM((tm_e * P, LANES), h_tok.dtype), pltpu.SemaphoreType.DMA,
                        pltpu.SemaphoreType.DMA],
        compiler_params=_cparams(("arbitrary",)),
        name="dispatch",
    )(idx, rank, row_start, pad_start, pad_len, n_used, h_tok)


def _experts_kernel(te_ref, nu_ref, x_ref, wg_ref, wu_ref, wd_ref, o_ref):
    sx = wg_ref.shape[3] // LANES // 2
    px = _pitch(sx)
    tm = x_ref.shape[0] // px
    kx = 8 if sx % 8 == 0 else sx

    @pl.when(pl.program_id(0) >= nu_ref[0])
    def _():
        o_ref[...] = jnp.zeros_like(o_ref)

    @pl.when(pl.program_id(0) < nu_ref[0])
    def _():
        g = u = None
        for c0 in range(0, sx, kx):
            words = jnp.concatenate([x_ref[pl.ds(s, tm, stride=px), :] for s in range(c0, c0 + kx)], axis=1)
            for xk, col0 in zip(_unpack_bf16_pairs(words), (c0, sx + c0)):
                ks = slice(col0 * LANES, (col0 + kx) * LANES)
                gk = lax.dot_general(xk, wg_ref[0, 0, :, ks].astype(BF16), NT, preferred_element_type=F32)
                uk = lax.dot_general(xk, wu_ref[0, 0, :, ks].astype(BF16), NT, preferred_element_type=F32)
                g = gk if g is None else g + gk
                u = uk if u is None else u + uk
        act = (_silu(g) * u).astype(BF16)
        for c0 in range(0, sx, kx):
            halves = []
            for col0 in (c0, sx + c0):
                ns = slice(col0 * LANES, (col0 + kx) * LANES)
                halves.append(jnp.dot(act, wd_ref[0, 0, :, ns].astype(BF16), preferred_element_type=F32))
            words = _pack_bf16_pairs(jnp.concatenate(halves, axis=1))
            for s in range(kx):
                o_ref[pl.ds(c0 + s, tm, stride=px), :] = words[:, s * LANES:(s + 1) * LANES]
        for s in range(sx, px):
            o_ref[pl.ds(s, tm, stride=px), :] = jnp.zeros((tm, LANES), o_ref.dtype)


def _experts(tile_expert, n_used, x_tok, wg, wu, wd, layer, tm):
    _, E, ff, D = wg.shape
    px = _pitch(D // 2 // LANES)
    n_tiles = x_tok.shape[0] // (tm * px)

    def row_map(i, te, nu):
        return (jnp.minimum(i, nu[0] - 1), 0)

    def w_map(i, te, nu):
        return (layer, te[i], 0, 0)

    grid_spec = pltpu.PrefetchScalarGridSpec(
        num_scalar_prefetch=2,
        grid=(n_tiles,),
        in_specs=[pl.BlockSpec((tm * px, LANES), row_map),
                  pl.BlockSpec((1, 1, ff, D), w_map),
                  pl.BlockSpec((1, 1, ff, D), w_map),
                  pl.BlockSpec((1, 1, ff, D), w_map)],
        out_specs=pl.BlockSpec((tm * px, LANES), lambda i, te, nu: (i, 0)),
    )
    return pl.pallas_call(
        _experts_kernel,
        grid_spec=grid_spec,
        out_shape=jax.ShapeDtypeStruct(x_tok.shape, jnp.int32),
        compiler_params=_cparams(("arbitrary",)),
        name="experts",
    )(tile_expert, n_used, x_tok, wg, wu, wd)


def _combine_kernel(has_norm, idx_ref, rank_ref, row_start_ref, w_ref, ysh_ref, x_ref, gate2_ref, *rest):
    if has_norm:
        norm_gain_ref, y_hbm, o_ref, buf_ref, acc_ref, sem = rest
    else:
        y_hbm, o_ref, buf_ref, acc_ref, sem = rest
    tc, D = x_ref.shape
    S = D // LANES
    P = _pitch(S)
    sy = S // 2
    py = _pitch(sy)
    align = math.gcd(py, 8)
    slot_rows = tc * py

    def token(n, carry):
        for j in range(TOP_K):
            dest = row_start_ref[idx_ref[j, n]] + rank_ref[j, n]
            src = y_hbm.at[pl.ds(pl.multiple_of(dest * py, align), sy)]
            dst = buf_ref.at[pl.ds(pl.multiple_of(j * slot_rows + n * py, align), sy)]
            pltpu.make_async_copy(src, dst, sem).start()
        return carry

    lax.fori_loop(0, tc, token, 0)
    n_rows = TOP_K * tc * sy
    pltpu.make_async_copy(y_hbm.at[pl.ds(0, n_rows)], buf_ref.at[pl.ds(0, n_rows)], sem).wait()

    def accumulate(n, carry):
        lo, hi = _unpack_bf16_pairs(ysh_ref[pl.ds(pl.multiple_of(n * py, align), sy), :], F32)
        for j in range(TOP_K):
            words = buf_ref[pl.ds(pl.multiple_of(j * slot_rows + n * py, align), sy), :]
            slab_lo, slab_hi = _unpack_bf16_pairs(words, F32)
            lo = lo + w_ref[j, n] * slab_lo
            hi = hi + w_ref[j, n] * slab_hi
        row = pl.multiple_of(n * P, math.gcd(P, 8))
        acc_ref[pl.ds(row, sy), :] = lo
        acc_ref[pl.ds(row + sy, sy), :] = hi
        return carry

    lax.fori_loop(0, tc, accumulate, 0, unroll=2)
    for s in range(S):
        ls = slice(s * LANES, (s + 1) * LANES)
        o_ref[:, ls] = x_ref[:, ls] + gate2_ref[:, ls] * acc_ref[pl.ds(s, tc, stride=P), :]
    if has_norm:
        xo = o_ref[...]
        o_ref[...] = xo * lax.rsqrt(jnp.mean(xo * xo, axis=-1, keepdims=True) + NORM_EPS) * norm_gain_ref[...]


def _combine(idx, rank, row_start, wts, y_shared, x, gate2, y_sorted, norm_gain=None, tc=128):
    T, D = x.shape
    tc = min(tc, T)
    P = _pitch(D // LANES)
    py = _pitch(D // 2 // LANES)
    slots = pl.BlockSpec((TOP_K, tc), lambda i: (0, i), memory_space=pltpu.SMEM)
    vec = pl.BlockSpec((1, D), lambda i: (0, 0))
    in_specs = [slots, slots, pl.BlockSpec(memory_space=pltpu.SMEM), slots,
                pl.BlockSpec((tc * py, LANES), lambda i: (i, 0)),
                pl.BlockSpec((tc, D), lambda i: (i, 0)), vec]
    args = [idx, rank, row_start, wts, y_shared, x, gate2]
    if norm_gain is not None:
        in_specs.append(vec)
        args.append(norm_gain)
    in_specs.append(pl.BlockSpec(memory_space=pl.ANY))
    args.append(y_sorted)
    return pl.pallas_call(
        functools.partial(_combine_kernel, norm_gain is not None),
        grid=(T // tc,),
        in_specs=in_specs,
        out_specs=pl.BlockSpec((tc, D), lambda i: (i, 0)),
        out_shape=jax.ShapeDtypeStruct((T, D), F32),
        scratch_shapes=[pltpu.VMEM((TOP_K * tc * py, LANES), jnp.int32), pltpu.VMEM((tc * P, LANES), F32),
                        pltpu.SemaphoreType.DMA],
        compiler_params=_cparams(("arbitrary",)),
        name="combine",
    )(*args)


def _moe(x, gain, shift, gate2, rw_t, bias, wg, wu, wd, swg, swu, swd, layer, norm_gain=None, tm_e=256,
         tc_d=256):
    T, D = x.shape
    tm_e = min(tm_e, T)
    tc_d = min(tc_d, T)
    h_tok, wts, idx, rank, cnt = _router(x, gain, shift, rw_t, bias)
    counts = cnt[:, 0].astype(jnp.int32)
    tiles = (counts + tm_e - 1) // tm_e
    tiles_cum = jnp.cumsum(tiles)
    row_start = (tiles_cum - tiles) * tm_e
    n_tiles = (T * TOP_K) // tm_e + N_EXPERTS
    tile_ids = jnp.arange(n_tiles, dtype=jnp.int32)
    tile_expert = jnp.minimum(jnp.sum((tiles_cum[None, :] <= tile_ids[:, None]).astype(jnp.int32), axis=1),
                              N_EXPERTS - 1)
    n_used = tiles_cum[-1:].astype(jnp.int32)
    x_sorted = _dispatch(idx, rank, row_start, row_start + counts, tiles * tm_e - counts, n_used, h_tok,
                         n_tiles, tm_e, tc_d)
    y_sorted = _experts(tile_expert, n_used, x_sorted, wg, wu, wd, layer, tm_e)
    n_sh = T // tm_e
    y_shared = _experts(jnp.zeros((n_sh,), jnp.int32), jnp.full((1,), n_sh, jnp.int32), h_tok,
                        swg[:, None], swu[:, None], swd[:, None], layer, tm_e)
    return _combine(idx, rank, row_start, wts, y_shared, x, gate2, y_sorted, norm_gain)


def _pad_cols(a, n):
    return jnp.pad(a, ((0, 0), (0, n - a.shape[1])))


def _pad_rows(a, n):
    return jnp.pad(a, ((0, n - a.shape[0]), (0, 0)))


def _pad_rwkv_cols(a, W, w_lora, a_lora):
    c0 = 3 * W
    return jnp.concatenate([a[:, :c0], _pad_cols(a[:, c0:c0 + w_lora], LORA_PAD),
                            _pad_cols(a[:, c0 + w_lora:c0 + w_lora + a_lora], LORA_PAD),
                            a[:, c0 + w_lora + a_lora:]], axis=1)


def kernel(x, c, w_mod, b_mod, norm_mix, norm_ffn, w_in, rwkv_mu, rwkv_w0, rwkv_w_up, rwkv_a0, rwkv_a_up, rwkv_g_up, rwkv_k_k, rwkv_k_a, rwkv_r_k, rwkv_gn_w, rwkv_gn_b, rwkv_v0, rwkv_v_down, rwkv_v_up, hgrn_lower_bounds, hgrn_norm, w_out, router_w, router_bias, expert_w_gate, expert_w_up, expert_w_down, shared_w_gate, shared_w_up, shared_w_down, final_norm):
    B, T, D = x.shape
    L = w_mod.shape[0]
    W = rwkv_w0.shape[1]
    w_lora, a_lora = rwkv_w_up.shape[1], rwkv_a_up.shape[1]
    rwkv_cols = rwkv_mu.shape[1]
    rwkv_cols_p = 3 * W + 2 * LORA_PAD + G_LORA

    lb_soft = jax.nn.softmax(hgrn_lower_bounds.astype(F32), axis=0)
    lbs = jnp.cumsum(lb_soft, axis=0) - lb_soft[0]
    mod = _modulation(c, w_mod, b_mod)

    xs = x.reshape(B * T, D)
    v_first = None
    for l in range(L):
        sh1, sc1, g1, sh2, sc2, g2 = [mod[l, n * D:(n + 1) * D].reshape(1, D) for n in range(6)]
        w_in_p = _pad_rwkv_cols(w_in[l], W, w_lora, a_lora).T.astype(BF16)
        p = _norm_proj(xs, norm_mix[l].reshape(1, D) * (1.0 + sc1), sh1, w_in_p)
        mu_p = _pad_rwkv_cols(rwkv_mu[l].reshape(1, -1), W, w_lora, a_lora)
        vres = None
        if l > 0:
            vres = (rwkv_v0[l - 1].reshape(1, W), _pad_cols(rwkv_v_down[l - 1], V_LORA_PAD).astype(BF16),
                    _pad_rows(rwkv_v_up[l - 1], V_LORA_PAD).astype(BF16), v_first)
        r, lw, k, v, kk, kka, g = _rwkv_prep(
            p, mu_p, rwkv_w0[l].reshape(1, W), _pad_rows(rwkv_w_up[l], LORA_PAD).astype(BF16),
            rwkv_a0[l].reshape(1, W), _pad_rows(rwkv_a_up[l], LORA_PAD).astype(BF16),
            rwkv_g_up[l].astype(BF16), rwkv_k_k[l].reshape(1, W), rwkv_k_a[l].reshape(1, W), vres)
        if l == 0:
            v_first = v
        y_r = _rwkv_recurrence(r, lw, k, v, kk, kka, g, rwkv_r_k[l].reshape(1, W),
                               rwkv_gn_w[l].reshape(1, W), rwkv_gn_b[l].reshape(1, W))
        y_h = _hgrn(p, rwkv_cols_p, lbs[l].reshape(1, -1), hgrn_norm[l].reshape(1, -1))
        xs = _out_proj(y_r, y_h, w_out[l].astype(BF16), xs, g1)
        xs = _moe(xs, norm_ffn[l].reshape(1, D) * (1.0 + sc2), sh2, g2, router_w[l].T,
                  router_bias[l].reshape(-1, 1), jnp.swapaxes(expert_w_gate, 2, 3),
                  jnp.swapaxes(expert_w_up, 2, 3), expert_w_down, jnp.swapaxes(shared_w_gate, 1, 2),
                  jnp.swapaxes(shared_w_up, 1, 2), shared_w_down, l,
                  final_norm.reshape(1, D) if l == L - 1 else None)
    return xs.reshape(B, T, D)
```

```python
import functools
import math

import jax
import jax.numpy as jnp
from jax import lax
from jax.experimental import pallas as pl
from jax.experimental.pallas import tpu as pltpu

F32 = jnp.float32
BF16 = jnp.bfloat16

LANES = 128
VMEM_LIMIT = 56 * 1024 * 1024

RWKV_HEAD = 64
HGRN_HEAD = 128
CHUNK = 64
SUB = 8
LORA_PAD = 128
G_LORA = 256
V_LORA_PAD = 128
N_EXPERTS = 64
N_GROUPS = 8
TOPK_GROUPS = 4
TOP_K = 8
ROUTE_SCALE = 2.5
MASKED_SCORE = -1e4
MIN_FORGET = 1e-30
NORM_EPS = 1e-6
GN_EPS = 64e-5
DECAY_SCALE = 0.6065306597126334

NN = (((1,), (0,)), ((), ()))
NT = (((1,), (1,)), ((), ()))
TN = (((0,), (0,)), ((), ()))


def _mm(a, b, dims=NN):
    return lax.dot_general(a.astype(BF16), b.astype(BF16), dims, preferred_element_type=F32)


def _mmx(a, b, dims=NN):
    return lax.dot_general(a.astype(F32), b.astype(F32), dims, preferred_element_type=F32,
                           precision=lax.Precision.HIGHEST)


def _split3(x):
    hi = x.astype(BF16)
    r1 = x - hi.astype(F32)
    mid = r1.astype(BF16)
    lo = (r1 - mid.astype(F32)).astype(BF16)
    return hi, mid, lo


def _mm_left01(a01, b, dims=NN):
    a = a01.astype(BF16)
    d = functools.partial(lax.dot_general, dimension_numbers=dims, preferred_element_type=F32)
    hi, mid, lo = _split3(b)
    return d(a, hi) + d(a, mid) + d(a, lo)


def _mm_right01(a, b01, dims=NN):
    b = b01.astype(BF16)
    d = functools.partial(lax.dot_general, dimension_numbers=dims, preferred_element_type=F32)
    hi, mid, lo = _split3(a)
    return d(hi, b) + d(mid, b) + d(lo, b)


def _sigmoid(x):
    return 1.0 / (1.0 + jnp.exp(-x))


def _silu(x):
    return x * _sigmoid(x)


def _cparams(sem):
    return pltpu.CompilerParams(dimension_semantics=sem, vmem_limit_bytes=VMEM_LIMIT)


def _mod_kernel(c_ref, w_ref, b_ref, o_ref):
    cond = _silu(c_ref[...])
    tn = w_ref.shape[2]
    cols = [jnp.sum(w_ref[0, :, j * LANES:(j + 1) * LANES] * cond, axis=0, keepdims=True)
            for j in range(tn // LANES)]
    o_ref[0] = jnp.concatenate(cols, axis=1) + b_ref[0]


def _modulation(c, w_mod, b_mod):
    L, D, N = w_mod.shape
    tn = 512
    c_lanes = jnp.broadcast_to(c.reshape(D, 1), (D, LANES))
    out = pl.pallas_call(
        _mod_kernel,
        grid=(L, N // tn),
        in_specs=[pl.BlockSpec((D, LANES), lambda l, j: (0, 0)),
                  pl.BlockSpec((1, D, tn), lambda l, j: (l, 0, j)),
                  pl.BlockSpec((1, 1, tn), lambda l, j: (l, 0, j))],
        out_specs=pl.BlockSpec((1, 1, tn), lambda l, j: (l, 0, j)),
        out_shape=jax.ShapeDtypeStruct((L, 1, N), F32),
        compiler_params=_cparams(("arbitrary", "arbitrary")),
        name="modulation",
    )(c_lanes, w_mod, b_mod.reshape(L, 1, N))
    return out[:, 0, :]


def _normmod(x, gain, shift):
    y = x * lax.rsqrt(jnp.mean(x * x, axis=-1, keepdims=True) + NORM_EPS)
    return y * gain + shift


def _norm_proj_kernel(x_ref, gain_ref, shift_ref, w_ref, o_ref, h_ref):
    @pl.when(pl.program_id(1) == 0)
    def _():
        h_ref[...] = _normmod(x_ref[...], gain_ref[...], shift_ref[...]).astype(BF16)

    o_ref[...] = lax.dot_general(h_ref[...], w_ref[...], NT, preferred_element_type=F32)


def _norm_proj(x, gain, shift, w_t, tm=1024, tn=512):
    T, D = x.shape
    N = w_t.shape[0]
    return pl.pallas_call(
        _norm_proj_kernel,
        grid=(T // tm, N // tn),
        in_specs=[pl.BlockSpec((tm, D), lambda i, j: (i, 0), pipeline_mode=pl.Buffered(1)),
                  pl.BlockSpec((1, D), lambda i, j: (0, 0)),
                  pl.BlockSpec((1, D), lambda i, j: (0, 0)),
                  pl.BlockSpec((tn, D), lambda i, j: (j, 0))],
        out_specs=pl.BlockSpec((tm, tn), lambda i, j: (i, j)),
        out_shape=jax.ShapeDtypeStruct((T, N), F32),
        scratch_shapes=[pltpu.VMEM((tm, D), BF16)],
        compiler_params=_cparams(("arbitrary", "arbitrary")),
        name="norm_proj",
    )(x, gain, shift, w_t)


def _pair_ones():
    r = lax.broadcasted_iota(jnp.int32, (LANES, LANES), 0) // RWKV_HEAD
    c = lax.broadcasted_iota(jnp.int32, (LANES, LANES), 1) // RWKV_HEAD
    return jnp.where(r == c, 1.0, 0.0).astype(F32)


def _rwkv_prep_kernel(has_vres, *refs):
    n_in = 20 if has_vres else 16
    (r_ref, k_ref, v_ref, lo_ref, rp_ref, kp_ref, vp_ref, lop_ref, mu_ref, w0_ref, wup_ref,
     a0_ref, aup_ref, gup_ref, kk_ref, ka_ref) = refs[:16]
    if has_vres:
        v0_ref, vdn_ref, vup_ref, vf_ref = refs[16:20]
    ro_ref, lwo_ref, ko_ref, vo_ref, kko_ref, kkao_ref, go_ref = refs[n_in:]
    first = pl.program_id(0) == 0
    W = r_ref.shape[1]

    def shift(cur_ref, prev_ref, mu):
        cur = cur_ref[...]
        prev_last = jnp.where(first, 0.0, prev_ref[7:8, :])
        rolled = pltpu.roll(cur, 1, axis=0)
        row = lax.broadcasted_iota(jnp.int32, cur.shape, 0)
        prev = jnp.where(row == 0, prev_last, rolled)
        return cur + (prev - cur) * mu

    r = shift(r_ref, rp_ref, mu_ref[:, 0:W])
    k = shift(k_ref, kp_ref, mu_ref[:, W:2 * W])
    v = shift(v_ref, vp_ref, mu_ref[:, 2 * W:3 * W])
    lo = shift(lo_ref, lop_ref, mu_ref[:, 3 * W:])
    w_lo = lo[:, 0:LORA_PAD]
    a_lo = lo[:, LORA_PAD:2 * LORA_PAD]
    g_lo = lo[:, 2 * LORA_PAD:]

    z = w0_ref[...] + _mm(jnp.tanh(w_lo), wup_ref[...])
    lw = -DECAY_SCALE * _sigmoid(z)
    a = _sigmoid(a0_ref[...] + _mm(a_lo, aup_ref[...]))
    g = _mm(_sigmoid(g_lo), gup_ref[...])
    if has_vres:
        mix = _sigmoid(v0_ref[...] + _mm(_mm(v, vdn_ref[...]), vup_ref[...]))
        v = v + (vf_ref[...] - v) * mix

    kkr = k * kk_ref[...]
    sq = kkr * kkr
    ones = _pair_ones()
    ss = jnp.concatenate(
        [_mm_right01(sq[:, j * LANES:(j + 1) * LANES], ones) for j in range(W // LANES)], axis=1)
    kk = kkr / jnp.maximum(jnp.sqrt(ss), 1e-12)
    k2 = k * (1.0 + (a - 1.0) * ka_ref[...])

    ro_ref[...] = r
    lwo_ref[...] = lw
    ko_ref[...] = k2
    vo_ref[...] = v
    kko_ref[...] = kk
    kkao_ref[...] = kk * a
    go_ref[...] = g


def _rwkv_prep(p, mu_p, w0, wup, a0, aup, gup, k_k, k_a, vres, tp=128):
    T = p.shape[0]
    W = w0.shape[1]
    lo_w = 2 * LORA_PAD + G_LORA
    nb = tp // 8
    lo_blk = 3 * W // lo_w

    def cur(cb, width):
        return pl.BlockSpec((tp, width), lambda i, cb=cb: (i, cb))

    def prev(cb, width):
        return pl.BlockSpec((8, width), lambda i, cb=cb: (jnp.maximum(i * nb - 1, 0), cb))

    def vec(n):
        return pl.BlockSpec((1, n), lambda i: (0, 0))

    def mat(a, b):
        return pl.BlockSpec((a, b), lambda i: (0, 0))

    in_specs = [cur(0, W), cur(1, W), cur(2, W), cur(lo_blk, lo_w),
                prev(0, W), prev(1, W), prev(2, W), prev(lo_blk, lo_w),
                vec(3 * W + lo_w), vec(W), mat(LORA_PAD, W), vec(W), mat(LORA_PAD, W),
                mat(G_LORA, W), vec(W), vec(W)]
    args = [p, p, p, p, p, p, p, p, mu_p, w0, wup, a0, aup, gup, k_k, k_a]
    if vres is not None:
        v0, vdn, vup, v_first = vres
        in_specs += [vec(W), mat(W, V_LORA_PAD), mat(V_LORA_PAD, W),
                     pl.BlockSpec((tp, W), lambda i: (i, 0))]
        args += [v0, vdn, vup, v_first]
    out_spec = pl.BlockSpec((tp, W), lambda i: (i, 0))
    sds = jax.ShapeDtypeStruct((T, W), F32)
    return pl.pallas_call(
        functools.partial(_rwkv_prep_kernel, vres is not None),
        grid=(T // tp,),
        in_specs=in_specs,
        out_specs=[out_spec] * 7,
        out_shape=[sds] * 7,
        compiler_params=_cparams(("arbitrary",)),
        name="rwkv_prep",
    )(*args)


def _tri_incl(n):
    row = lax.broadcasted_iota(jnp.int32, (n, n), 0)
    col = lax.broadcasted_iota(jnp.int32, (n, n), 1)
    return jnp.where(col <= row, 1.0, 0.0).astype(BF16)


def _rwkv_pair(r, c, lw, k, v, kk, kka, g, rk, gnw, gnb, st):
    C = r.shape[0]
    C2 = 2 * C
    cp = c - lw
    c_last = c[C - 1:C, :]
    e_c, e_cp, e_nc, e_dl = jnp.exp(c), jnp.exp(cp), jnp.exp(-c), jnp.exp(c_last - c)

    lane = lax.broadcasted_iota(jnp.int32, (C, LANES), 1)
    m0 = lane < RWKV_HEAD

    def stack(x):
        return jnp.concatenate([jnp.where(m0, x, 0.0), jnp.where(m0, 0.0, x)], axis=0)

    aq_rt = jnp.concatenate([stack(kk * e_cp), stack(r * e_c)], axis=0).astype(BF16)
    bt_kt = jnp.concatenate([stack(kka * e_nc), stack(k * e_nc)], axis=0).astype(BF16)
    kw_bw = jnp.concatenate([stack(k * e_dl), stack(kka * e_dl)], axis=0).astype(BF16)
    vs = stack(v)
    vs_b = vs.astype(BF16)

    row = lax.broadcasted_iota(jnp.int32, (C2, C2), 0)
    col = lax.broadcasted_iota(jnp.int32, (C2, C2), 1)
    t_i, s_j = row % C, col % C
    strict, incl, eye = s_j < t_i, s_j <= t_i, row == col

    yield
    p = _mm(aq_rt, bt_kt, NT)
    yield
    l_ab = jnp.where(strict, p[:C2, :C2], 0.0)
    l_ak = jnp.where(strict, p[:C2, C2:], 0.0)
    l_rb = jnp.where(incl, p[C2:, :C2], 0.0)
    l_rk = jnp.where(incl, p[C2:, C2:], 0.0)

    pw = -l_ab
    inv = jnp.where(eye, 1.0, 0.0) + pw
    n = 2
    while n < C:
        pw = _mm(pw, pw)
        yield
        inv = inv + _mm(inv, pw)
        n *= 2

    xy = _mm(aq_rt, st) + _mm(jnp.concatenate([l_ak, l_rk], axis=0), vs_b)
    yield
    u = _mm(inv, xy[:C2])
    yield
    y = xy[C2:] - _mm(l_rb, u)
    w_col = jnp.broadcast_to(jnp.exp(c_last), (C2, C2)).T
    st_new = st * w_col + _mm(kw_bw, jnp.concatenate([vs, -u], axis=0), TN)
    yield

    mh = (row // C) == (col // RWKV_HEAD)
    inv_n = 1.0 / RWKV_HEAD
    mean = jnp.sum(y, axis=1, keepdims=True) * inv_n
    yc = jnp.where(mh, y - mean, 0.0)
    var = jnp.sum(yc * yc, axis=1, keepdims=True) * inv_n
    yn = yc * lax.rsqrt(var + GN_EPS)
    bonus = jnp.sum(stack(r * k * rk), axis=1, keepdims=True) * vs
    yn_t = yn[:C] + yn[C:]
    bonus_t = bonus[:C] + bonus[C:]
    return (yn_t * gnw + gnb + bonus_t) * g, st_new


def _interleave(gens):
    results = [None] * len(gens)
    live = list(enumerate(gens))
    while live:
        still = []
        for j, gen in live:
            try:
                next(gen)
                still.append((j, gen))
            except StopIteration as stop:
                results[j] = stop.value
        live = still
    return results


def _rwkv_rec_kernel(r_ref, lw_ref, k_ref, v_ref, kk_ref, kka_ref, g_ref, rk_ref, gnw_ref, gnb_ref,
                     o_ref, st_ref):
    @pl.when(pl.program_id(1) == 0)
    def _():
        st_ref[...] = jnp.zeros_like(st_ref)

    rows, lanes = r_ref.shape
    tri = _tri_incl(CHUNK)

    def chunk(ci, carry):
        rs = pl.ds(pl.multiple_of(ci * CHUNK, CHUNK), CHUNK)
        lw_all = lw_ref[rs, :]
        c_all = _mm_left01(tri, lw_all)
        r, k, v, kk, kka, g = (ref[rs, :] for ref in (r_ref, k_ref, v_ref, kk_ref, kka_ref, g_ref))
        rk, gnw, gnb = rk_ref[...], gnw_ref[...], gnb_ref[...]
        gens = []
        for j in range(lanes // LANES):
            sl = slice(j * LANES, (j + 1) * LANES)
            gens.append(_rwkv_pair(r[:, sl], c_all[:, sl], lw_all[:, sl], k[:, sl], v[:, sl], kk[:, sl],
                                   kka[:, sl], g[:, sl], rk[:, sl], gnw[:, sl], gnb[:, sl], st_ref[j]))
        results = _interleave(gens)
        for j, (_, st_new) in enumerate(results):
            st_ref[j] = st_new
        o_ref[rs, :] = jnp.concatenate([out for out, _ in results], axis=1).astype(o_ref.dtype)
        return carry

    lax.fori_loop(0, rows // CHUNK, chunk, 0)


def _rwkv_recurrence(r, lw, k, v, kk, kka, g, rk, gnw, gnb, hb_lanes=1024, rows=512):
    T, W = r.shape
    rows = min(rows, T)
    seq = pl.BlockSpec((rows, hb_lanes), lambda h, t: (t, h))
    vec = pl.BlockSpec((1, hb_lanes), lambda h, t: (0, h))
    return pl.pallas_call(
        _rwkv_rec_kernel,
        grid=(W // hb_lanes, T // rows),
        in_specs=[seq] * 7 + [vec] * 3,
        out_specs=seq,
        out_shape=jax.ShapeDtypeStruct((T, W), BF16),
        scratch_shapes=[pltpu.VMEM((hb_lanes // LANES, LANES, LANES), F32)],
        compiler_params=_cparams(("arbitrary", "arbitrary")),
        name="rwkv_recurrence",
    )(r, lw, k, v, kk, kka, g, rk, gnw, gnb)


def _hgrn_head(q_raw, f_raw, i_c, g_raw, lb, gain, tri, s):
    C = q_raw.shape[0]
    q = _silu(q_raw)
    forget = lb + (1.0 - lb) * _sigmoid(f_raw)
    lf = jnp.log(jnp.maximum(forget, MIN_FORGET))
    kin = (1.0 - lb) * _sigmoid(-f_raw)
    b = _mm_left01(tri, lf)
    yield
    o = _mm(q * jnp.exp(b), s)

    lane_s = lax.broadcasted_iota(jnp.int32, (SUB, C), 1)
    row_s = lax.broadcasted_iota(jnp.int32, (SUB, C), 0)
    score_rows = []
    for i in range(C // SUB):
        lo, hi = i * SUB, (i + 1) * SUB
        b_i, q_i = b[lo:hi], q[lo:hi]
        blk = jnp.zeros((SUB, C), F32)
        for sidx in range(SUB):
            srow = lo + sidx
            d = jnp.exp(b_i - b[srow:srow + 1]) * q_i * kin[srow:srow + 1]
            blk = jnp.where(lane_s == srow, jnp.sum(d, axis=1, keepdims=True), blk)
        blk = jnp.where(lane_s <= row_s + lo, blk, 0.0)
        yield
        if i > 0:
            b_st = b[lo - 1:lo]
            qs = q_i * jnp.exp(b_i - b_st)
            ks = kin * jnp.exp(jnp.minimum(b_st - b, 0.0))
            blk = blk + jnp.where(lane_s < lo, _mm(qs, ks, NT), 0.0)
        score_rows.append(blk)
    scores = jnp.concatenate(score_rows, axis=0)
    o = o + _mm(scores, i_c)

    b_last = b[C - 1:C]
    f_col = jnp.broadcast_to(jnp.exp(b_last), (LANES, LANES)).T
    s_new = s * f_col + _mm(kin * jnp.exp(b_last - b), i_c, TN)

    o = o * lax.rsqrt(jnp.mean(o * o, axis=1, keepdims=True) + NORM_EPS)
    return o * gain * _sigmoid(g_raw), s_new


def _hgrn_kernel(q_ref, f_ref, i_ref, g_ref, lb_ref, gain_ref, o_ref, s_ref):
    @pl.when(pl.program_id(1) == 0)
    def _():
        s_ref[...] = jnp.zeros_like(s_ref)

    rows, lanes = q_ref.shape
    tri = _tri_incl(CHUNK)

    def chunk(ci, carry):
        rs = pl.ds(pl.multiple_of(ci * CHUNK, CHUNK), CHUNK)
        q, f, i_c, g = (ref[rs, :] for ref in (q_ref, f_ref, i_ref, g_ref))
        lb, gain = lb_ref[...], gain_ref[...]
        gens = []
        for j in range(lanes // LANES):
            sl = slice(j * LANES, (j + 1) * LANES)
            gens.append(_hgrn_head(q[:, sl], f[:, sl], i_c[:, sl], g[:, sl], lb[:, sl], gain[:, sl],
                                   tri, s_ref[j]))
        results = _interleave(gens)
        for j, (_, s_new) in enumerate(results):
            s_ref[j] = s_new
        o_ref[rs, :] = jnp.concatenate([out for out, _ in results], axis=1).astype(o_ref.dtype)
        return carry

    lax.fori_loop(0, rows // CHUNK, chunk, 0)


def _hgrn(p, col0, lb, gain, hb_lanes=512, rows=512):
    T = p.shape[0]
    W = lb.shape[1]
    rows = min(rows, T)
    nb = W // hb_lanes
    b0 = col0 // hb_lanes

    def sec(n):
        return pl.BlockSpec((rows, hb_lanes), lambda h, t, n=n: (t, b0 + n * nb + h))

    vec = pl.BlockSpec((1, hb_lanes), lambda h, t: (0, h))
    return pl.pallas_call(
        _hgrn_kernel,
        grid=(nb, T // rows),
        in_specs=[sec(0), sec(1), sec(2), sec(3), vec, vec],
        out_specs=pl.BlockSpec((rows, hb_lanes), lambda h, t: (t, h)),
        out_shape=jax.ShapeDtypeStruct((T, W), BF16),
        scratch_shapes=[pltpu.VMEM((hb_lanes // LANES, LANES, LANES), F32)],
        compiler_params=_cparams(("arbitrary", "arbitrary")),
        name="hgrn2",
    )(p, p, p, p, lb, gain)


def _out_proj_kernel(ya_ref, yb_ref, wa_ref, wb_ref, x_ref, gate_ref, o_ref):
    acc = jnp.dot(ya_ref[...], wa_ref[...], preferred_element_type=F32)
    acc = acc + jnp.dot(yb_ref[...], wb_ref[...], preferred_element_type=F32)
    o_ref[...] = x_ref[...] + gate_ref[...] * acc


def _out_proj(ya, yb, w, x, gate, tm=512, tn=1024):
    T, Ka = ya.shape
    Kb = yb.shape[1]
    D = w.shape[1]
    return pl.pallas_call(
        _out_proj_kernel,
        grid=(T // tm, D // tn),
        in_specs=[pl.BlockSpec((tm, Ka), lambda i, j: (i, 0)),
                  pl.BlockSpec((tm, Kb), lambda i, j: (i, 0)),
                  pl.BlockSpec((Ka, tn), lambda i, j: (0, j)),
                  pl.BlockSpec((Kb, tn), lambda i, j: (Ka // Kb, j)),
                  pl.BlockSpec((tm, tn), lambda i, j: (i, j)),
                  pl.BlockSpec((1, tn), lambda i, j: (0, j))],
        out_specs=pl.BlockSpec((tm, tn), lambda i, j: (i, j)),
        out_shape=jax.ShapeDtypeStruct((T, D), F32),
        compiler_params=_cparams(("arbitrary", "arbitrary")),
        name="out_proj",
    )(ya, yb, w, w, x, gate)


def _first_max_onehot(work, axis, size):
    m = jnp.max(work, axis=axis, keepdims=True)
    idx = lax.broadcasted_iota(jnp.int32, work.shape, axis)
    first = jnp.min(jnp.where(work == m, idx, size), axis=axis, keepdims=True)
    return idx == first, m


def _pitch(s_per):
    return s_per + 8 if s_per % 16 == 0 else s_per


def _pack_bf16_pairs(h):
    half = h.shape[1] // 2
    bits = lambda v: lax.bitcast_convert_type(v.astype(BF16).astype(F32), jnp.uint32)
    word = (bits(h[:, :half]) >> 16) | (bits(h[:, half:]) & jnp.uint32(0xFFFF0000))
    return lax.bitcast_convert_type(word, jnp.int32)


def _unpack_bf16_pairs(word, dtype=BF16):
    u = lax.bitcast_convert_type(word, jnp.uint32)
    lo = lax.bitcast_convert_type(u << 16, F32).astype(dtype)
    hi = lax.bitcast_convert_type(u & jnp.uint32(0xFFFF0000), F32).astype(dtype)
    return lo, hi


def _store_token_major(ref, val):
    n, d = val.shape
    pitch = ref.shape[0] // n
    for s in range(d // LANES):
        ref[pl.ds(s, n, stride=pitch), :] = val[:, s * LANES:(s + 1) * LANES]
    for s in range(d // LANES, pitch):
        ref[pl.ds(s, n, stride=pitch), :] = jnp.zeros((n, LANES), ref.dtype)


def _router_kernel(x_ref, gain_ref, shift_ref, rwt_ref, bias_ref,
                   h_ref, wts_ref, idx_ref, rank_ref, cnt_ref, carry_ref):
    @pl.when(pl.program_id(0) == 0)
    def _():
        carry_ref[...] = jnp.zeros_like(carry_ref)

    h = _normmod(x_ref[...], gain_ref[...], shift_ref[...])
    _store_token_major(h_ref, _pack_bf16_pairs(h))
    tm = h.shape[0]
    per_group = N_EXPERTS // N_GROUPS
    logits = _mmx(rwt_ref[...], h, NT)
    scores = _sigmoid(logits)
    biased = scores + bias_ref[...]

    b3 = biased.reshape(N_GROUPS, per_group, tm)
    pick1, m1 = _first_max_onehot(b3, 1, per_group)
    m2 = jnp.max(jnp.where(pick1, -jnp.inf, b3), axis=1, keepdims=True)
    gscore = (m1 + m2).reshape(N_GROUPS, tm)

    work = gscore
    gsel = jnp.zeros_like(gscore)
    for _ in range(TOPK_GROUPS):
        pick, _m = _first_max_onehot(work, 0, N_GROUPS)
        gsel = jnp.where(pick, 1.0, gsel)
        work = jnp.where(pick, -jnp.inf, work)
    ok = jnp.broadcast_to(gsel.reshape(N_GROUPS, 1, tm), (N_GROUPS, per_group, tm)).reshape(N_EXPERTS, tm)
    work = jnp.where(ok > 0.5, biased, MASKED_SCORE)
    picks = []
    for _ in range(TOP_K):
        pick, _m = _first_max_onehot(work, 0, N_EXPERTS)
        picks.append(pick)
        work = jnp.where(pick, -jnp.inf, work)
    picked = functools.reduce(lambda a, b: a + b, [jnp.where(p, 1.0, 0.0) for p in picks])
    sel = picked * scores
    gates_t = sel / jnp.sum(sel, axis=0, keepdims=True) * ROUTE_SCALE

    ra = lax.broadcasted_iota(jnp.int32, (tm, tm), 0)
    rb = lax.broadcasted_iota(jnp.int32, (tm, tm), 1)
    before = jnp.where(ra < rb, 1.0, 0.0).astype(BF16)
    carry = carry_ref[:, 0:1]
    rank_full = _mm(picked, before) + carry
    e_iota = lax.broadcasted_iota(jnp.int32, (N_EXPERTS, tm), 0).astype(F32)

    def slot_rows(table):
        rows = [jnp.sum(jnp.where(p, table, 0.0), axis=0, keepdims=True) for p in picks]
        return jnp.concatenate(rows, axis=0)

    idx_ref[...] = slot_rows(e_iota).astype(jnp.int32)
    rank_ref[...] = slot_rows(rank_full).astype(jnp.int32)
    wts_ref[...] = slot_rows(gates_t)
    new_carry = carry + jnp.sum(picked, axis=1, keepdims=True)
    carry_ref[...] = jnp.broadcast_to(new_carry, carry_ref.shape)
    cnt_ref[...] = jnp.broadcast_to(new_carry, cnt_ref.shape)


def _router(x, gain, shift, rw_t, bias, tm=256):
    T, D = x.shape
    tm = min(tm, T)
    P = _pitch(D // 2 // LANES)
    return pl.pallas_call(
        _router_kernel,
        grid=(T // tm,),
        in_specs=[pl.BlockSpec((tm, D), lambda i: (i, 0)),
                  pl.BlockSpec((1, D), lambda i: (0, 0)),
                  pl.BlockSpec((1, D), lambda i: (0, 0)),
                  pl.BlockSpec((N_EXPERTS, D), lambda i: (0, 0)),
                  pl.BlockSpec((N_EXPERTS, 1), lambda i: (0, 0))],
        out_specs=[pl.BlockSpec((tm * P, LANES), lambda i: (i, 0)),
                   pl.BlockSpec((TOP_K, tm), lambda i: (0, i)),
                   pl.BlockSpec((TOP_K, tm), lambda i: (0, i)),
                   pl.BlockSpec((TOP_K, tm), lambda i: (0, i)),
                   pl.BlockSpec((N_EXPERTS, LANES), lambda i: (0, 0))],
        out_shape=[jax.ShapeDtypeStruct((T * P, LANES), jnp.int32),
                   jax.ShapeDtypeStruct((TOP_K, T), F32),
                   jax.ShapeDtypeStruct((TOP_K, T), jnp.int32),
                   jax.ShapeDtypeStruct((TOP_K, T), jnp.int32),
                   jax.ShapeDtypeStruct((N_EXPERTS, LANES), F32)],
        scratch_shapes=[pltpu.VMEM((N_EXPERTS, LANES), F32)],
        compiler_params=_cparams(("arbitrary",)),
        name="router",
    )(x, gain, shift, rw_t, bias)


def _dispatch_kernel(P, idx_ref, rank_ref, row_start_ref, pad_start_ref, pad_len_ref, nu_ref, h_ref, xs_hbm,
                     zero_ref, sem, pad_sem):
    i = pl.program_id(0)
    tc = idx_ref.shape[1]
    tile_rows = zero_ref.shape[0]

    def slab(ref, row):
        return ref.at[pl.ds(pl.multiple_of(row * P, math.gcd(P, 8)), P)]

    def token(n, carry):
        src = slab(h_ref, n)
        for j in range(TOP_K):
            dest = row_start_ref[idx_ref[j, n]] + rank_ref[j, n]
            pltpu.make_async_copy(src, slab(xs_hbm, dest), sem).start(priority=j % 2)
        return carry

    lax.fori_loop(0, tc, token, 0)

    @pl.when(i == 0)
    def _():
        zero_ref[...] = jnp.zeros_like(zero_ref)
        zero_slab = zero_ref.at[pl.ds(0, P)]

        def expert(e, carry):
            def fill(r, c):
                pltpu.make_async_copy(zero_slab, slab(xs_hbm, pad_start_ref[e] + r), pad_sem).start()
                return c

            def drain(r, c):
                pltpu.make_async_copy(zero_slab, slab(xs_hbm, 0), pad_sem).wait()
                return c

            lax.fori_loop(0, pad_len_ref[e], fill, 0)
            lax.fori_loop(0, pad_len_ref[e], drain, 0)
            return carry

        lax.fori_loop(0, N_EXPERTS, expert, 0)

        def tile_dst(t):
            return xs_hbm.at[pl.ds(pl.multiple_of(t * tile_rows, 8), tile_rows)]

        def fill_tile(t, c):
            pltpu.make_async_copy(zero_ref, tile_dst(t), pad_sem).start()
            return c

        def drain_tile(t, c):
            pltpu.make_async_copy(zero_ref, tile_dst(0), pad_sem).wait()
            return c

        n_tiles = xs_hbm.shape[0] // tile_rows
        lax.fori_loop(nu_ref[0], n_tiles, fill_tile, 0)
        lax.fori_loop(nu_ref[0], n_tiles, drain_tile, 0)

    for j in range(TOP_K):
        pltpu.make_async_copy(h_ref, xs_hbm.at[pl.ds(0, tc * P)], sem).wait()


def _dispatch(idx, rank, row_start, pad_start, pad_len, n_used, h_tok, n_tiles, tm_e, tc):
    T = idx.shape[1]
    P = h_tok.shape[0] // T
    smem = pl.BlockSpec(memory_space=pltpu.SMEM)
    slots = pl.BlockSpec((TOP_K, tc), lambda i: (0, i), memory_space=pltpu.SMEM)
    return pl.pallas_call(
        functools.partial(_dispatch_kernel, P),
        grid=(T // tc,),
        in_specs=[slots, slots, smem, smem, smem, smem,
                  pl.BlockSpec((tc * P, LANES), lambda i: (i, 0))],
        out_specs=pl.BlockSpec(memory_space=pl.ANY),
        out_shape=jax.ShapeDtypeStruct((n_tiles * tm_e * P, LANES), h_tok.dtype),
        scratch_shapes=[pltpu.VMEM((tm_e * P, LANES), h_tok.dtype), pltpu.SemaphoreType.DMA,
                        pltpu.SemaphoreType.DMA],
        compiler_params=_cparams(("arbitrary",)),
        name="dispatch",
    )(idx, rank, row_start, pad_start, pad_len, n_used, h_tok)


def _experts_kernel(te_ref, nu_ref, x_ref, wg_ref, wu_ref, wd_ref, o_ref):
    sx = wg_ref.shape[3] // LANES // 2
    px = _pitch(sx)
    tm = x_ref.shape[0] // px
    kx = 8 if sx % 8 == 0 else sx

    @pl.when(pl.program_id(0) >= nu_ref[0])
    def _():
        o_ref[...] = jnp.zeros_like(o_ref)

    @pl.when(pl.program_id(0) < nu_ref[0])
    def _():
        g = u = None
        for c0 in range(0, sx, kx):
            words = jnp.concatenate([x_ref[pl.ds(s, tm, stride=px), :] for s in range(c0, c0 + kx)], axis=1)
            for xk, col0 in zip(_unpack_bf16_pairs(words), (c0, sx + c0)):
                ks = slice(col0 * LANES, (col0 + kx) * LANES)
                gk = lax.dot_general(xk, wg_ref[0, 0, :, ks].astype(BF16), NT, preferred_element_type=F32)
                uk = lax.dot_general(xk, wu_ref[0, 0, :, ks].astype(BF16), NT, preferred_element_type=F32)
                g = gk if g is None else g + gk
                u = uk if u is None else u + uk
        act = (_silu(g) * u).astype(BF16)
        for c0 in range(0, sx, kx):
            halves = []
            for col0 in (c0, sx + c0):
                ns = slice(col0 * LANES, (col0 + kx) * LANES)
                halves.append(jnp.dot(act, wd_ref[0, 0, :, ns].astype(BF16), preferred_element_type=F32))
            words = _pack_bf16_pairs(jnp.concatenate(halves, axis=1))
            for s in range(kx):
                o_ref[pl.ds(c0 + s, tm, stride=px), :] = words[:, s * LANES:(s + 1) * LANES]
        for s in range(sx, px):
            o_ref[pl.ds(s, tm, stride=px), :] = jnp.zeros((tm, LANES), o_ref.dtype)


def _experts(tile_expert, n_used, x_tok, wg, wu, wd, layer, tm):
    _, E, ff, D = wg.shape
    px = _pitch(D // 2 // LANES)
    n_tiles = x_tok.shape[0] // (tm * px)

    def row_map(i, te, nu):
        return (jnp.minimum(i, nu[0] - 1), 0)

    def w_map(i, te, nu):
        return (layer, te[i], 0, 0)

    grid_spec = pltpu.PrefetchScalarGridSpec(
        num_scalar_prefetch=2,
        grid=(n_tiles,),
        in_specs=[pl.BlockSpec((tm * px, LANES), row_map),
                  pl.BlockSpec((1, 1, ff, D), w_map),
                  pl.BlockSpec((1, 1, ff, D), w_map),
                  pl.BlockSpec((1, 1, ff, D), w_map)],
        out_specs=pl.BlockSpec((tm * px, LANES), lambda i, te, nu: (i, 0)),
    )
    return pl.pallas_call(
        _experts_kernel,
        grid_spec=grid_spec,
        out_shape=jax.ShapeDtypeStruct(x_tok.shape, jnp.int32),
        compiler_params=_cparams(("arbitrary",)),
        name="experts",
    )(tile_expert, n_used, x_tok, wg, wu, wd)


def _combine_kernel(has_norm, idx_ref, rank_ref, row_start_ref, w_ref, ysh_ref, x_ref, gate2_ref, *rest):
    if has_norm:
        norm_gain_ref, y_hbm, o_ref, buf_ref, acc_ref, sem = rest
    else:
        y_hbm, o_ref, buf_ref, acc_ref, sem = rest
    tc, D = x_ref.shape
    S = D // LANES
    P = _pitch(S)
    sy = S // 2
    py = _pitch(sy)
    align = math.gcd(py, 8)
    slot_rows = tc * py

    def token(n, carry):
        for j in range(TOP_K):
            dest = row_start_ref[idx_ref[j, n]] + rank_ref[j, n]
            src = y_hbm.at[pl.ds(pl.multiple_of(dest * py, align), sy)]
            dst = buf_ref.at[pl.ds(pl.multiple_of(j * slot_rows + n * py, align), sy)]
            pltpu.make_async_copy(src, dst, sem).start(priority=j % 2)
        return carry

    lax.fori_loop(0, tc, token, 0)
    n_rows = TOP_K * tc * sy
    pltpu.make_async_copy(y_hbm.at[pl.ds(0, n_rows)], buf_ref.at[pl.ds(0, n_rows)], sem).wait()

    def accumulate(n, carry):
        lo, hi = _unpack_bf16_pairs(ysh_ref[pl.ds(pl.multiple_of(n * py, align), sy), :], F32)
        for j in range(TOP_K):
            words = buf_ref[pl.ds(pl.multiple_of(j * slot_rows + n * py, align), sy), :]
            slab_lo, slab_hi = _unpack_bf16_pairs(words, F32)
            lo = lo + w_ref[j, n] * slab_lo
            hi = hi + w_ref[j, n] * slab_hi
        row = pl.multiple_of(n * P, math.gcd(P, 8))
        acc_ref[pl.ds(row, sy), :] = lo
        acc_ref[pl.ds(row + sy, sy), :] = hi
        return carry

    lax.fori_loop(0, tc, accumulate, 0, unroll=2)
    for s in range(S):
        ls = slice(s * LANES, (s + 1) * LANES)
        o_ref[:, ls] = x_ref[:, ls] + gate2_ref[:, ls] * acc_ref[pl.ds(s, tc, stride=P), :]
    if has_norm:
        xo = o_ref[...]
        o_ref[...] = xo * lax.rsqrt(jnp.mean(xo * xo, axis=-1, keepdims=True) + NORM_EPS) * norm_gain_ref[...]


def _combine(idx, rank, row_start, wts, y_shared, x, gate2, y_sorted, norm_gain=None, tc=128):
    T, D = x.shape
    tc = min(tc, T)
    P = _pitch(D // LANES)
    py = _pitch(D // 2 // LANES)
    slots = pl.BlockSpec((TOP_K, tc), lambda i: (0, i), memory_space=pltpu.SMEM)
    vec = pl.BlockSpec((1, D), lambda i: (0, 0))
    in_specs = [slots, slots, pl.BlockSpec(memory_space=pltpu.SMEM), slots,
                pl.BlockSpec((tc * py, LANES), lambda i: (i, 0)),
                pl.BlockSpec((tc, D), lambda i: (i, 0)), vec]
    args = [idx, rank, row_start, wts, y_shared, x, gate2]
    if norm_gain is not None:
        in_specs.append(vec)
        args.append(norm_gain)
    in_specs.append(pl.BlockSpec(memory_space=pl.ANY))
    args.append(y_sorted)
    return pl.pallas_call(
        functools.partial(_combine_kernel, norm_gain is not None),
        grid=(T // tc,),
        in_specs=in_specs,
        out_specs=pl.BlockSpec((tc, D), lambda i: (i, 0)),
        out_shape=jax.ShapeDtypeStruct((T, D), F32),
        scratch_shapes=[pltpu.VMEM((TOP_K * tc * py, LANES), jnp.int32), pltpu.VMEM((tc * P, LANES), F32),
                        pltpu.SemaphoreType.DMA],
        compiler_params=_cparams(("arbitrary",)),
        name="combine",
    )(*args)


def _moe(x, gain, shift, gate2, rw_t, bias, wg, wu, wd, swg, swu, swd, layer, norm_gain=None, tm_e=256,
         tc_d=256):
    T, D = x.shape
    tm_e = min(tm_e, T)
    tc_d = min(tc_d, T)
    h_tok, wts, idx, rank, cnt = _router(x, gain, shift, rw_t, bias)
    counts = cnt[:, 0].astype(jnp.int32)
    tiles = (counts + tm_e - 1) // tm_e
    tiles_cum = jnp.cumsum(tiles)
    row_start = (tiles_cum - tiles) * tm_e
    n_tiles = (T * TOP_K) // tm_e + N_EXPERTS
    tile_ids = jnp.arange(n_tiles, dtype=jnp.int32)
    tile_expert = jnp.minimum(jnp.sum((tiles_cum[None, :] <= tile_ids[:, None]).astype(jnp.int32), axis=1),
                              N_EXPERTS - 1)
    n_used = tiles_cum[-1:].astype(jnp.int32)
    x_sorted = _dispatch(idx, rank, row_start, row_start + counts, tiles * tm_e - counts, n_used, h_tok,
                         n_tiles, tm_e, tc_d)
    y_sorted = _experts(tile_expert, n_used, x_sorted, wg, wu, wd, layer, tm_e)
    n_sh = T // tm_e
    y_shared = _experts(jnp.zeros((n_sh,), jnp.int32), jnp.full((1,), n_sh, jnp.int32), h_tok,
                        swg[:, None], swu[:, None], swd[:, None], layer, tm_e)
    return _combine(idx, rank, row_start, wts, y_shared, x, gate2, y_sorted, norm_gain)


def _pad_cols(a, n):
    return jnp.pad(a, ((0, 0), (0, n - a.shape[1])))


def _pad_rows(a, n):
    return jnp.pad(a, ((0, n - a.shape[0]), (0, 0)))


def _pad_rwkv_cols(a, W, w_lora, a_lora):
    c0 = 3 * W
    return jnp.concatenate([a[:, :c0], _pad_cols(a[:, c0:c0 + w_lora], LORA_PAD),
                            _pad_cols(a[:, c0 + w_lora:c0 + w_lora + a_lora], LORA_PAD),
                            a[:, c0 + w_lora + a_lora:]], axis=1)


def kernel(x, c, w_mod, b_mod, norm_mix, norm_ffn, w_in, rwkv_mu, rwkv_w0, rwkv_w_up, rwkv_a0, rwkv_a_up, rwkv_g_up, rwkv_k_k, rwkv_k_a, rwkv_r_k, rwkv_gn_w, rwkv_gn_b, rwkv_v0, rwkv_v_down, rwkv_v_up, hgrn_lower_bounds, hgrn_norm, w_out, router_w, router_bias, expert_w_gate, expert_w_up, expert_w_down, shared_w_gate, shared_w_up, shared_w_down, final_norm):
    B, T, D = x.shape
    L = w_mod.shape[0]
    W = rwkv_w0.shape[1]
    w_lora, a_lora = rwkv_w_up.shape[1], rwkv_a_up.shape[1]
    rwkv_cols = rwkv_mu.shape[1]
    rwkv_cols_p = 3 * W + 2 * LORA_PAD + G_LORA

    lb_soft = jax.nn.softmax(hgrn_lower_bounds.astype(F32), axis=0)
    lbs = jnp.cumsum(lb_soft, axis=0) - lb_soft[0]
    mod = _modulation(c, w_mod, b_mod)

    xs = x.reshape(B * T, D)
    v_first = None
    for l in range(L):
        sh1, sc1, g1, sh2, sc2, g2 = [mod[l, n * D:(n + 1) * D].reshape(1, D) for n in range(6)]
        w_in_p = _pad_rwkv_cols(w_in[l], W, w_lora, a_lora).T.astype(BF16)
        p = _norm_proj(xs, norm_mix[l].reshape(1, D) * (1.0 + sc1), sh1, w_in_p)
        mu_p = _pad_rwkv_cols(rwkv_mu[l].reshape(1, -1), W, w_lora, a_lora)
        vres = None
        if l > 0:
            vres = (rwkv_v0[l - 1].reshape(1, W), _pad_cols(rwkv_v_down[l - 1], V_LORA_PAD).astype(BF16),
                    _pad_rows(rwkv_v_up[l - 1], V_LORA_PAD).astype(BF16), v_first)
        r, lw, k, v, kk, kka, g = _rwkv_prep(
            p, mu_p, rwkv_w0[l].reshape(1, W), _pad_rows(rwkv_w_up[l], LORA_PAD).astype(BF16),
            rwkv_a0[l].reshape(1, W), _pad_rows(rwkv_a_up[l], LORA_PAD).astype(BF16),
            rwkv_g_up[l].astype(BF16), rwkv_k_k[l].reshape(1, W), rwkv_k_a[l].reshape(1, W), vres)
        if l == 0:
            v_first = v
        y_r = _rwkv_recurrence(r, lw, k, v, kk, kka, g, rwkv_r_k[l].reshape(1, W),
                               rwkv_gn_w[l].reshape(1, W), rwkv_gn_b[l].reshape(1, W))
        y_h = _hgrn(p, rwkv_cols_p, lbs[l].reshape(1, -1), hgrn_norm[l].reshape(1, -1))
        xs = _out_proj(y_r, y_h, w_out[l].astype(BF16), xs, g1)
        xs = _moe(xs, norm_ffn[l].reshape(1, D) * (1.0 + sc2), sh2, g2, router_w[l].T,
                  router_bias[l].reshape(-1, 1), jnp.swapaxes(expert_w_gate, 2, 3),
                  jnp.swapaxes(expert_w_up, 2, 3), expert_w_down, jnp.swapaxes(shared_w_gate, 1, 2),
                  jnp.swapaxes(shared_w_up, 1, 2), shared_w_down, l,
                  final_norm.reshape(1, D) if l == L - 1 else None)
    return xs.reshape(B, T, D)
```
